```python
import jax, jax.numpy as jnp
from jax import lax
import numpy as np

D_MODEL = 1024
BATCH = 8
SEQ = 2048
DEPTH = 1

N_META = 16
GRID_W = 64
NA_WIN_ROWS = 8
NA_WIN_COLS = 16
NA_HEADS = 8
NA_HEAD_DIM = 64
NA_WIDTH = NA_HEADS * NA_HEAD_DIM
GLA_HEADS = 4
GLA_DK = 64
GLA_DV = 128
GLA_KW = GLA_HEADS * GLA_DK
GLA_VW = GLA_HEADS * GLA_DV
GLA_GATE_RANK = 16
GLA_GATE_TAU = 16.0
GLA_CHUNK = 64
MIX_WIDTH = NA_WIDTH + GLA_VW
IN_COLS = 3 * NA_WIDTH + 2 * GLA_KW + 2 * GLA_VW + 2 * GLA_GATE_RANK
D_FF = 2816
CONV_W = 3
RMS_EPS = 1e-6

kernel_name = 'hybrid_na_gla_convffn_encoder'


def rmsnorm(x, g):
    xf = x.astype(jnp.float32)
    y = xf * lax.rsqrt(jnp.mean(xf * xf, axis=-1, keepdims=True) + RMS_EPS)
    return (y * g.astype(jnp.float32)).astype(x.dtype)


def neighbourhood_attention(q, k, v, rpb):
    B, L, H, dh = q.shape
    T = L - N_META
    W = GRID_W
    R = T // W
    kh = min(NA_WIN_ROWS, R)
    kw = NA_WIN_COLS
    q = q * (dh ** -0.5)
    qm, km, vm = q[:, :N_META], k[:, :N_META], v[:, :N_META]

    def grid(t):
        return t[:, N_META:].reshape(B, R, W, H, dh).transpose(0, 3, 1, 2, 4)

    qg, kg, vg = grid(q), grid(k), grid(v)
    rs = jnp.clip(jnp.arange(R) - kh // 2, 0, R - kh)
    row_idx = rs[:, None] + jnp.arange(kh)[None, :]
    k_rows = kg[:, :, row_idx]
    v_rows = vg[:, :, row_idx]
    cq = jnp.arange(W)
    cs = jnp.clip(cq - kw // 2, 0, W - kw)
    in_win = (cq[None, :] >= cs[:, None]) & (cq[None, :] < cs[:, None] + kw)
    dr = row_idx - jnp.arange(R)[:, None] + (NA_WIN_ROWS - 1)
    dc = jnp.clip(cq[None, :] - cq[:, None], -(kw - 1), kw - 1) + (NA_WIN_COLS - 1)
    bias = rpb[:, dr[:, None, :, None], dc[None, :, None, :]]

    s_loc = jnp.einsum('bhrcd,bhrikd->bhrcik', qg, k_rows).astype(jnp.float32) + bias.astype(jnp.float32)
    s_loc = jnp.where(in_win[:, None, :], s_loc, -jnp.inf)
    s_meta = jnp.einsum('bhrcd,bmhd->bhrcm', qg, km).astype(jnp.float32)
    s_all = jnp.concatenate([s_loc.reshape(B, H, R, W, kh * W), s_meta], axis=-1)
    p = jax.nn.softmax(s_all, axis=-1).astype(v.dtype)
    p_loc = p[..., :kh * W].reshape(B, H, R, W, kh, W)
    p_meta = p[..., kh * W:]
    og = (jnp.einsum('bhrcik,bhrikd->bhrcd', p_loc, v_rows)
          + jnp.einsum('bhrcm,bmhd->bhrcd', p_meta, vm))
    og = og.transpose(0, 2, 3, 1, 4).reshape(B, T, H * dh)
    sm = jnp.einsum('bqhd,bkhd->bhqk', qm, km).astype(jnp.float32)
    pm = jax.nn.softmax(sm, axis=-1).astype(v.dtype)
    om = jnp.einsum('bhqk,bkhd->bqhd', pm, vm).reshape(B, N_META, H * dh)
    return jnp.concatenate([om, og], axis=1)


def gla_chunked(q, k, v, log_a, strict):
    C = q.shape[-2]
    b = jnp.cumsum(log_a, axis=-2)
    b_last = b[..., -1:, :]
    q_dec = q * jnp.exp(b)
    k_inv = k * jnp.exp(-b)
    k_end = k * jnp.exp(b_last - b)
    mask = jnp.tril(jnp.ones((C, C), dtype=bool), k=-1 if strict else 0)
    a = jnp.where(mask, jnp.einsum('bhncd,bhnsd->bhncs', q_dec, k_inv), 0.0)
    o = jnp.einsum('bhncs,bhnse->bhnce', a, v)
    ds = jnp.einsum('bhncd,bhnce->nbhde', k_end, v)
    decay = jnp.exp(b_last[..., 0, :]).transpose(2, 0, 1, 3)

    def step(s, inp):
        dec, d = inp
        return dec[..., None] * s + d, s

    s0 = jnp.zeros(ds.shape[1:], ds.dtype)
    _, s_prev = lax.scan(step, s0, (decay, ds))
    return o + jnp.einsum('bhncd,nbhde->bhnce', q_dec, s_prev)


def gla_bidirectional(q, k, v, g, z, up_f, bias_f, up_b, bias_b, out_gain):
    B, L, _ = q.shape
    H, dk, dv, C = GLA_HEADS, GLA_DK, GLA_DV, GLA_CHUNK
    f32 = jnp.float32
    pad = (-N_META) % C
    Lp = L + pad
    N = Lp // C
    la_f = jax.nn.log_sigmoid((z[..., :GLA_GATE_RANK] @ up_f + bias_f).astype(f32)) / GLA_GATE_TAU
    la_b = jax.nn.log_sigmoid((z[..., GLA_GATE_RANK:] @ up_b + bias_b).astype(f32)) / GLA_GATE_TAU

    def pad_seq(t, d):
        return jnp.pad(t.astype(f32).reshape(B, L, H, d), ((0, 0), (pad, 0), (0, 0), (0, 0)))

    def chunk(t):
        return t.reshape(B, N, C, H, t.shape[-1]).transpose(0, 3, 1, 2, 4)

    def unchunk(t):
        return t.transpose(0, 2, 3, 1, 4).reshape(B, Lp, H, t.shape[-1])

    qp = pad_seq(q * (dk ** -0.5), dk)
    kp = pad_seq(k, dk)
    vp = pad_seq(v, dv)
    lfp = pad_seq(la_f, dk)
    lbp = pad_seq(la_b, dk)
    o_f = unchunk(gla_chunked(chunk(qp), chunk(kp), chunk(vp), chunk(lfp), False))
    rev = lambda t: t[:, ::-1]
    o_b = rev(unchunk(gla_chunked(chunk(rev(qp)), chunk(rev(kp)), chunk(rev(vp)), chunk(rev(lbp)), True)))
    o = (o_f + o_b)[:, pad:].astype(q.dtype)
    o = rmsnorm(o, out_gain) * jax.nn.silu(g.reshape(B, L, H, dv))
    return o.reshape(B, L, H * dv)


def hybrid_layer(h, norm_mix_pre, w_in, rpb, na_gain, up_f, bias_f, up_b, bias_b, gla_gain,
                 w_o, norm_mix_post, norm_ffn_pre, w_ffn_in, conv_w, conv_b, w_ffn_out, norm_ffn_post):
    B, L, _ = h.shape
    u = rmsnorm(h, norm_mix_pre)
    proj = u @ w_in
    sizes = [NA_WIDTH, NA_WIDTH, NA_WIDTH, GLA_KW, GLA_KW, GLA_VW, GLA_VW, 2 * GLA_GATE_RANK]
    offs = np.cumsum([0] + sizes)
    parts = [proj[..., int(offs[i]):int(offs[i + 1])] for i in range(len(sizes))]
    q_na, k_na, v_na, q_gl, k_gl, v_gl, g_gl, z_gl = parts
    shp = (B, L, NA_HEADS, NA_HEAD_DIM)
    o_na = neighbourhood_attention(q_na.reshape(shp), k_na.reshape(shp), v_na.reshape(shp), rpb)
    o_na = rmsnorm(o_na, na_gain)
    o_gl = gla_bidirectional(q_gl, k_gl, v_gl, g_gl, z_gl, up_f, bias_f, up_b, bias_b, gla_gain)
    mix = jnp.concatenate([o_na, o_gl], axis=-1) @ w_o
    h = h + rmsnorm(mix, norm_mix_post)
    a = rmsnorm(h, norm_ffn_pre) @ w_ffn_in
    half = CONV_W // 2
    ap = jnp.pad(a, ((0, 0), (half, half), (0, 0)))
    a = sum(ap[:, j:j + L] * conv_w[j] for j in range(CONV_W)) + conv_b
    val, gate = a[..., :D_FF], a[..., D_FF:]
    y = (jax.nn.gelu(gate, approximate=True) * val) @ w_ffn_out
    return h + rmsnorm(y, norm_ffn_post)


def setup_inputs(seed: int = 0) -> dict:
    key = jax.random.key(seed)
    ks = jax.random.split(key, 20)
    f32 = jnp.float32

    def nrm(k, shape, scale):
        return jax.random.normal(k, shape, f32) * scale

    def gain(k, shape):
        return 1.0 + 0.1 * jax.random.normal(k, shape, f32)

    return {
        'x': nrm(ks[0], (BATCH, SEQ, D_MODEL), 1.0),
        'meta_tokens': nrm(ks[1], (N_META, D_MODEL), 1.0),
        'norm_mix_pre': gain(ks[2], (DEPTH, D_MODEL)),
        'w_in': nrm(ks[3], (DEPTH, D_MODEL, IN_COLS), D_MODEL ** -0.5),
        'na_rel_bias': nrm(ks[4], (DEPTH, NA_HEADS, 2 * NA_WIN_ROWS - 1, 2 * NA_WIN_COLS - 1), 0.5),
        'na_out_gain': gain(ks[5], (DEPTH, NA_WIDTH)),
        'gla_gate_up_fwd': nrm(ks[6], (DEPTH, GLA_GATE_RANK, GLA_KW), GLA_GATE_RANK ** -0.5),
        'gla_gate_bias_fwd': nrm(ks[7], (DEPTH, GLA_KW), 0.5),
        'gla_gate_up_bwd': nrm(ks[8], (DEPTH, GLA_GATE_RANK, GLA_KW), GLA_GATE_RANK ** -0.5),
        'gla_gate_bias_bwd': nrm(ks[9], (DEPTH, GLA_KW), 0.5),
        'gla_out_gain': gain(ks[10], (DEPTH, GLA_DV)),
        'w_o': nrm(ks[11], (DEPTH, MIX_WIDTH, D_MODEL), MIX_WIDTH ** -0.5),
        'norm_mix_post': gain(ks[12], (DEPTH, D_MODEL)),
        'norm_ffn_pre': gain(ks[13], (DEPTH, D_MODEL)),
        'w_ffn_in': nrm(ks[14], (DEPTH, D_MODEL, 2 * D_FF), D_MODEL ** -0.5),
        'ffn_conv_w': nrm(ks[15], (DEPTH, CONV_W, 2 * D_FF), CONV_W ** -0.5),
        'ffn_conv_b': nrm(ks[16], (DEPTH, 2 * D_FF), 0.02),
        'w_ffn_out': nrm(ks[17], (DEPTH, D_FF, D_MODEL), D_FF ** -0.5),
        'norm_ffn_post': gain(ks[18], (DEPTH, D_MODEL)),
    }


def reference(x, meta_tokens, norm_mix_pre, w_in, na_rel_bias, na_out_gain, gla_gate_up_fwd,
              gla_gate_bias_fwd, gla_gate_up_bwd, gla_gate_bias_bwd, gla_out_gain, w_o, norm_mix_post,
              norm_ffn_pre, w_ffn_in, ffn_conv_w, ffn_conv_b, w_ffn_out, norm_ffn_post):
    B = x.shape[0]
    meta = jnp.broadcast_to(meta_tokens.astype(x.dtype)[None], (B, N_META, x.shape[-1]))
    h = jnp.concatenate([meta, x], axis=1)
    for l in range(DEPTH):
        h = hybrid_layer(h, norm_mix_pre[l], w_in[l], na_rel_bias[l], na_out_gain[l],
                         gla_gate_up_fwd[l], gla_gate_bias_fwd[l], gla_gate_up_bwd[l], gla_gate_bias_bwd[l],
                         gla_out_gain[l], w_o[l], norm_mix_post[l], norm_ffn_pre[l], w_ffn_in[l],
                         ffn_conv_w[l], ffn_conv_b[l], w_ffn_out[l], norm_ffn_post[l])
    return h[:, N_META:]
```

```python
import functools

import jax
import jax.numpy as jnp
from jax import lax
from jax.experimental import pallas as pl
from jax.experimental.pallas import tpu as pltpu

F32 = jnp.float32
BF16 = jnp.bfloat16

N_META = 16
GRID_W = 64
NA_WIN_ROWS = 8
NA_WIN_COLS = 16
NA_HEADS = 8
NA_HEAD_DIM = 64
NA_WIDTH = NA_HEADS * NA_HEAD_DIM
GLA_HEADS = 4
GLA_DK = 64
GLA_DV = 128
GLA_KW = GLA_HEADS * GLA_DK
GLA_VW = GLA_HEADS * GLA_DV
GLA_GATE_RANK = 16
GLA_GATE_TAU = 16.0
GLA_CHUNK = 64
GLA_PAD = (-N_META) % GLA_CHUNK
NA_COLS = 3 * NA_WIDTH
GLA_COLS = 2 * GLA_KW + 2 * GLA_VW + 2 * GLA_GATE_RANK
D_FF = 2816
FF_BLK = 256
CONV_W = 3
RMS_EPS = 1e-6
MASK_NEG = -1e30

LANES = 128
SUBLANES = 8
VMEM_LIMIT = 56 * 1024 * 1024

_CONTRACT_LAST = (((1,), (1,)), ((), ()))
_CONTRACT_FIRST = (((0,), (0,)), ((), ()))


def _rms(x, g):
    return x * lax.rsqrt(jnp.mean(x * x, axis=-1, keepdims=True) + RMS_EPS) * g


def _const_spec(shape):
    nd = len(shape)
    return pl.BlockSpec(shape, lambda *_: (0,) * nd)


def _inproj_body(x_ref, g_ref, wna_ref, wgl_ref, na_ref, gl_ref):
    u = _rms(x_ref[...], g_ref[...]).astype(BF16)
    na_ref[...] = jnp.dot(u, wna_ref[...], preferred_element_type=F32).astype(BF16)
    gl_ref[...] = jnp.dot(u, wgl_ref[...], preferred_element_type=F32).astype(BF16)


def _inproj(x2, g, wna, wgl, tm):
    rows, d = x2.shape
    return pl.pallas_call(
        _inproj_body,
        grid=(rows // tm,),
        in_specs=[
            pl.BlockSpec((tm, d), lambda i: (i, 0)),
            _const_spec(g.shape),
            _const_spec(wna.shape),
            _const_spec(wgl.shape),
        ],
        out_specs=[
            pl.BlockSpec((tm, NA_COLS), lambda i: (i, 0)),
            pl.BlockSpec((tm, GLA_COLS), lambda i: (i, 0)),
        ],
        out_shape=[
            jax.ShapeDtypeStruct((rows, NA_COLS), BF16),
            jax.ShapeDtypeStruct((rows, GLA_COLS), BF16),
        ],
        compiler_params=pltpu.CompilerParams(
            dimension_semantics=("arbitrary",), vmem_limit_bytes=VMEM_LIMIT),
        name="inproj",
    )(x2, g, wna, wgl)


def _split_heads_rows(pair, lo):
    zero = jnp.zeros_like(pair)
    return jnp.concatenate([jnp.where(lo, pair, zero), jnp.where(lo, zero, pair)], axis=0)


def _na_body(q_ref, k_ref, v_ref, km_ref, vm_ref, t2_ref, gain_ref, o_ref, *, rq, n_rows):
    j = pl.program_id(1)
    w = GRID_W
    kh = NA_WIN_ROWS
    lo = lax.broadcasted_iota(jnp.int32, (w, LANES), 1) < NA_HEAD_DIM
    n_pairs = NA_HEADS // 2
    for i in range(rq):
        r = j * rq + i
        rs = jnp.clip(r - kh // 2, 0, n_rows - kh)
        e0 = rs - r + (NA_WIN_ROWS - 1)
        k0 = pl.multiple_of(rs * w, w)
        outs = []
        ssq = jnp.zeros((w, 1), F32)
        for p in range(n_pairs):
            cols = slice(p * LANES, (p + 1) * LANES)
            qp = q_ref[0, i * w:(i + 1) * w, cols] * (NA_HEAD_DIM ** -0.5)
            q2 = _split_heads_rows(qp, lo)
            kw = k_ref[0, pl.ds(k0, kh * w), cols]
            vw = v_ref[0, pl.ds(k0, kh * w), cols]
            s = lax.dot_general(q2, kw, _CONTRACT_LAST, preferred_element_type=F32)
            bias = jnp.concatenate(
                [jnp.concatenate([t2_ref[2 * p + hh, e0 + 2 * jj] for jj in range(kh // 2)], axis=1)
                 for hh in range(2)], axis=0)
            s = s + bias
            sm = lax.dot_general(q2, km_ref[:, cols], _CONTRACT_LAST, preferred_element_type=F32)
            m = jnp.maximum(jnp.max(s, axis=-1, keepdims=True), jnp.max(sm, axis=-1, keepdims=True))
            pw = jnp.exp(s - m)
            pm = jnp.exp(sm - m)
            l = jnp.sum(pw, axis=-1, keepdims=True) + jnp.sum(pm, axis=-1, keepdims=True)
            o2 = (jnp.dot(pw.astype(BF16), vw, preferred_element_type=F32)
                  + jnp.dot(pm.astype(BF16), vm_ref[:, cols], preferred_element_type=F32))
            o2 = o2 / l
            op = jnp.where(lo, o2[:w], o2[w:])
            outs.append(op)
            ssq = ssq + jnp.sum(op * op, axis=-1, keepdims=True)
        inv = lax.rsqrt(ssq * (1.0 / NA_WIDTH) + RMS_EPS)
        for p in range(n_pairs):
            cols = slice(p * LANES, (p + 1) * LANES)
            o_ref[0, i * w:(i + 1) * w, cols] = (outs[p] * inv * gain_ref[:, cols]).astype(BF16)


def _na(na_x, na_m, t2, gain, rq):
    b, t, _ = na_x.shape
    n_rows = t // GRID_W
    nw = NA_WIDTH
    return pl.pallas_call(
        functools.partial(_na_body, rq=rq, n_rows=n_rows),
        grid=(b, n_rows // rq),
        in_specs=[
            pl.BlockSpec((1, rq * GRID_W, nw), lambda bi, j: (bi, j, 0)),
            pl.BlockSpec((1, t, nw), lambda bi, j: (bi, 0, 1)),
            pl.BlockSpec((1, t, nw), lambda bi, j: (bi, 0, 2)),
            pl.BlockSpec((N_META, nw), lambda bi, j: (0, 1)),
            pl.BlockSpec((N_META, nw), lambda bi, j: (0, 2)),
            _const_spec(t2.shape),
            _const_spec(gain.shape),
        ],
        out_specs=pl.BlockSpec((1, rq * GRID_W, nw), lambda bi, j: (bi, j, 0)),
        out_shape=jax.ShapeDtypeStruct((b, t, nw), BF16),
        compiler_params=pltpu.CompilerParams(
            dimension_semantics=("arbitrary", "arbitrary"), vmem_limit_bytes=VMEM_LIMIT),
        name="na",
    )(na_x, na_x, na_x, na_m, na_m, t2, gain)


def _na_meta_body(q_ref, k_ref, v_ref, gain_ref, o_ref):
    lane = lax.broadcasted_iota(jnp.int32, (N_META, NA_WIDTH), 1)
    q = q_ref[...] * (NA_HEAD_DIM ** -0.5)
    k = k_ref[...]
    v = v_ref[...]
    om = jnp.zeros((N_META, NA_WIDTH), F32)
    for h in range(NA_HEADS):
        in_head = (lane >= h * NA_HEAD_DIM) & (lane < (h + 1) * NA_HEAD_DIM)
        qh = jnp.where(in_head, q, jnp.zeros_like(q))
        s = lax.dot_general(qh, k, _CONTRACT_LAST, preferred_element_type=F32)
        m = jnp.max(s, axis=-1, keepdims=True)
        pw = jnp.exp(s - m)
        pw = pw / jnp.sum(pw, axis=-1, keepdims=True)
        oh = jnp.dot(pw.astype(BF16), v, preferred_element_type=F32)
        om = jnp.where(in_head, oh, om)
    o_ref[...] = _rms(om, gain_ref[...]).astype(BF16)


def _na_meta(na_m, gain):
    nw = NA_WIDTH
    return pl.pallas_call(
        _na_meta_body,
        grid=(1,),
        in_specs=[
            pl.BlockSpec((N_META, nw), lambda i: (0, 0)),
            pl.BlockSpec((N_META, nw), lambda i: (0, 1)),
            pl.BlockSpec((N_META, nw), lambda i: (0, 2)),
            _const_spec(gain.shape),
        ],
        out_specs=pl.BlockSpec((N_META, nw), lambda i: (0, 0)),
        out_shape=jax.ShapeDtypeStruct((N_META, nw), BF16),
        name="na_meta",
    )(na_m, na_m, na_m, gain)


_GQ, _GK, _GV, _GG, _GZ = 0, GLA_KW, 2 * GLA_KW, 2 * GLA_KW + GLA_VW, 2 * GLA_KW + 2 * GLA_VW


def _gla_chunk(buf, st, row0, up, bias, backward, first_chunk):
    c = GLA_CHUNK
    rows = pl.ds(row0, c)
    q = buf[rows, _GQ:_GQ + GLA_KW].astype(F32) * (GLA_DK ** -0.5)
    k = buf[rows, _GK:_GK + GLA_KW].astype(F32)
    z = buf[rows, _GZ:_GZ + 2 * GLA_GATE_RANK]
    gate = jnp.dot(z, up, preferred_element_type=F32) + bias
    la = (jnp.minimum(gate, 0.0) - jnp.log1p(jnp.exp(-jnp.abs(gate)))) * (1.0 / GLA_GATE_TAU)
    if first_chunk:
        la = jnp.where(lax.broadcasted_iota(jnp.int32, (c, 1), 0) >= GLA_PAD, la, 0.0)
    ti = lax.broadcasted_iota(jnp.int32, (c, c), 0)
    si = lax.broadcasted_iota(jnp.int32, (c, c), 1)
    tri = jnp.where((si >= ti) if backward else (si <= ti), 1.0, 0.0).astype(BF16)
    hi = la.astype(BF16)
    rem = la - hi.astype(F32)
    mid = rem.astype(BF16)
    low = (rem - mid.astype(F32)).astype(BF16)
    cs = jnp.dot(tri, jnp.concatenate([hi, mid, low], axis=1), preferred_element_type=F32)
    b = cs[:, :GLA_KW] + cs[:, GLA_KW:2 * GLA_KW] + cs[:, 2 * GLA_KW:]
    b_last = b[0:1] if backward else b[c - 1:c]
    q_dec = (q * jnp.exp(b)).astype(BF16)
    k_inv = (k * jnp.exp(-b)).astype(BF16)
    k_end = (k * jnp.exp(b_last - b)).astype(BF16)
    dec = jnp.exp(b_last)

    lo_c = lax.broadcasted_iota(jnp.int32, (c, LANES), 1) < GLA_DK
    lo_s = lax.broadcasted_iota(jnp.int32, (GLA_DV, LANES), 1) < GLA_DK
    t2 = lax.broadcasted_iota(jnp.int32, (2 * c, c), 0) % c
    s2 = lax.broadcasted_iota(jnp.int32, (2 * c, c), 1)
    keep = (s2 > t2) if backward else (s2 <= t2)
    outs = []
    for p in range(GLA_HEADS // 2):
        cols = slice(p * LANES, (p + 1) * LANES)
        q2 = _split_heads_rows(q_dec[:, cols], lo_c)
        a2 = lax.dot_general(q2, k_inv[:, cols], _CONTRACT_LAST, preferred_element_type=F32)
        a2 = jnp.where(keep, a2, 0.0).astype(BF16)
        s_old = st[p]
        inter = lax.dot_general(q2, s_old.astype(BF16), _CONTRACT_LAST, preferred_element_type=F32)
        ds = []
        for hh in range(2):
            h = 2 * p + hh
            vh = buf[rows, _GV + h * GLA_DV:_GV + (h + 1) * GLA_DV]
            intra = jnp.dot(a2[hh * c:(hh + 1) * c], vh, preferred_element_type=F32)
            outs.append(intra + inter[hh * c:(hh + 1) * c])
            ds.append(lax.dot_general(vh, k_end[:, cols], _CONTRACT_FIRST, preferred_element_type=F32))
        st[p] = dec[:, cols] * s_old + jnp.where(lo_s, ds[0], ds[1])
    return outs


def _gla_body(x_ref, m_ref, upf_ref, bf_ref, upb_ref, bb_ref, gain_ref, ox_ref, om_ref, buf, ob, st):
    c = GLA_CHUNK
    t = x_ref.shape[1]
    n_chunks = t // c + 1
    buf[0:GLA_PAD, :] = jnp.zeros((GLA_PAD, GLA_COLS), BF16)
    buf[GLA_PAD:c, :] = m_ref[...]
    buf[c:, :] = x_ref[0]

    st[...] = jnp.zeros(st.shape, F32)
    up_b = upb_ref[...]
    bias_b = bb_ref[...]

    def bwd_step(it, carry):
        n = n_chunks - 1 - it
        row0 = pl.multiple_of(n * c, c)
        outs = _gla_chunk(buf, st, row0, up_b, bias_b, True, False)
        for h in range(GLA_HEADS):
            ob[pl.ds(row0, c), h * GLA_DV:(h + 1) * GLA_DV] = outs[h]
        return carry

    lax.fori_loop(0, n_chunks - 1, bwd_step, 0)
    outs = _gla_chunk(buf, st, 0, up_b, bias_b, True, True)
    for h in range(GLA_HEADS):
        ob[0:c, h * GLA_DV:(h + 1) * GLA_DV] = outs[h]

    st[...] = jnp.zeros(st.shape, F32)
    up_f = upf_ref[...]
    bias_f = bf_ref[...]
    gain = gain_ref[...]

    def finish(outs, row0):
        res = []
        for h in range(GLA_HEADS):
            o = outs[h] + ob[pl.ds(row0, c), h * GLA_DV:(h + 1) * GLA_DV]
            g = buf[pl.ds(row0, c), _GG + h * GLA_DV:_GG + (h + 1) * GLA_DV].astype(F32)
            res.append((_rms(o, gain) * (g * jax.nn.sigmoid(g))).astype(BF16))
        return res

    res = finish(_gla_chunk(buf, st, 0, up_f, bias_f, False, True), 0)
    for h in range(GLA_HEADS):
        om_ref[0, :, h * GLA_DV:(h + 1) * GLA_DV] = res[h][GLA_PAD:]

    def fwd_step(n, carry):
        row0 = pl.multiple_of(n * c, c)
        res = finish(_gla_chunk(buf, st, row0, up_f, bias_f, False, False), row0)
        out0 = pl.multiple_of((n - 1) * c, c)
        for h in range(GLA_HEADS):
            ox_ref[0, pl.ds(out0, c), h * GLA_DV:(h + 1) * GLA_DV] = res[h]
        return carry

    lax.fori_loop(1, n_chunks, fwd_step, 0)


def _gla(gl_x, gl_m, up_f, bias_f, up_b, bias_b, gain):
    b, t, _ = gl_x.shape
    lp = t + GLA_CHUNK
    return pl.pallas_call(
        _gla_body,
        grid=(b,),
        in_specs=[
            pl.BlockSpec((1, t, GLA_COLS), lambda bi: (bi, 0, 0)),
            _const_spec(gl_m.shape),
            _const_spec(up_f.shape),
            _const_spec(bias_f.shape),
            _const_spec(up_b.shape),
            _const_spec(bias_b.shape),
            _const_spec(gain.shape),
        ],
        out_specs=[
            pl.BlockSpec((1, t, GLA_VW), lambda bi: (bi, 0, 0)),
            pl.BlockSpec((1, N_META, GLA_VW), lambda bi: (bi, 0, 0)),
        ],
        out_shape=[
            jax.ShapeDtypeStruct((b, t, GLA_VW), BF16),
            jax.ShapeDtypeStruct((b, N_META, GLA_VW), BF16),
        ],
        scratch_shapes=[
            pltpu.VMEM((lp, GLA_COLS), BF16),
            pltpu.VMEM((lp, GLA_VW), F32),
            pltpu.VMEM((GLA_HEADS // 2, GLA_DV, LANES), F32),
        ],
        compiler_params=pltpu.CompilerParams(
            dimension_semantics=("arbitrary",), vmem_limit_bytes=VMEM_LIMIT),
        name="gla",
    )(gl_x, gl_m, up_f, bias_f, up_b, bias_b, gain)


def _mix_body(x_ref, na_ref, gl_ref, wo_ref, g_ref, h_ref):
    mix = (jnp.dot(na_ref[...], wo_ref[:NA_WIDTH], preferred_element_type=F32)
           + jnp.dot(gl_ref[...], wo_ref[NA_WIDTH:], preferred_element_type=F32))
    h_ref[...] = x_ref[...] + _rms(mix, g_ref[...])


def _mix(x2, o_na, o_gl, wo, g, tm):
    rows, d = x2.shape
    return pl.pallas_call(
        _mix_body,
        grid=(rows // tm,),
        in_specs=[
            pl.BlockSpec((tm, d), lambda i: (i, 0)),
            pl.BlockSpec((tm, NA_WIDTH), lambda i: (i, 0)),
            pl.BlockSpec((tm, GLA_VW), lambda i: (i, 0)),
            _const_spec(wo.shape),
            _const_spec(g.shape),
        ],
        out_specs=pl.BlockSpec((tm, d), lambda i: (i, 0)),
        out_shape=jax.ShapeDtypeStruct((rows, d), F32),
        compiler_params=pltpu.CompilerParams(
            dimension_semantics=("arbitrary",), vmem_limit_bytes=VMEM_LIMIT),
        name="mix",
    )(x2, o_na, o_gl, wo, g)


def _ffn_body(hm_ref, hp_ref, hn_ref, hmeta_ref, g1_ref, win_ref, cw_ref, wout_ref, g2_ref,
              o_ref, acc_ref, *, tm, n_tiles):
    t = pl.program_id(1)
    halo = SUBLANES
    ext = tm + 2 * halo
    prev = jnp.where(t == 0, hmeta_ref[0], hp_ref[0])
    hext = jnp.concatenate([prev, hm_ref[0], hn_ref[0]], axis=0)
    rowi = lax.broadcasted_iota(jnp.int32, (ext, 1), 0)
    keep = jnp.logical_or(rowi < tm + halo, t < n_tiles - 1)
    n2 = jnp.where(keep, _rms(hext, g1_ref[...]), 0.0).astype(BF16)
    acc_ref[...] = jnp.zeros(acc_ref.shape, F32)

    def conv(a, taps):
        a_prev = pltpu.roll(a, 1, 0)[halo:halo + tm]
        a_next = pltpu.roll(a, ext - 1, 0)[halo:halo + tm]
        return a_prev * taps[0:1] + a[halo:halo + tm] * taps[1:2] + a_next * taps[2:3] + taps[3:4]

    def step(cb, carry):
        cw = cw_ref[cb]
        val = conv(jnp.dot(n2, win_ref[0, cb], preferred_element_type=F32), cw[0:4])
        gate = conv(jnp.dot(n2, win_ref[1, cb], preferred_element_type=F32), cw[4:8])
        y = (jax.nn.gelu(gate, approximate=True) * val).astype(BF16)
        acc_ref[...] += jnp.dot(y, wout_ref[cb], preferred_element_type=F32)
        return carry

    lax.fori_loop(0, D_FF // FF_BLK, step, 0)
    o_ref[0] = hm_ref[0] + _rms(acc_ref[...], g2_ref[...])


def _ffn(h1, h1m, g1, win, cw, wout, g2, tm):
    b, t, d = h1.shape
    n_tiles = t // tm
    hb = tm // SUBLANES
    last = t // SUBLANES - 1
    single = pl.Buffered(1)
    return pl.pallas_call(
        functools.partial(_ffn_body, tm=tm, n_tiles=n_tiles),
        grid=(b, n_tiles),
        in_specs=[
            pl.BlockSpec((1, tm, d), lambda bi, ti: (bi, ti, 0)),
            pl.BlockSpec((1, SUBLANES, d), lambda bi, ti: (bi, jnp.maximum(ti * hb - 1, 0), 0)),
            pl.BlockSpec((1, SUBLANES, d), lambda bi, ti: (bi, jnp.minimum((ti + 1) * hb, last), 0)),
            pl.BlockSpec((1, SUBLANES, d), lambda bi, ti: (bi, N_META // SUBLANES - 1, 0)),
            _const_spec(g1.shape),
            pl.BlockSpec(win.shape, lambda bi, ti: (0, 0, 0, 0), pipeline_mode=single),
            _const_spec(cw.shape),
            pl.BlockSpec(wout.shape, lambda bi, ti: (0, 0, 0), pipeline_mode=single),
            _const_spec(g2.shape),
        ],
        out_specs=pl.BlockSpec((1, tm, d), lambda bi, ti: (bi, ti, 0)),
        out_shape=jax.ShapeDtypeStruct((b, t, d), F32),
        scratch_shapes=[pltpu.VMEM((tm, d), F32)],
        compiler_params=pltpu.CompilerParams(
            dimension_semantics=("arbitrary", "arbitrary"), vmem_limit_bytes=VMEM_LIMIT),
        name="ffn",
    )(h1, h1, h1, h1m, g1, win, cw, wout, g2)


def _na_bias_table(rpb):
    w, kw = GRID_W, NA_WIN_COLS
    cq = jnp.arange(w)
    cs = jnp.clip(cq - kw // 2, 0, w - kw)
    in_win = (cq[None, :] >= cs[:, None]) & (cq[None, :] < cs[:, None] + kw)
    dc = jnp.clip(cq[None, :] - cq[:, None], -(kw - 1), kw - 1) + (kw - 1)
    tb = jnp.where(in_win[None, None], rpb[:, :, dc].astype(F32), MASK_NEG)
    return jnp.concatenate([tb[:, :-1], tb[:, 1:]], axis=-1)


def kernel(x, meta_tokens, norm_mix_pre, w_in, na_rel_bias, na_out_gain, gla_gate_up_fwd,
           gla_gate_bias_fwd, gla_gate_up_bwd, gla_gate_bias_bwd, gla_out_gain, w_o, norm_mix_post,
           norm_ffn_pre, w_ffn_in, ffn_conv_w, ffn_conv_b, w_ffn_out, norm_ffn_post):
    b, t, d = x.shape
    depth = w_in.shape[0]
    assert depth == 1, "meta rows are only carried as far as a single layer needs them"
    assert t % GRID_W == 0 and t // GRID_W >= NA_WIN_ROWS and N_META == 2 * SUBLANES
    l = 0
    row = lambda a: a[l].reshape(1, -1).astype(F32)

    wna = w_in[l, :, :NA_COLS].astype(BF16)
    wgl = w_in[l, :, NA_COLS:].astype(BF16)
    wo = w_o[l].astype(BF16)
    nb = D_FF // FF_BLK
    win = w_ffn_in[l].astype(BF16).reshape(d, 2, nb, FF_BLK).transpose(1, 2, 0, 3)
    wout = w_ffn_out[l].astype(BF16).reshape(nb, FF_BLK, d)
    taps = jnp.concatenate([ffn_conv_w[l], ffn_conv_b[l][None]], axis=0).astype(F32)
    cw = taps.reshape(CONV_W + 1, 2, nb, FF_BLK).transpose(2, 1, 0, 3).reshape(nb, 2 * (CONV_W + 1), FF_BLK)
    t2 = _na_bias_table(na_rel_bias[l])
    zpad = jnp.zeros((GLA_GATE_RANK, GLA_KW), BF16)
    up_f = jnp.concatenate([gla_gate_up_fwd[l].astype(BF16), zpad], axis=0)
    up_b = jnp.concatenate([zpad, gla_gate_up_bwd[l].astype(BF16)], axis=0)

    x2 = x.reshape(b * t, d)
    g_pre = row(norm_mix_pre)
    na_x, gl_x = _inproj(x2, g_pre, wna, wgl, 512)
    na_m, gl_m = _inproj(meta_tokens.astype(F32), g_pre, wna, wgl, N_META)
    na_x = na_x.reshape(b, t, NA_COLS)
    gl_x = gl_x.reshape(b, t, GLA_COLS)

    na_gain = row(na_out_gain)
    o_na = _na(na_x, na_m, t2, na_gain, 2)
    o_na_m = _na_meta(na_m, na_gain)
    o_gl, o_gl_m = _gla(gl_x, gl_m, up_f, row(gla_gate_bias_fwd), up_b, row(gla_gate_bias_bwd),
                        row(gla_out_gain))

    g_post = row(norm_mix_post)
    h1 = _mix(x2, o_na.reshape(b * t, NA_WIDTH), o_gl.reshape(b * t, GLA_VW), wo, g_post, 512)
    h1m = _mix(jnp.broadcast_to(meta_tokens.astype(F32)[None], (b, N_META, d)).reshape(b * N_META, d),
               jnp.broadcast_to(o_na_m[None], (b, N_META, NA_WIDTH)).reshape(b * N_META, NA_WIDTH),
               o_gl_m.reshape(b * N_META, GLA_VW), wo, g_post, b * N_META)
    out = _ffn(h1.reshape(b, t, d), h1m.reshape(b, N_META, d), row(norm_ffn_pre), win, cw, wout,
               row(norm_ffn_post), 512)
    return out
```

```python
import functools

import jax
import jax.numpy as jnp
from jax import lax
from jax.experimental import pallas as pl
from jax.experimental.pallas import tpu as pltpu

F32 = jnp.float32
BF16 = jnp.bfloat16

N_META = 16
GRID_W = 64
NA_WIN_ROWS = 8
NA_WIN_COLS = 16
NA_HEADS = 8
NA_HEAD_DIM = 64
NA_WIDTH = NA_HEADS * NA_HEAD_DIM
GLA_HEADS = 4
GLA_DK = 64
GLA_DV = 128
GLA_KW = GLA_HEADS * GLA_DK
GLA_VW = GLA_HEADS * GLA_DV
GLA_GATE_RANK = 16
GLA_GATE_TAU = 16.0
GLA_CHUNK = 64
GLA_PAD = (-N_META) % GLA_CHUNK
NA_COLS = 3 * NA_WIDTH
GLA_COLS = 2 * GLA_KW + 2 * GLA_VW + 2 * GLA_GATE_RANK
D_FF = 2816
FF_BLK = 256
CONV_W = 3
RMS_EPS = 1e-6
MASK_NEG = -1e30

LANES = 128
SUBLANES = 8
VMEM_LIMIT = 56 * 1024 * 1024

_CONTRACT_LAST = (((1,), (1,)), ((), ()))
_CONTRACT_FIRST = (((0,), (0,)), ((), ()))


def _rms(x, g):
    return x * lax.rsqrt(jnp.mean(x * x, axis=-1, keepdims=True) + RMS_EPS) * g


def _const_spec(shape):
    nd = len(shape)
    return pl.BlockSpec(shape, lambda *_: (0,) * nd)


def _inproj_body(x_ref, g_ref, wna_ref, wgl_ref, na_ref, gl_ref):
    u = _rms(x_ref[...], g_ref[...]).astype(BF16)
    na_ref[...] = jnp.dot(u, wna_ref[...], preferred_element_type=F32).astype(BF16)
    gl_ref[...] = jnp.dot(u, wgl_ref[...], preferred_element_type=F32).astype(BF16)


def _inproj(x2, g, wna, wgl, tm):
    rows, d = x2.shape
    return pl.pallas_call(
        _inproj_body,
        grid=(rows // tm,),
        in_specs=[
            pl.BlockSpec((tm, d), lambda i: (i, 0)),
            _const_spec(g.shape),
            _const_spec(wna.shape),
            _const_spec(wgl.shape),
        ],
        out_specs=[
            pl.BlockSpec((tm, NA_COLS), lambda i: (i, 0)),
            pl.BlockSpec((tm, GLA_COLS), lambda i: (i, 0)),
        ],
        out_shape=[
            jax.ShapeDtypeStruct((rows, NA_COLS), BF16),
            jax.ShapeDtypeStruct((rows, GLA_COLS), BF16),
        ],
        compiler_params=pltpu.CompilerParams(
            dimension_semantics=("arbitrary",), vmem_limit_bytes=VMEM_LIMIT),
        name="inproj",
    )(x2, g, wna, wgl)


def _split_heads_rows(pair, lo):
    zero = jnp.zeros_like(pair)
    return jnp.concatenate([jnp.where(lo, pair, zero), jnp.where(lo, zero, pair)], axis=0)


def _na_body(q_ref, k_ref, v_ref, km_ref, vm_ref, t2_ref, gain_ref, o_ref, *, rq, n_rows):
    j = pl.program_id(1)
    w = GRID_W
    kh = NA_WIN_ROWS
    lo = lax.broadcasted_iota(jnp.int32, (w, LANES), 1) < NA_HEAD_DIM
    n_pairs = NA_HEADS // 2
    units = [(i, p) for i in range(rq) for p in range(n_pairs)]

    def stage1(i, p):
        r = j * rq + i
        rs = jnp.clip(r - kh // 2, 0, n_rows - kh)
        e0 = rs - r + (NA_WIN_ROWS - 1)
        k0 = pl.multiple_of(rs * w, w)
        cols = slice(p * LANES, (p + 1) * LANES)
        qp = q_ref[0, i * w:(i + 1) * w, cols] * (NA_HEAD_DIM ** -0.5)
        q2 = _split_heads_rows(qp, lo)
        kw = k_ref[0, pl.ds(k0, kh * w), cols]
        s = lax.dot_general(q2, kw, _CONTRACT_LAST, preferred_element_type=F32)
        bias = jnp.concatenate(
            [jnp.concatenate([t2_ref[2 * p + hh, e0 + 2 * jj] for jj in range(kh // 2)], axis=1)
             for hh in range(2)], axis=0)
        s = s + bias
        sm = lax.dot_general(q2, km_ref[:, cols], _CONTRACT_LAST, preferred_element_type=F32)
        m = jnp.maximum(jnp.max(s, axis=-1, keepdims=True), jnp.max(sm, axis=-1, keepdims=True))
        return s, sm, m, k0

    def stage2(s, sm, m):
        pw = jnp.exp(s - m)
        pm = jnp.exp(sm - m)
        l = jnp.sum(pw, axis=-1, keepdims=True) + jnp.sum(pm, axis=-1, keepdims=True)
        return pw.astype(BF16), pm.astype(BF16), l

    def stage3(p, k0, pw, pm, l):
        cols = slice(p * LANES, (p + 1) * LANES)
        vw = v_ref[0, pl.ds(k0, kh * w), cols]
        o2 = (jnp.dot(pw, vw, preferred_element_type=F32)
              + jnp.dot(pm, vm_ref[:, cols], preferred_element_type=F32))
        o2 = o2 / l
        return jnp.where(lo, o2[:w], o2[w:])

    n = len(units)
    r1, r2, outs = {}, {}, {}
    for step in range(n + 2):
        if step < n:
            r1[step] = stage1(*units[step])
        if 0 <= step - 1 < n:
            s, sm, m, k0 = r1.pop(step - 1)
            r2[step - 1] = stage2(s, sm, m) + (k0,)
        if 0 <= step - 2 < n:
            pw, pm, l, k0 = r2.pop(step - 2)
            outs[units[step - 2]] = stage3(units[step - 2][1], k0, pw, pm, l)
    for i in range(rq):
        ssq = jnp.zeros((w, 1), F32)
        for p in range(n_pairs):
            ssq = ssq + jnp.sum(outs[(i, p)] * outs[(i, p)], axis=-1, keepdims=True)
        inv = lax.rsqrt(ssq * (1.0 / NA_WIDTH) + RMS_EPS)
        for p in range(n_pairs):
            cols = slice(p * LANES, (p + 1) * LANES)
            o_ref[0, i * w:(i + 1) * w, cols] = (outs[(i, p)] * inv * gain_ref[:, cols]).astype(BF16)


def _na(na_x, na_m, t2, gain, rq):
    b, t, _ = na_x.shape
    n_rows = t // GRID_W
    nw = NA_WIDTH
    return pl.pallas_call(
        functools.partial(_na_body, rq=rq, n_rows=n_rows),
        grid=(b, n_rows // rq),
        in_specs=[
            pl.BlockSpec((1, rq * GRID_W, nw), lambda bi, j: (bi, j, 0)),
            pl.BlockSpec((1, t, nw), lambda bi, j: (bi, 0, 1)),
            pl.BlockSpec((1, t, nw), lambda bi, j: (bi, 0, 2)),
            pl.BlockSpec((N_META, nw), lambda bi, j: (0, 1)),
            pl.BlockSpec((N_META, nw), lambda bi, j: (0, 2)),
            _const_spec(t2.shape),
            _const_spec(gain.shape),
        ],
        out_specs=pl.BlockSpec((1, rq * GRID_W, nw), lambda bi, j: (bi, j, 0)),
        out_shape=jax.ShapeDtypeStruct((b, t, nw), BF16),
        compiler_params=pltpu.CompilerParams(
            dimension_semantics=("arbitrary", "arbitrary"), vmem_limit_bytes=VMEM_LIMIT),
        name="na",
    )(na_x, na_x, na_x, na_m, na_m, t2, gain)


def _na_meta_body(q_ref, k_ref, v_ref, gain_ref, o_ref):
    lane = lax.broadcasted_iota(jnp.int32, (N_META, NA_WIDTH), 1)
    q = q_ref[...] * (NA_HEAD_DIM ** -0.5)
    k = k_ref[...]
    v = v_ref[...]
    om = jnp.zeros((N_META, NA_WIDTH), F32)
    for h in range(NA_HEADS):
        in_head = (lane >= h * NA_HEAD_DIM) & (lane < (h + 1) * NA_HEAD_DIM)
        qh = jnp.where(in_head, q, jnp.zeros_like(q))
        s = lax.dot_general(qh, k, _CONTRACT_LAST, preferred_element_type=F32)
        m = jnp.max(s, axis=-1, keepdims=True)
        pw = jnp.exp(s - m)
        pw = pw / jnp.sum(pw, axis=-1, keepdims=True)
        oh = jnp.dot(pw.astype(BF16), v, preferred_element_type=F32)
        om = jnp.where(in_head, oh, om)
    o_ref[...] = _rms(om, gain_ref[...]).astype(BF16)


def _na_meta(na_m, gain):
    nw = NA_WIDTH
    return pl.pallas_call(
        _na_meta_body,
        grid=(1,),
        in_specs=[
            pl.BlockSpec((N_META, nw), lambda i: (0, 0)),
            pl.BlockSpec((N_META, nw), lambda i: (0, 1)),
            pl.BlockSpec((N_META, nw), lambda i: (0, 2)),
            _const_spec(gain.shape),
        ],
        out_specs=pl.BlockSpec((N_META, nw), lambda i: (0, 0)),
        out_shape=jax.ShapeDtypeStruct((N_META, nw), BF16),
        name="na_meta",
    )(na_m, na_m, na_m, gain)


_GQ, _GK, _GV, _GG, _GZ = 0, GLA_KW, 2 * GLA_KW, 2 * GLA_KW + GLA_VW, 2 * GLA_KW + 2 * GLA_VW


def _gla_chunk(buf, st, row0, up, bias, backward, first_chunk):
    c = GLA_CHUNK
    rows = pl.ds(row0, c)
    q = buf[rows, _GQ:_GQ + GLA_KW].astype(F32) * (GLA_DK ** -0.5)
    k = buf[rows, _GK:_GK + GLA_KW].astype(F32)
    z = buf[rows, _GZ:_GZ + 2 * GLA_GATE_RANK]
    gate = jnp.dot(z, up, preferred_element_type=F32) + bias
    la = (jnp.minimum(gate, 0.0) - jnp.log1p(jnp.exp(-jnp.abs(gate)))) * (1.0 / GLA_GATE_TAU)
    if first_chunk:
        la = jnp.where(lax.broadcasted_iota(jnp.int32, (c, 1), 0) >= GLA_PAD, la, 0.0)
    ti = lax.broadcasted_iota(jnp.int32, (c, c), 0)
    si = lax.broadcasted_iota(jnp.int32, (c, c), 1)
    tri = jnp.where((si >= ti) if backward else (si <= ti), 1.0, 0.0).astype(BF16)
    hi = la.astype(BF16)
    rem = la - hi.astype(F32)
    mid = rem.astype(BF16)
    low = (rem - mid.astype(F32)).astype(BF16)
    cs = jnp.dot(tri, jnp.concatenate([hi, mid, low], axis=1), preferred_element_type=F32)
    b = cs[:, :GLA_KW] + cs[:, GLA_KW:2 * GLA_KW] + cs[:, 2 * GLA_KW:]
    b_last = b[0:1] if backward else b[c - 1:c]
    q_dec = (q * jnp.exp(b)).astype(BF16)
    k_inv = (k * jnp.exp(-b)).astype(BF16)
    k_end = (k * jnp.exp(b_last - b)).astype(BF16)
    dec = jnp.exp(b_last)

    lo_c = lax.broadcasted_iota(jnp.int32, (c, LANES), 1) < GLA_DK
    lo_s = lax.broadcasted_iota(jnp.int32, (GLA_DV, LANES), 1) < GLA_DK
    t2 = lax.broadcasted_iota(jnp.int32, (2 * c, c), 0) % c
    s2 = lax.broadcasted_iota(jnp.int32, (2 * c, c), 1)
    keep = (s2 > t2) if backward else (s2 <= t2)
    outs = []
    for p in range(GLA_HEADS // 2):
        cols = slice(p * LANES, (p + 1) * LANES)
        q2 = _split_heads_rows(q_dec[:, cols], lo_c)
        a2 = lax.dot_general(q2, k_inv[:, cols], _CONTRACT_LAST, preferred_element_type=F32)
        a2 = jnp.where(keep, a2, 0.0).astype(BF16)
        s_old = st[p]
        inter = lax.dot_general(q2, s_old.astype(BF16), _CONTRACT_LAST, preferred_element_type=F32)
        ds = []
        for hh in range(2):
            h = 2 * p + hh
            vh = buf[rows, _GV + h * GLA_DV:_GV + (h + 1) * GLA_DV]
            intra = jnp.dot(a2[hh * c:(hh + 1) * c], vh, preferred_element_type=F32)
            outs.append(intra + inter[hh * c:(hh + 1) * c])
            ds.append(lax.dot_general(vh, k_end[:, cols], _CONTRACT_FIRST, preferred_element_type=F32))
        st[p] = dec[:, cols] * s_old + jnp.where(lo_s, ds[0], ds[1])
    return outs


def _gla_body(x_ref, m_ref, upf_ref, bf_ref, upb_ref, bb_ref, gain_ref, ox_ref, om_ref, buf, ob, st):
    c = GLA_CHUNK
    t = x_ref.shape[1]
    n_chunks = t // c + 1
    buf[0:GLA_PAD, :] = jnp.zeros((GLA_PAD, GLA_COLS), BF16)
    buf[GLA_PAD:c, :] = m_ref[...]
    buf[c:, :] = x_ref[0]

    st[...] = jnp.zeros(st.shape, F32)
    up_b = upb_ref[...]
    bias_b = bb_ref[...]

    def bwd_step(it, carry):
        n = n_chunks - 1 - it
        row0 = pl.multiple_of(n * c, c)
        outs = _gla_chunk(buf, st, row0, up_b, bias_b, True, False)
        for h in range(GLA_HEADS):
            ob[pl.ds(row0, c), h * GLA_DV:(h + 1) * GLA_DV] = outs[h]
        return carry

    lax.fori_loop(0, n_chunks - 1, bwd_step, 0)
    outs = _gla_chunk(buf, st, 0, up_b, bias_b, True, True)
    for h in range(GLA_HEADS):
        ob[0:c, h * GLA_DV:(h + 1) * GLA_DV] = outs[h]

    st[...] = jnp.zeros(st.shape, F32)
    up_f = upf_ref[...]
    bias_f = bf_ref[...]
    gain = gain_ref[...]

    def finish(outs, row0):
        res = []
        for h in range(GLA_HEADS):
            o = outs[h] + ob[pl.ds(row0, c), h * GLA_DV:(h + 1) * GLA_DV]
            g = buf[pl.ds(row0, c), _GG + h * GLA_DV:_GG + (h + 1) * GLA_DV].astype(F32)
            res.append((_rms(o, gain) * (g * jax.nn.sigmoid(g))).astype(BF16))
        return res

    res = finish(_gla_chunk(buf, st, 0, up_f, bias_f, False, True), 0)
    for h in range(GLA_HEADS):
        om_ref[0, :, h * GLA_DV:(h + 1) * GLA_DV] = res[h][GLA_PAD:]

    def fwd_step(n, carry):
        row0 = pl.multiple_of(n * c, c)
        res = finish(_gla_chunk(buf, st, row0, up_f, bias_f, False, False), row0)
        out0 = pl.multiple_of((n - 1) * c, c)
        for h in range(GLA_HEADS):
            ox_ref[0, pl.ds(out0, c), h * GLA_DV:(h + 1) * GLA_DV] = res[h]
        return carry

    lax.fori_loop(1, n_chunks, fwd_step, 0)


def _gla(gl_x, gl_m, up_f, bias_f, up_b, bias_b, gain):
    b, t, _ = gl_x.shape
    lp = t + GLA_CHUNK
    return pl.pallas_call(
        _gla_body,
        grid=(b,),
        in_specs=[
            pl.BlockSpec((1, t, GLA_COLS), lambda bi: (bi, 0, 0)),
            _const_spec(gl_m.shape),
            _const_spec(up_f.shape),
            _const_spec(bias_f.shape),
            _const_spec(up_b.shape),
            _const_spec(bias_b.shape),
            _const_spec(gain.shape),
        ],
        out_specs=[
            pl.BlockSpec((1, t, GLA_VW), lambda bi: (bi, 0, 0)),
            pl.BlockSpec((1, N_META, GLA_VW), lambda bi: (bi, 0, 0)),
        ],
        out_shape=[
            jax.ShapeDtypeStruct((b, t, GLA_VW), BF16),
            jax.ShapeDtypeStruct((b, N_META, GLA_VW), BF16),
        ],
        scratch_shapes=[
            pltpu.VMEM((lp, GLA_COLS), BF16),
            pltpu.VMEM((lp, GLA_VW), F32),
            pltpu.VMEM((GLA_HEADS // 2, GLA_DV, LANES), F32),
        ],
        compiler_params=pltpu.CompilerParams(
            dimension_semantics=("arbitrary",), vmem_limit_bytes=VMEM_LIMIT),
        name="gla",
    )(gl_x, gl_m, up_f, bias_f, up_b, bias_b, gain)


def _mix_body(x_ref, na_ref, gl_ref, wo_ref, g_ref, h_ref):
    mix = (jnp.dot(na_ref[...], wo_ref[:NA_WIDTH], preferred_element_type=F32)
           + jnp.dot(gl_ref[...], wo_ref[NA_WIDTH:], preferred_element_type=F32))
    h_ref[...] = x_ref[...] + _rms(mix, g_ref[...])


def _mix(x2, o_na, o_gl, wo, g, tm):
    rows, d = x2.shape
    return pl.pallas_call(
        _mix_body,
        grid=(rows // tm,),
        in_specs=[
            pl.BlockSpec((tm, d), lambda i: (i, 0)),
            pl.BlockSpec((tm, NA_WIDTH), lambda i: (i, 0)),
            pl.BlockSpec((tm, GLA_VW), lambda i: (i, 0)),
            _const_spec(wo.shape),
            _const_spec(g.shape),
        ],
        out_specs=pl.BlockSpec((tm, d), lambda i: (i, 0)),
        out_shape=jax.ShapeDtypeStruct((rows, d), F32),
        compiler_params=pltpu.CompilerParams(
            dimension_semantics=("arbitrary",), vmem_limit_bytes=VMEM_LIMIT),
        name="mix",
    )(x2, o_na, o_gl, wo, g)


def _ffn_body(hm_ref, hp_ref, hn_ref, hmeta_ref, g1_ref, win_ref, cw_ref, wout_ref, g2_ref,
              o_ref, y_ref, perm_ref, *, tm, n_tiles):
    t = pl.program_id(1)
    sl = SUBLANES
    nv = tm // sl
    n_col = perm_ref.shape[0]

    def restride(x, to_permuted):
        for c in range(n_col):
            perm_ref[c] = x[:, c * LANES:(c + 1) * LANES]
        groups = []
        for k in range(nv):
            if to_permuted:
                start, stride = k, nv
            else:
                start, stride = sl * ((sl * k) % nv) + (sl * k) // nv, sl
            groups.append(jnp.concatenate(
                [perm_ref[c, pl.ds(start, sl, stride=stride), :] for c in range(n_col)], axis=1))
        return jnp.concatenate(groups, axis=0)

    g1 = g1_ref[...]
    n2_main = restride(_rms(hm_ref[0], g1), True)
    prev = jnp.where(t == 0, hmeta_ref[0], hp_ref[0])
    sub = lax.broadcasted_iota(jnp.int32, (sl, 1), 0)
    slab = jnp.where(sub == 0, pltpu.roll(prev, 1, 0),
                     jnp.where(sub == sl - 1, pltpu.roll(hn_ref[0], sl - 1, 0), 0.0))
    keep = jnp.logical_or(sub < sl - 1, t < n_tiles - 1)
    n2 = jnp.concatenate([n2_main, jnp.where(keep, _rms(slab, g1), 0.0)], axis=0).astype(BF16)

    nb = D_FF // FF_BLK
    sub_b = lax.broadcasted_iota(jnp.int32, (sl, FF_BLK), 0)

    def proj(cb):
        va = jnp.dot(n2, win_ref[:, cb * FF_BLK:(cb + 1) * FF_BLK], preferred_element_type=F32)
        ga = jnp.dot(n2, win_ref[:, D_FF + cb * FF_BLK:D_FF + (cb + 1) * FF_BLK],
                     preferred_element_type=F32)
        return va, ga

    def conv(a, taps):
        main = a[:tm]
        hal = a[tm:]
        first_prev = jnp.where(sub_b == 0, hal, pltpu.roll(main[tm - sl:], 1, 0))
        last_next = jnp.where(sub_b == sl - 1, hal, pltpu.roll(main[:sl], sl - 1, 0))
        a_prev = jnp.concatenate([first_prev, main[:tm - sl]], axis=0)
        a_next = jnp.concatenate([main[sl:], last_next], axis=0)
        return a_prev * taps[0:1] + main * taps[1:2] + a_next * taps[2:3] + taps[3:4]

    def act(cb, va, ga):
        val = conv(va, cw_ref[:, cb * FF_BLK:(cb + 1) * FF_BLK])
        gate = conv(ga, cw_ref[:, D_FF + cb * FF_BLK:D_FF + (cb + 1) * FF_BLK])
        y_ref[:, cb * FF_BLK:(cb + 1) * FF_BLK] = (jax.nn.gelu(gate, approximate=True) * val).astype(BF16)

    pending = proj(0)
    for cb in range(nb):
        nxt = proj(cb + 1) if cb + 1 < nb else None
        act(cb, *pending)
        pending = nxt
    r = _rms(jnp.dot(y_ref[...], wout_ref[...], preferred_element_type=F32), g2_ref[...])
    o_ref[0] = hm_ref[0] + restride(r, False)


def _ffn(h1, h1m, g1, win, cw, wout, g2, tm):
    b, t, d = h1.shape
    n_tiles = t // tm
    hb = tm // SUBLANES
    last = t // SUBLANES - 1
    single = pl.Buffered(1)
    return pl.pallas_call(
        functools.partial(_ffn_body, tm=tm, n_tiles=n_tiles),
        grid=(b, n_tiles),
        in_specs=[
            pl.BlockSpec((1, tm, d), lambda bi, ti: (bi, ti, 0)),
            pl.BlockSpec((1, SUBLANES, d), lambda bi, ti: (bi, jnp.maximum(ti * hb - 1, 0), 0)),
            pl.BlockSpec((1, SUBLANES, d), lambda bi, ti: (bi, jnp.minimum((ti + 1) * hb, last), 0)),
            pl.BlockSpec((1, SUBLANES, d), lambda bi, ti: (bi, N_META // SUBLANES - 1, 0)),
            _const_spec(g1.shape),
            pl.BlockSpec(win.shape, lambda bi, ti: (0, 0), pipeline_mode=single),
            _const_spec(cw.shape),
            pl.BlockSpec(wout.shape, lambda bi, ti: (0, 0), pipeline_mode=single),
            _const_spec(g2.shape),
        ],
        out_specs=pl.BlockSpec((1, tm, d), lambda bi, ti: (bi, ti, 0)),
        out_shape=jax.ShapeDtypeStruct((b, t, d), F32),
        scratch_shapes=[pltpu.VMEM((tm, D_FF), BF16), pltpu.VMEM((d // LANES, tm, LANES), F32)],
        compiler_params=pltpu.CompilerParams(
            dimension_semantics=("arbitrary", "arbitrary"), vmem_limit_bytes=VMEM_LIMIT),
        name="ffn",
    )(h1, h1, h1, h1m, g1, win, cw, wout, g2)


def _na_bias_table(rpb):
    w, kw = GRID_W, NA_WIN_COLS
    cq = jnp.arange(w)
    cs = jnp.clip(cq - kw // 2, 0, w - kw)
    in_win = (cq[None, :] >= cs[:, None]) & (cq[None, :] < cs[:, None] + kw)
    dc = jnp.clip(cq[None, :] - cq[:, None], -(kw - 1), kw - 1) + (kw - 1)
    tb = jnp.where(in_win[None, None], rpb[:, :, dc].astype(F32), MASK_NEG)
    return jnp.concatenate([tb[:, :-1], tb[:, 1:]], axis=-1)


def kernel(x, meta_tokens, norm_mix_pre, w_in, na_rel_bias, na_out_gain, gla_gate_up_fwd,
           gla_gate_bias_fwd, gla_gate_up_bwd, gla_gate_bias_bwd, gla_out_gain, w_o, norm_mix_post,
           norm_ffn_pre, w_ffn_in, ffn_conv_w, ffn_conv_b, w_ffn_out, norm_ffn_post):
    b, t, d = x.shape
    depth = w_in.shape[0]
    assert depth == 1, "meta rows are only carried as far as a single layer needs them"
    assert t % GRID_W == 0 and t // GRID_W >= NA_WIN_ROWS and N_META == 2 * SUBLANES
    l = 0
    row = lambda a: a[l].reshape(1, -1).astype(F32)

    wna = w_in[l, :, :NA_COLS].astype(BF16)
    wgl = w_in[l, :, NA_COLS:].astype(BF16)
    wo = w_o[l].astype(BF16)
    win = w_ffn_in[l].astype(BF16)
    wout = w_ffn_out[l].astype(BF16)
    cw = jnp.concatenate([ffn_conv_w[l], ffn_conv_b[l][None]], axis=0).astype(F32)
    t2 = _na_bias_table(na_rel_bias[l])
    zpad = jnp.zeros((GLA_GATE_RANK, GLA_KW), BF16)
    up_f = jnp.concatenate([gla_gate_up_fwd[l].astype(BF16), zpad], axis=0)
    up_b = jnp.concatenate([zpad, gla_gate_up_bwd[l].astype(BF16)], axis=0)

    x2 = x.reshape(b * t, d)
    g_pre = row(norm_mix_pre)
    na_x, gl_x = _inproj(x2, g_pre, wna, wgl, 512)
    na_m, gl_m = _inproj(meta_tokens.astype(F32), g_pre, wna, wgl, N_META)
    na_x = na_x.reshape(b, t, NA_COLS)
    gl_x = gl_x.reshape(b, t, GLA_COLS)

    na_gain = row(na_out_gain)
    o_na = _na(na_x, na_m, t2, na_gain, 4)
    o_na_m = _na_meta(na_m, na_gain)
    o_gl, o_gl_m = _gla(gl_x, gl_m, up_f, row(gla_gate_bias_fwd), up_b, row(gla_gate_bias_bwd),
                        row(gla_out_gain))

    g_post = row(norm_mix_post)
    h1 = _mix(x2, o_na.reshape(b * t, NA_WIDTH), o_gl.reshape(b * t, GLA_VW), wo, g_post, 512)
    h1m = _mix(jnp.broadcast_to(meta_tokens.astype(F32)[None], (b, N_META, d)).reshape(b * N_META, d),
               jnp.broadcast_to(o_na_m[None], (b, N_META, NA_WIDTH)).reshape(b * N_META, NA_WIDTH),
               o_gl_m.reshape(b * N_META, GLA_VW), wo, g_post, b * N_META)
    out = _ffn(h1.reshape(b, t, d), h1m.reshape(b, N_META, d), row(norm_ffn_pre), win, cw, wout,
               row(norm_ffn_post), 512)
    return out
```

```python
import functools

import jax
import jax.numpy as jnp
from jax import lax
from jax.experimental import pallas as pl
from jax.experimental.pallas import tpu as pltpu

F32 = jnp.float32
BF16 = jnp.bfloat16

N_META = 16
GRID_W = 64
NA_WIN_ROWS = 8
NA_WIN_COLS = 16
NA_HEADS = 8
NA_HEAD_DIM = 64
NA_WIDTH = NA_HEADS * NA_HEAD_DIM
GLA_HEADS = 4
GLA_DK = 64
GLA_DV = 128
GLA_KW = GLA_HEADS * GLA_DK
GLA_VW = GLA_HEADS * GLA_DV
GLA_GATE_RANK = 16
GLA_GATE_TAU = 16.0
GLA_CHUNK = 64
GLA_PAD = (-N_META) % GLA_CHUNK
NA_COLS = 3 * NA_WIDTH
GLA_COLS = 2 * GLA_KW + 2 * GLA_VW + 2 * GLA_GATE_RANK
D_FF = 2816
FF_BLK = 256
CONV_W = 3
RMS_EPS = 1e-6
MASK_NEG = -1e30

LANES = 128
SUBLANES = 8
VMEM_LIMIT = 56 * 1024 * 1024

_CONTRACT_LAST = (((1,), (1,)), ((), ()))
_CONTRACT_FIRST = (((0,), (0,)), ((), ()))


def _rms(x, g):
    return x * lax.rsqrt(jnp.mean(x * x, axis=-1, keepdims=True) + RMS_EPS) * g


def _const_spec(shape):
    nd = len(shape)
    return pl.BlockSpec(shape, lambda *_: (0,) * nd)


def _inproj_body(x_ref, g_ref, wna_ref, wgl_ref, na_ref, gl_ref):
    u = _rms(x_ref[...], g_ref[...]).astype(BF16)
    na_ref[...] = jnp.dot(u, wna_ref[...], preferred_element_type=F32).astype(BF16)
    gl_ref[...] = jnp.dot(u, wgl_ref[...], preferred_element_type=F32).astype(BF16)


def _inproj(x2, g, wna, wgl, tm):
    rows, d = x2.shape
    return pl.pallas_call(
        _inproj_body,
        grid=(rows // tm,),
        in_specs=[
            pl.BlockSpec((tm, d), lambda i: (i, 0)),
            _const_spec(g.shape),
            _const_spec(wna.shape),
            _const_spec(wgl.shape),
        ],
        out_specs=[
            pl.BlockSpec((tm, NA_COLS), lambda i: (i, 0)),
            pl.BlockSpec((tm, GLA_COLS), lambda i: (i, 0)),
        ],
        out_shape=[
            jax.ShapeDtypeStruct((rows, NA_COLS), BF16),
            jax.ShapeDtypeStruct((rows, GLA_COLS), BF16),
        ],
        compiler_params=pltpu.CompilerParams(
            dimension_semantics=("arbitrary",), vmem_limit_bytes=VMEM_LIMIT),
        name="inproj",
    )(x2, g, wna, wgl)


def _split_heads_rows(pair, lo):
    zero = jnp.zeros_like(pair)
    return jnp.concatenate([jnp.where(lo, pair, zero), jnp.where(lo, zero, pair)], axis=0)


def _na_body(q_ref, k_ref, v_ref, km_ref, vm_ref, t2_ref, gain_ref, o_ref, *, rq, n_rows):
    j = pl.program_id(1)
    w = GRID_W
    kh = NA_WIN_ROWS
    lo = lax.broadcasted_iota(jnp.int32, (w, LANES), 1) < NA_HEAD_DIM
    n_pairs = NA_HEADS // 2
    units = [(i, p) for i in range(rq) for p in range(n_pairs)]

    def stage1(i, p):
        r = j * rq + i
        rs = jnp.clip(r - kh // 2, 0, n_rows - kh)
        e0 = rs - r + (NA_WIN_ROWS - 1)
        k0 = pl.multiple_of(rs * w, w)
        cols = slice(p * LANES, (p + 1) * LANES)
        qp = q_ref[0, i * w:(i + 1) * w, cols] * (NA_HEAD_DIM ** -0.5)
        q2 = _split_heads_rows(qp, lo)
        kw = k_ref[0, pl.ds(k0, kh * w), cols]
        s = lax.dot_general(q2, kw, _CONTRACT_LAST, preferred_element_type=F32)
        bias = jnp.concatenate(
            [jnp.concatenate([t2_ref[2 * p + hh, e0 + 2 * jj] for jj in range(kh // 2)], axis=1)
             for hh in range(2)], axis=0)
        s = s + bias
        sm = lax.dot_general(q2, km_ref[:, cols], _CONTRACT_LAST, preferred_element_type=F32)
        m = jnp.maximum(jnp.max(s, axis=-1, keepdims=True), jnp.max(sm, axis=-1, keepdims=True))
        return s, sm, m, k0

    def stage2(s, sm, m):
        pw = jnp.exp(s - m)
        pm = jnp.exp(sm - m)
        l = jnp.sum(pw, axis=-1, keepdims=True) + jnp.sum(pm, axis=-1, keepdims=True)
        return pw.astype(BF16), pm.astype(BF16), l

    def stage3(p, k0, pw, pm, l):
        cols = slice(p * LANES, (p + 1) * LANES)
        vw = v_ref[0, pl.ds(k0, kh * w), cols]
        o2 = (jnp.dot(pw, vw, preferred_element_type=F32)
              + jnp.dot(pm, vm_ref[:, cols], preferred_element_type=F32))
        o2 = o2 / l
        return jnp.where(lo, o2[:w], o2[w:])

    n = len(units)
    r1, r2, outs = {}, {}, {}
    for step in range(n + 2):
        if step < n:
            r1[step] = stage1(*units[step])
        if 0 <= step - 1 < n:
            s, sm, m, k0 = r1.pop(step - 1)
            r2[step - 1] = stage2(s, sm, m) + (k0,)
        if 0 <= step - 2 < n:
            pw, pm, l, k0 = r2.pop(step - 2)
            outs[units[step - 2]] = stage3(units[step - 2][1], k0, pw, pm, l)
    for i in range(rq):
        ssq = jnp.zeros((w, 1), F32)
        for p in range(n_pairs):
            ssq = ssq + jnp.sum(outs[(i, p)] * outs[(i, p)], axis=-1, keepdims=True)
        inv = lax.rsqrt(ssq * (1.0 / NA_WIDTH) + RMS_EPS)
        for p in range(n_pairs):
            cols = slice(p * LANES, (p + 1) * LANES)
            o_ref[0, i * w:(i + 1) * w, cols] = (outs[(i, p)] * inv * gain_ref[:, cols]).astype(BF16)


def _na(na_x, na_m, t2, gain, rq):
    b, t, _ = na_x.shape
    n_rows = t // GRID_W
    nw = NA_WIDTH
    return pl.pallas_call(
        functools.partial(_na_body, rq=rq, n_rows=n_rows),
        grid=(b, n_rows // rq),
        in_specs=[
            pl.BlockSpec((1, rq * GRID_W, nw), lambda bi, j: (bi, j, 0)),
            pl.BlockSpec((1, t, nw), lambda bi, j: (bi, 0, 1)),
            pl.BlockSpec((1, t, nw), lambda bi, j: (bi, 0, 2)),
            pl.BlockSpec((N_META, nw), lambda bi, j: (0, 1)),
            pl.BlockSpec((N_META, nw), lambda bi, j: (0, 2)),
            _const_spec(t2.shape),
            _const_spec(gain.shape),
        ],
        out_specs=pl.BlockSpec((1, rq * GRID_W, nw), lambda bi, j: (bi, j, 0)),
        out_shape=jax.ShapeDtypeStruct((b, t, nw), BF16),
        compiler_params=pltpu.CompilerParams(
            dimension_semantics=("arbitrary", "arbitrary"), vmem_limit_bytes=VMEM_LIMIT),
        name="na",
    )(na_x, na_x, na_x, na_m, na_m, t2, gain)


def _na_meta_body(q_ref, k_ref, v_ref, gain_ref, o_ref):
    lane = lax.broadcasted_iota(jnp.int32, (N_META, NA_WIDTH), 1)
    q = q_ref[...] * (NA_HEAD_DIM ** -0.5)
    k = k_ref[...]
    v = v_ref[...]
    om = jnp.zeros((N_META, NA_WIDTH), F32)
    for h in range(NA_HEADS):
        in_head = (lane >= h * NA_HEAD_DIM) & (lane < (h + 1) * NA_HEAD_DIM)
        qh = jnp.where(in_head, q, jnp.zeros_like(q))
        s = lax.dot_general(qh, k, _CONTRACT_LAST, preferred_element_type=F32)
        m = jnp.max(s, axis=-1, keepdims=True)
        pw = jnp.exp(s - m)
        pw = pw / jnp.sum(pw, axis=-1, keepdims=True)
        oh = jnp.dot(pw.astype(BF16), v, preferred_element_type=F32)
        om = jnp.where(in_head, oh, om)
    o_ref[...] = _rms(om, gain_ref[...]).astype(BF16)


def _na_meta(na_m, gain):
    nw = NA_WIDTH
    return pl.pallas_call(
        _na_meta_body,
        grid=(1,),
        in_specs=[
            pl.BlockSpec((N_META, nw), lambda i: (0, 0)),
            pl.BlockSpec((N_META, nw), lambda i: (0, 1)),
            pl.BlockSpec((N_META, nw), lambda i: (0, 2)),
            _const_spec(gain.shape),
        ],
        out_specs=pl.BlockSpec((N_META, nw), lambda i: (0, 0)),
        out_shape=jax.ShapeDtypeStruct((N_META, nw), BF16),
        name="na_meta",
    )(na_m, na_m, na_m, gain)


_GQ, _GK, _GV, _GG, _GZ = 0, GLA_KW, 2 * GLA_KW, 2 * GLA_KW + GLA_VW, 2 * GLA_KW + 2 * GLA_VW


GLA_GROUP = 4


def _gla_body(x_ref, m_ref, up_ref, bias_ref, gain_ref, ox_ref, om_ref, c0, qd, oi, dst, decs, st):
    c = GLA_CHUNK
    kw = GLA_KW
    n_pairs = GLA_HEADS // 2
    t = x_ref.shape[1]
    n_chunks = t // c + 1
    c0[0:GLA_PAD, :] = jnp.zeros((GLA_PAD, GLA_COLS), BF16)
    c0[GLA_PAD:, :] = m_ref[...]

    ti = lax.broadcasted_iota(jnp.int32, (c, c), 0)
    si = lax.broadcasted_iota(jnp.int32, (c, c), 1)
    tri = jnp.where(si <= ti, 1.0, 0.0).astype(BF16)
    lo_c = lax.broadcasted_iota(jnp.int32, (c, LANES), 1) < GLA_DK
    lo_s = lax.broadcasted_iota(jnp.int32, (GLA_DV, 2 * LANES), 1) % LANES < GLA_DK
    t2 = lax.broadcasted_iota(jnp.int32, (2 * c, c), 0) % c
    s2 = lax.broadcasted_iota(jnp.int32, (2 * c, c), 1)
    keep_f = s2 <= t2
    up = up_ref[...]
    bias = bias_ref[...]
    gain = gain_ref[...]

    def reader(n):
        if isinstance(n, int):
            assert n == 0
            return lambda a, b: c0[:, a:b]
        r0 = pl.multiple_of((n - 1) * c, c)
        return lambda a, b: x_ref[0, pl.ds(r0, c), a:b]

    def prow(n):
        return pl.ds(n * c, c) if isinstance(n, int) else pl.ds(pl.multiple_of(n * c, c), c)

    def a0(n):
        return (jnp.dot(reader(n)(_GZ, _GZ + 2 * GLA_GATE_RANK), up, preferred_element_type=F32),)

    def a1(n, gate):
        gate = gate + bias
        la = (jnp.minimum(gate, 0.0) - jnp.log1p(jnp.exp(-jnp.abs(gate)))) * (1.0 / GLA_GATE_TAU)
        if isinstance(n, int):
            la = jnp.where(lax.broadcasted_iota(jnp.int32, (c, 1), 0) >= GLA_PAD, la, 0.0)
        hi = la.astype(BF16)
        low = (la - hi.astype(F32)).astype(BF16)
        cs = jnp.dot(tri, jnp.concatenate([hi, low], axis=1), preferred_element_type=F32)
        return la, cs

    def a2(n, la, cs):
        rd = reader(n)
        pre = cs[:, :2 * kw] + cs[:, 2 * kw:]
        b_f = pre[:, :kw]
        bl_f = b_f[c - 1:c]
        bl_b = pre[c - 1:c, kw:]
        b_b = bl_b - pre[:, kw:] + la[:, kw:]
        q = rd(_GQ, _GQ + kw).astype(F32) * (GLA_DK ** -0.5)
        k = rd(_GK, _GK + kw).astype(F32)
        qd_f = (q * jnp.exp(b_f)).astype(BF16)
        qd_b = (q * jnp.exp(b_b)).astype(BF16)
        ki_f = (k * jnp.exp(-b_f)).astype(BF16)
        ki_b = (k * jnp.exp(-b_b)).astype(BF16)
        ke = jnp.concatenate([k * jnp.exp(bl_f - b_f), k * jnp.exp(bl_b - b_b)], axis=1).astype(BF16)
        dec = jnp.concatenate([jnp.exp(bl_f), jnp.exp(bl_b)], axis=1)
        qd[prow(n), :kw] = qd_f
        qd[prow(n), kw:] = qd_b
        drow = n * SUBLANES if isinstance(n, int) else pl.multiple_of(n * SUBLANES, SUBLANES)
        decs[pl.ds(drow, SUBLANES), :] = jnp.broadcast_to(dec, (SUBLANES, 2 * kw))
        araw = []
        for p in range(n_pairs):
            cols = slice(p * LANES, (p + 1) * LANES)
            a_f = lax.dot_general(_split_heads_rows(qd_f[:, cols], lo_c), ki_f[:, cols], _CONTRACT_LAST,
                                  preferred_element_type=F32)
            a_b = lax.dot_general(_split_heads_rows(qd_b[:, cols], lo_c), ki_b[:, cols], _CONTRACT_LAST,
                                  preferred_element_type=F32)
            araw.append((a_f, a_b))
        return araw, ke

    def a3(n, araw, ke):
        rd = reader(n)
        amat = [jnp.where(keep_f, a_f, a_b).astype(BF16) for a_f, a_b in araw]
        intra, incr = [], []
        for p in range(n_pairs):
            kcat = jnp.concatenate([ke[:, p * LANES:(p + 1) * LANES],
                                    ke[:, kw + p * LANES:kw + (p + 1) * LANES]], axis=1)
            for hh in range(2):
                h = 2 * p + hh
                vh = rd(_GV + h * GLA_DV, _GV + (h + 1) * GLA_DV)
                intra.append(jnp.dot(amat[p][hh * c:(hh + 1) * c], vh, preferred_element_type=F32))
                incr.append(lax.dot_general(vh, kcat, _CONTRACT_FIRST, preferred_element_type=F32))
        return intra, incr

    def a4(n, intra, incr):
        for h in range(GLA_HEADS):
            oi[prow(n), h * GLA_DV:(h + 1) * GLA_DV] = intra[h]
        for p in range(n_pairs):
            dst[n, p] = jnp.where(lo_s, incr[2 * p], incr[2 * p + 1])

    def c1(n):
        inter = []
        for p in range(n_pairs):
            qf = qd[prow(n), p * LANES:(p + 1) * LANES]
            qb = qd[prow(n), kw + p * LANES:kw + (p + 1) * LANES]
            q2 = jnp.concatenate([_split_heads_rows(qf, lo_c), _split_heads_rows(qb, lo_c)], axis=1)
            inter.append(lax.dot_general(q2, dst[n, p].astype(BF16), _CONTRACT_LAST,
                                         preferred_element_type=F32))
        return (inter,)

    def c2(n, inter):
        rd = reader(n)
        for h in range(GLA_HEADS):
            p, hh = divmod(h, 2)
            hc = slice(h * GLA_DV, (h + 1) * GLA_DV)
            o = oi[prow(n), hc] + inter[p][hh * c:(hh + 1) * c]
            g = rd(_GG + h * GLA_DV, _GG + (h + 1) * GLA_DV).astype(F32)
            res = (_rms(o, gain) * (g * jax.nn.sigmoid(g))).astype(BF16)
            if isinstance(n, int):
                om_ref[0, :, hc] = res[GLA_PAD:]
            else:
                ox_ref[0, pl.ds(pl.multiple_of((n - 1) * c, c), c), hc] = res

    def run_group(stages, chunks):
        vals = [()] * len(chunks)
        for stage in stages:
            vals = [stage(n, *v) or () for n, v in zip(chunks, vals)]

    n_groups = (n_chunks - 1) // GLA_GROUP

    def group_chunks(i):
        return [1 + GLA_GROUP * i + u for u in range(GLA_GROUP)]

    stages_a = [a0, a1, a2, a3, a4]
    run_group(stages_a, [0])

    def pass_a(i, carry):
        run_group(stages_a, group_chunks(i))
        return carry

    lax.fori_loop(0, n_groups, pass_a, 0)

    st[...] = jnp.zeros(st.shape, F32)

    def scan(it, carry):
        for n, lanes, off in ((it, slice(0, LANES), 0), (n_chunks - 1 - it, slice(LANES, 2 * LANES), kw)):
            drow = pl.multiple_of(n * SUBLANES, SUBLANES)
            for p in range(n_pairs):
                inc = dst[n, p, :, lanes]
                s_in = st[p, :, lanes]
                dst[n, p, :, lanes] = s_in
                dec = decs[pl.ds(drow, 1), off + p * LANES:off + (p + 1) * LANES]
                st[p, :, lanes] = dec * s_in + inc
        return carry

    lax.fori_loop(0, n_chunks, scan, 0)

    run_group([c1, c2], [0])

    def pass_c(i, carry):
        run_group([c1, c2], group_chunks(i))
        return carry

    lax.fori_loop(0, n_groups, pass_c, 0)


def _gla(gl_x, gl_m, up, bias, gain):
    b, t, _ = gl_x.shape
    assert (t // GLA_CHUNK) % GLA_GROUP == 0
    n_chunks = t // GLA_CHUNK + 1
    lp = n_chunks * GLA_CHUNK
    n_pairs = GLA_HEADS // 2
    return pl.pallas_call(
        _gla_body,
        grid=(b,),
        in_specs=[
            pl.BlockSpec((1, t, GLA_COLS), lambda bi: (bi, 0, 0)),
            _const_spec(gl_m.shape),
            _const_spec(up.shape),
            _const_spec(bias.shape),
            _const_spec(gain.shape),
        ],
        out_specs=[
            pl.BlockSpec((1, t, GLA_VW), lambda bi: (bi, 0, 0)),
            pl.BlockSpec((1, N_META, GLA_VW), lambda bi: (bi, 0, 0)),
        ],
        out_shape=[
            jax.ShapeDtypeStruct((b, t, GLA_VW), BF16),
            jax.ShapeDtypeStruct((b, N_META, GLA_VW), BF16),
        ],
        scratch_shapes=[
            pltpu.VMEM((GLA_CHUNK, GLA_COLS), BF16),
            pltpu.VMEM((lp, 2 * GLA_KW), BF16),
            pltpu.VMEM((lp, GLA_VW), F32),
            pltpu.VMEM((n_chunks, n_pairs, GLA_DV, 2 * LANES), F32),
            pltpu.VMEM((n_chunks * SUBLANES, 2 * GLA_KW), F32),
            pltpu.VMEM((n_pairs, GLA_DV, 2 * LANES), F32),
        ],
        compiler_params=pltpu.CompilerParams(
            dimension_semantics=("arbitrary",), vmem_limit_bytes=VMEM_LIMIT),
        name="gla",
    )(gl_x, gl_m, up, bias, gain)


def _mix_body(x_ref, na_ref, gl_ref, wo_ref, g_ref, h_ref):
    mix = (jnp.dot(na_ref[...], wo_ref[:NA_WIDTH], preferred_element_type=F32)
           + jnp.dot(gl_ref[...], wo_ref[NA_WIDTH:], preferred_element_type=F32))
    h_ref[...] = x_ref[...] + _rms(mix, g_ref[...])


def _mix(x2, o_na, o_gl, wo, g, tm):
    rows, d = x2.shape
    return pl.pallas_call(
        _mix_body,
        grid=(rows // tm,),
        in_specs=[
            pl.BlockSpec((tm, d), lambda i: (i, 0)),
            pl.BlockSpec((tm, NA_WIDTH), lambda i: (i, 0)),
            pl.BlockSpec((tm, GLA_VW), lambda i: (i, 0)),
            _const_spec(wo.shape),
            _const_spec(g.shape),
        ],
        out_specs=pl.BlockSpec((tm, d), lambda i: (i, 0)),
        out_shape=jax.ShapeDtypeStruct((rows, d), F32),
        compiler_params=pltpu.CompilerParams(
            dimension_semantics=("arbitrary",), vmem_limit_bytes=VMEM_LIMIT),
        name="mix",
    )(x2, o_na, o_gl, wo, g)


def _ffn_body(hm_ref, hp_ref, hn_ref, hmeta_ref, g1_ref, win_ref, cw_ref, wout_ref, g2_ref,
              o_ref, y_ref, perm_ref, *, tm, n_tiles):
    t = pl.program_id(1)
    sl = SUBLANES
    nv = tm // sl
    n_col = perm_ref.shape[0]

    def restride(x, to_permuted):
        for c in range(n_col):
            perm_ref[c] = x[:, c * LANES:(c + 1) * LANES]
        groups = []
        for k in range(nv):
            if to_permuted:
                start, stride = k, nv
            else:
                start, stride = sl * ((sl * k) % nv) + (sl * k) // nv, sl
            groups.append(jnp.concatenate(
                [perm_ref[c, pl.ds(start, sl, stride=stride), :] for c in range(n_col)], axis=1))
        return jnp.concatenate(groups, axis=0)

    g1 = g1_ref[...]
    n2_main = restride(_rms(hm_ref[0], g1), True)
    prev = jnp.where(t == 0, hmeta_ref[0], hp_ref[0])
    sub = lax.broadcasted_iota(jnp.int32, (sl, 1), 0)
    slab = jnp.where(sub == 0, pltpu.roll(prev, 1, 0),
                     jnp.where(sub == sl - 1, pltpu.roll(hn_ref[0], sl - 1, 0), 0.0))
    keep = jnp.logical_or(sub < sl - 1, t < n_tiles - 1)
    n2 = jnp.concatenate([n2_main, jnp.where(keep, _rms(slab, g1), 0.0)], axis=0).astype(BF16)

    nb = D_FF // FF_BLK
    sub_b = lax.broadcasted_iota(jnp.int32, (sl, FF_BLK), 0)

    def proj(cb):
        va = jnp.dot(n2, win_ref[:, cb * FF_BLK:(cb + 1) * FF_BLK], preferred_element_type=F32)
        ga = jnp.dot(n2, win_ref[:, D_FF + cb * FF_BLK:D_FF + (cb + 1) * FF_BLK],
                     preferred_element_type=F32)
        return va, ga

    def conv(a, taps):
        main = a[:tm]
        hal = a[tm:]
        first_prev = jnp.where(sub_b == 0, hal, pltpu.roll(main[tm - sl:], 1, 0))
        last_next = jnp.where(sub_b == sl - 1, hal, pltpu.roll(main[:sl], sl - 1, 0))
        a_prev = jnp.concatenate([first_prev, main[:tm - sl]], axis=0)
        a_next = jnp.concatenate([main[sl:], last_next], axis=0)
        return a_prev * taps[0:1] + main * taps[1:2] + a_next * taps[2:3] + taps[3:4]

    def act(cb, va, ga):
        val = conv(va, cw_ref[:, cb * FF_BLK:(cb + 1) * FF_BLK])
        gate = conv(ga, cw_ref[:, D_FF + cb * FF_BLK:D_FF + (cb + 1) * FF_BLK])
        y_ref[:, cb * FF_BLK:(cb + 1) * FF_BLK] = (jax.nn.gelu(gate, approximate=True) * val).astype(BF16)

    pending = proj(0)
    for cb in range(nb):
        nxt = proj(cb + 1) if cb + 1 < nb else None
        act(cb, *pending)
        pending = nxt
    r = _rms(jnp.dot(y_ref[...], wout_ref[...], preferred_element_type=F32), g2_ref[...])
    o_ref[0] = hm_ref[0] + restride(r, False)


def _ffn(h1, h1m, g1, win, cw, wout, g2, tm):
    b, t, d = h1.shape
    n_tiles = t // tm
    hb = tm // SUBLANES
    last = t // SUBLANES - 1
    single = pl.Buffered(1)
    return pl.pallas_call(
        functools.partial(_ffn_body, tm=tm, n_tiles=n_tiles),
        grid=(b, n_tiles),
        in_specs=[
            pl.BlockSpec((1, tm, d), lambda bi, ti: (bi, ti, 0)),
            pl.BlockSpec((1, SUBLANES, d), lambda bi, ti: (bi, jnp.maximum(ti * hb - 1, 0), 0)),
            pl.BlockSpec((1, SUBLANES, d), lambda bi, ti: (bi, jnp.minimum((ti + 1) * hb, last), 0)),
            pl.BlockSpec((1, SUBLANES, d), lambda bi, ti: (bi, N_META // SUBLANES - 1, 0)),
            _const_spec(g1.shape),
            pl.BlockSpec(win.shape, lambda bi, ti: (0, 0), pipeline_mode=single),
            _const_spec(cw.shape),
            pl.BlockSpec(wout.shape, lambda bi, ti: (0, 0), pipeline_mode=single),
            _const_spec(g2.shape),
        ],
        out_specs=pl.BlockSpec((1, tm, d), lambda bi, ti: (bi, ti, 0)),
        out_shape=jax.ShapeDtypeStruct((b, t, d), F32),
        scratch_shapes=[pltpu.VMEM((tm, D_FF), BF16), pltpu.VMEM((d // LANES, tm, LANES), F32)],
        compiler_params=pltpu.CompilerParams(
            dimension_semantics=("arbitrary", "arbitrary"), vmem_limit_bytes=VMEM_LIMIT),
        name="ffn",
    )(h1, h1, h1, h1m, g1, win, cw, wout, g2)


def _na_bias_table(rpb):
    w, kw = GRID_W, NA_WIN_COLS
    cq = jnp.arange(w)
    cs = jnp.clip(cq - kw // 2, 0, w - kw)
    in_win = (cq[None, :] >= cs[:, None]) & (cq[None, :] < cs[:, None] + kw)
    dc = jnp.clip(cq[None, :] - cq[:, None], -(kw - 1), kw - 1) + (kw - 1)
    tb = jnp.where(in_win[None, None], rpb[:, :, dc].astype(F32), MASK_NEG)
    return jnp.concatenate([tb[:, :-1], tb[:, 1:]], axis=-1)


def kernel(x, meta_tokens, norm_mix_pre, w_in, na_rel_bias, na_out_gain, gla_gate_up_fwd,
           gla_gate_bias_fwd, gla_gate_up_bwd, gla_gate_bias_bwd, gla_out_gain, w_o, norm_mix_post,
           norm_ffn_pre, w_ffn_in, ffn_conv_w, ffn_conv_b, w_ffn_out, norm_ffn_post):
    b, t, d = x.shape
    depth = w_in.shape[0]
    assert depth == 1, "meta rows are only carried as far as a single layer needs them"
    assert t % GRID_W == 0 and t // GRID_W >= NA_WIN_ROWS and N_META == 2 * SUBLANES
    l = 0
    row = lambda a: a[l].reshape(1, -1).astype(F32)

    wna = w_in[l, :, :NA_COLS].astype(BF16)
    wgl = w_in[l, :, NA_COLS:].astype(BF16)
    wo = w_o[l].astype(BF16)
    win = w_ffn_in[l].astype(BF16)
    wout = w_ffn_out[l].astype(BF16)
    cw = jnp.concatenate([ffn_conv_w[l], ffn_conv_b[l][None]], axis=0).astype(F32)
    t2 = _na_bias_table(na_rel_bias[l])
    zpad = jnp.zeros((GLA_GATE_RANK, GLA_KW), BF16)
    gate_up = jnp.concatenate(
        [jnp.concatenate([gla_gate_up_fwd[l].astype(BF16), zpad], axis=1),
         jnp.concatenate([zpad, gla_gate_up_bwd[l].astype(BF16)], axis=1)], axis=0)
    gate_bias = jnp.concatenate([row(gla_gate_bias_fwd), row(gla_gate_bias_bwd)], axis=1)

    x2 = x.reshape(b * t, d)
    g_pre = row(norm_mix_pre)
    na_x, gl_x = _inproj(x2, g_pre, wna, wgl, 512)
    na_m, gl_m = _inproj(meta_tokens.astype(F32), g_pre, wna, wgl, N_META)
    na_x = na_x.reshape(b, t, NA_COLS)
    gl_x = gl_x.reshape(b, t, GLA_COLS)

    na_gain = row(na_out_gain)
    o_na = _na(na_x, na_m, t2, na_gain, 4)
    o_na_m = _na_meta(na_m, na_gain)
    o_gl, o_gl_m = _gla(gl_x, gl_m, gate_up, gate_bias, row(gla_out_gain))

    g_post = row(norm_mix_post)
    h1 = _mix(x2, o_na.reshape(b * t, NA_WIDTH), o_gl.reshape(b * t, GLA_VW), wo, g_post, 512)
    h1m = _mix(jnp.broadcast_to(meta_tokens.astype(F32)[None], (b, N_META, d)).reshape(b * N_META, d),
               jnp.broadcast_to(o_na_m[None], (b, N_META, NA_WIDTH)).reshape(b * N_META, NA_WIDTH),
               o_gl_m.reshape(b * N_META, GLA_VW), wo, g_post, b * N_META)
    out = _ffn(h1.reshape(b, t, d), h1m.reshape(b, N_META, d), row(norm_ffn_pre), win, cw, wout,
               row(norm_ffn_post), 512)
    return out
```

```python
import functools

import jax
import jax.numpy as jnp
from jax import lax
from jax.experimental import pallas as pl
from jax.experimental.pallas import tpu as pltpu

F32 = jnp.float32
BF16 = jnp.bfloat16

N_META = 16
GRID_W = 64
NA_WIN_ROWS = 8
NA_WIN_COLS = 16
NA_HEADS = 8
NA_HEAD_DIM = 64
NA_WIDTH = NA_HEADS * NA_HEAD_DIM
GLA_HEADS = 4
GLA_DK = 64
GLA_DV = 128
GLA_KW = GLA_HEADS * GLA_DK
GLA_VW = GLA_HEADS * GLA_DV
GLA_GATE_RANK = 16
GLA_GATE_TAU = 16.0
GLA_CHUNK = 64
GLA_PAD = (-N_META) % GLA_CHUNK
NA_COLS = 3 * NA_WIDTH
GLA_COLS = 2 * GLA_KW + 2 * GLA_VW + 2 * GLA_GATE_RANK
D_FF = 2816
FF_BLK = 256
CONV_W = 3
RMS_EPS = 1e-6
MASK_NEG = -1e30

LANES = 128
SUBLANES = 8
HALO_ROWS = 16
VMEM_LIMIT = 56 * 1024 * 1024

_CONTRACT_LAST = (((1,), (1,)), ((), ()))
_CONTRACT_FIRST = (((0,), (0,)), ((), ()))


def _rms(x, g):
    return x * lax.rsqrt(jnp.mean(x * x, axis=-1, keepdims=True) + RMS_EPS) * g


def _aligned(v, m):
    return v if isinstance(v, int) else pl.multiple_of(v, m)


def _const_spec(shape):
    nd = len(shape)
    return pl.BlockSpec(shape, lambda *_: (0,) * nd)


def _inproj_body(x_ref, g_ref, wna_ref, wgl_ref, na_ref, gl_ref):
    u = _rms(x_ref[...], g_ref[...]).astype(BF16)
    na_ref[...] = jnp.dot(u, wna_ref[...], preferred_element_type=F32).astype(BF16)
    gl_ref[...] = jnp.dot(u, wgl_ref[...], preferred_element_type=F32).astype(BF16)


def _inproj(x2, g, wna, wgl, tm):
    rows, d = x2.shape
    return pl.pallas_call(
        _inproj_body,
        grid=(rows // tm,),
        in_specs=[
            pl.BlockSpec((tm, d), lambda i: (i, 0)),
            _const_spec(g.shape),
            _const_spec(wna.shape),
            _const_spec(wgl.shape),
        ],
        out_specs=[
            pl.BlockSpec((tm, NA_COLS), lambda i: (i, 0)),
            pl.BlockSpec((tm, GLA_COLS), lambda i: (i, 0)),
        ],
        out_shape=[
            jax.ShapeDtypeStruct((rows, NA_COLS), BF16),
            jax.ShapeDtypeStruct((rows, GLA_COLS), BF16),
        ],
        compiler_params=pltpu.CompilerParams(
            dimension_semantics=("arbitrary",), vmem_limit_bytes=VMEM_LIMIT),
        name="inproj",
    )(x2, g, wna, wgl)


def _split_heads_rows(pair, lo):
    zero = jnp.zeros_like(pair)
    return jnp.concatenate([jnp.where(lo, pair, zero), jnp.where(lo, zero, pair)], axis=0)


def _na_body(q_ref, k_ref, v_ref, km_ref, vm_ref, t2_ref, gain_ref, o_ref, *, rq, n_rows):
    j = pl.program_id(1)
    w = GRID_W
    kh = NA_WIN_ROWS
    lo = lax.broadcasted_iota(jnp.int32, (w, LANES), 1) < NA_HEAD_DIM
    n_pairs = NA_HEADS // 2
    units = [(i, p) for i in range(rq) for p in range(n_pairs)]

    def stage1(i, p):
        r = j * rq + i
        rs = jnp.clip(r - kh // 2, 0, n_rows - kh)
        e0 = rs - r + (NA_WIN_ROWS - 1)
        k0 = pl.multiple_of(rs * w, w)
        cols = slice(p * LANES, (p + 1) * LANES)
        qp = q_ref[0, i * w:(i + 1) * w, cols] * (NA_HEAD_DIM ** -0.5)
        q2 = _split_heads_rows(qp, lo)
        kw = k_ref[0, pl.ds(k0, kh * w), cols]
        s = lax.dot_general(q2, kw, _CONTRACT_LAST, preferred_element_type=F32)
        bias = jnp.concatenate(
            [jnp.concatenate([t2_ref[2 * p + hh, e0 + 2 * jj] for jj in range(kh // 2)], axis=1)
             for hh in range(2)], axis=0)
        s = s + bias
        sm = lax.dot_general(q2, km_ref[:, cols], _CONTRACT_LAST, preferred_element_type=F32)
        m = jnp.maximum(jnp.max(s, axis=-1, keepdims=True), jnp.max(sm, axis=-1, keepdims=True))
        return s, sm, m, k0

    def stage2(s, sm, m):
        pw = jnp.exp(s - m)
        pm = jnp.exp(sm - m)
        l = jnp.sum(pw, axis=-1, keepdims=True) + jnp.sum(pm, axis=-1, keepdims=True)
        return pw.astype(BF16), pm.astype(BF16), l

    def stage3(p, k0, pw, pm, l):
        cols = slice(p * LANES, (p + 1) * LANES)
        vw = v_ref[0, pl.ds(k0, kh * w), cols]
        o2 = (jnp.dot(pw, vw, preferred_element_type=F32)
              + jnp.dot(pm, vm_ref[:, cols], preferred_element_type=F32))
        o2 = o2 / l
        return jnp.where(lo, o2[:w], o2[w:])

    n = len(units)
    r1, r2, outs = {}, {}, {}
    for step in range(n + 2):
        if step < n:
            r1[step] = stage1(*units[step])
        if 0 <= step - 1 < n:
            s, sm, m, k0 = r1.pop(step - 1)
            r2[step - 1] = stage2(s, sm, m) + (k0,)
        if 0 <= step - 2 < n:
            pw, pm, l, k0 = r2.pop(step - 2)
            outs[units[step - 2]] = stage3(units[step - 2][1], k0, pw, pm, l)
    for i in range(rq):
        ssq = jnp.zeros((w, 1), F32)
        for p in range(n_pairs):
            ssq = ssq + jnp.sum(outs[(i, p)] * outs[(i, p)], axis=-1, keepdims=True)
        inv = lax.rsqrt(ssq * (1.0 / NA_WIDTH) + RMS_EPS)
        for p in range(n_pairs):
            cols = slice(p * LANES, (p + 1) * LANES)
            o_ref[0, i * w:(i + 1) * w, cols] = (outs[(i, p)] * inv * gain_ref[:, cols]).astype(BF16)


def _na(na_x, na_m, t2, gain, rq):
    b, t, _ = na_x.shape
    n_rows = t // GRID_W
    nw = NA_WIDTH
    return pl.pallas_call(
        functools.partial(_na_body, rq=rq, n_rows=n_rows),
        grid=(b, n_rows // rq),
        in_specs=[
            pl.BlockSpec((1, rq * GRID_W, nw), lambda bi, j: (bi, j, 0)),
            pl.BlockSpec((1, t, nw), lambda bi, j: (bi, 0, 1)),
            pl.BlockSpec((1, t, nw), lambda bi, j: (bi, 0, 2)),
            pl.BlockSpec((N_META, nw), lambda bi, j: (0, 1)),
            pl.BlockSpec((N_META, nw), lambda bi, j: (0, 2)),
            _const_spec(t2.shape),
            _const_spec(gain.shape),
        ],
        out_specs=pl.BlockSpec((1, rq * GRID_W, nw), lambda bi, j: (bi, j, 0)),
        out_shape=jax.ShapeDtypeStruct((b, t, nw), BF16),
        compiler_params=pltpu.CompilerParams(
            dimension_semantics=("arbitrary", "arbitrary"), vmem_limit_bytes=VMEM_LIMIT),
        name="na",
    )(na_x, na_x, na_x, na_m, na_m, t2, gain)


def _na_meta_body(q_ref, k_ref, v_ref, gain_ref, o_ref):
    lane = lax.broadcasted_iota(jnp.int32, (N_META, NA_WIDTH), 1)
    q = q_ref[...] * (NA_HEAD_DIM ** -0.5)
    k = k_ref[...]
    v = v_ref[...]
    om = jnp.zeros((N_META, NA_WIDTH), F32)
    for h in range(NA_HEADS):
        in_head = (lane >= h * NA_HEAD_DIM) & (lane < (h + 1) * NA_HEAD_DIM)
        qh = jnp.where(in_head, q, jnp.zeros_like(q))
        s = lax.dot_general(qh, k, _CONTRACT_LAST, preferred_element_type=F32)
        m = jnp.max(s, axis=-1, keepdims=True)
        pw = jnp.exp(s - m)
        pw = pw / jnp.sum(pw, axis=-1, keepdims=True)
        oh = jnp.dot(pw.astype(BF16), v, preferred_element_type=F32)
        om = jnp.where(in_head, oh, om)
    o_ref[...] = _rms(om, gain_ref[...]).astype(BF16)


def _na_meta(na_m, gain):
    nw = NA_WIDTH
    return pl.pallas_call(
        _na_meta_body,
        grid=(1,),
        in_specs=[
            pl.BlockSpec((N_META, nw), lambda i: (0, 0)),
            pl.BlockSpec((N_META, nw), lambda i: (0, 1)),
            pl.BlockSpec((N_META, nw), lambda i: (0, 2)),
            _const_spec(gain.shape),
        ],
        out_specs=pl.BlockSpec((N_META, nw), lambda i: (0, 0)),
        out_shape=jax.ShapeDtypeStruct((N_META, nw), BF16),
        name="na_meta",
    )(na_m, na_m, na_m, gain)


_GQ, _GK, _GV, _GG, _GZ = 0, GLA_KW, 2 * GLA_KW, 2 * GLA_KW + GLA_VW, 2 * GLA_KW + 2 * GLA_VW


GLA_GROUP = 4


def _gla_body(x_ref, m_ref, up_ref, bias_ref, gain_ref, ox_ref, om_ref, c0, qd, oi, dst, decs, st):
    c = GLA_CHUNK
    kw = GLA_KW
    n_pairs = GLA_HEADS // 2
    t = x_ref.shape[1]
    n_chunks = t // c + 1
    c0[0:GLA_PAD, :] = jnp.zeros((GLA_PAD, GLA_COLS), BF16)
    c0[GLA_PAD:, :] = m_ref[...]

    ti = lax.broadcasted_iota(jnp.int32, (c, c), 0)
    si = lax.broadcasted_iota(jnp.int32, (c, c), 1)
    tri = jnp.where(si <= ti, 1.0, 0.0).astype(BF16)
    lo_c = lax.broadcasted_iota(jnp.int32, (c, LANES), 1) < GLA_DK
    lo_s = lax.broadcasted_iota(jnp.int32, (GLA_DV, 2 * LANES), 1) % LANES < GLA_DK
    t2 = lax.broadcasted_iota(jnp.int32, (2 * c, c), 0) % c
    s2 = lax.broadcasted_iota(jnp.int32, (2 * c, c), 1)
    keep_f = s2 <= t2
    up = up_ref[...]
    bias = bias_ref[...]
    gain = gain_ref[...]

    def reader(n):
        if n is None:
            return lambda a, b: c0[:, a:b]
        r0 = _aligned((n - 1) * c, c)
        return lambda a, b: x_ref[0, pl.ds(r0, c), a:b]

    def cidx(n):
        return 0 if n is None else n

    def prow(n):
        return pl.ds(_aligned(cidx(n) * c, c), c)

    def a0(n):
        return (jnp.dot(reader(n)(_GZ, _GZ + 2 * GLA_GATE_RANK), up, preferred_element_type=F32),)

    def a1(n, gate):
        gate = gate + bias
        la = (jnp.minimum(gate, 0.0) - jnp.log1p(jnp.exp(-jnp.abs(gate)))) * (1.0 / GLA_GATE_TAU)
        if n is None:
            la = jnp.where(lax.broadcasted_iota(jnp.int32, (c, 1), 0) >= GLA_PAD, la, 0.0)
        hi = la.astype(BF16)
        low = (la - hi.astype(F32)).astype(BF16)
        cs = jnp.dot(tri, jnp.concatenate([hi, low], axis=1), preferred_element_type=F32)
        return la, cs

    def a2(n, la, cs):
        rd = reader(n)
        pre = cs[:, :2 * kw] + cs[:, 2 * kw:]
        b_f = pre[:, :kw]
        bl_f = b_f[c - 1:c]
        bl_b = pre[c - 1:c, kw:]
        b_b = bl_b - pre[:, kw:] + la[:, kw:]
        q = rd(_GQ, _GQ + kw).astype(F32) * (GLA_DK ** -0.5)
        k = rd(_GK, _GK + kw).astype(F32)
        qd_f = (q * jnp.exp(b_f)).astype(BF16)
        qd_b = (q * jnp.exp(b_b)).astype(BF16)
        ki_f = (k * jnp.exp(-b_f)).astype(BF16)
        ki_b = (k * jnp.exp(-b_b)).astype(BF16)
        ke = jnp.concatenate([k * jnp.exp(bl_f - b_f), k * jnp.exp(bl_b - b_b)], axis=1).astype(BF16)
        dec = jnp.concatenate([jnp.exp(bl_f), jnp.exp(bl_b)], axis=1)
        qd[prow(n), :kw] = qd_f
        qd[prow(n), kw:] = qd_b
        drow = _aligned(cidx(n) * SUBLANES, SUBLANES)
        decs[pl.ds(drow, SUBLANES), :] = jnp.broadcast_to(dec, (SUBLANES, 2 * kw))
        araw = []
        for p in range(n_pairs):
            cols = slice(p * LANES, (p + 1) * LANES)
            a_f = lax.dot_general(_split_heads_rows(qd_f[:, cols], lo_c), ki_f[:, cols], _CONTRACT_LAST,
                                  preferred_element_type=F32)
            a_b = lax.dot_general(_split_heads_rows(qd_b[:, cols], lo_c), ki_b[:, cols], _CONTRACT_LAST,
                                  preferred_element_type=F32)
            araw.append((a_f, a_b))
        return araw, ke

    def a3(n, araw, ke):
        rd = reader(n)
        amat = [jnp.where(keep_f, a_f, a_b).astype(BF16) for a_f, a_b in araw]
        intra, incr = [], []
        for p in range(n_pairs):
            kcat = jnp.concatenate([ke[:, p * LANES:(p + 1) * LANES],
                                    ke[:, kw + p * LANES:kw + (p + 1) * LANES]], axis=1)
            for hh in range(2):
                h = 2 * p + hh
                vh = rd(_GV + h * GLA_DV, _GV + (h + 1) * GLA_DV)
                intra.append(jnp.dot(amat[p][hh * c:(hh + 1) * c], vh, preferred_element_type=F32))
                incr.append(lax.dot_general(vh, kcat, _CONTRACT_FIRST, preferred_element_type=F32))
        return intra, incr

    def a4(n, intra, incr):
        for h in range(GLA_HEADS):
            oi[prow(n), h * GLA_DV:(h + 1) * GLA_DV] = intra[h]
        for p in range(n_pairs):
            dst[cidx(n), p] = jnp.where(lo_s, incr[2 * p], incr[2 * p + 1])

    def c1(n):
        inter = []
        for p in range(n_pairs):
            qf = qd[prow(n), p * LANES:(p + 1) * LANES]
            qb = qd[prow(n), kw + p * LANES:kw + (p + 1) * LANES]
            q2 = jnp.concatenate([_split_heads_rows(qf, lo_c), _split_heads_rows(qb, lo_c)], axis=1)
            inter.append(lax.dot_general(q2, dst[cidx(n), p].astype(BF16), _CONTRACT_LAST,
                                         preferred_element_type=F32))
        return (inter,)

    def c2(n, inter):
        rd = reader(n)
        for h in range(GLA_HEADS):
            p, hh = divmod(h, 2)
            hc = slice(h * GLA_DV, (h + 1) * GLA_DV)
            o = oi[prow(n), hc] + inter[p][hh * c:(hh + 1) * c]
            g = rd(_GG + h * GLA_DV, _GG + (h + 1) * GLA_DV).astype(F32)
            res = (_rms(o, gain) * (g * jax.nn.sigmoid(g))).astype(BF16)
            if n is None:
                om_ref[0, :, hc] = res[GLA_PAD:]
            else:
                ox_ref[0, pl.ds(_aligned((n - 1) * c, c), c), hc] = res

    def run_group(stages, chunks):
        vals = [()] * len(chunks)
        for stage in stages:
            vals = [stage(n, *v) or () for n, v in zip(chunks, vals)]

    n_groups = (n_chunks - 1) // GLA_GROUP

    def group_chunks(i):
        return [1 + GLA_GROUP * i + u for u in range(GLA_GROUP)]

    stages_a = [a0, a1, a2, a3, a4]
    run_group(stages_a, [None])

    def pass_a(i, carry):
        run_group(stages_a, group_chunks(i))
        return carry

    lax.fori_loop(0, n_groups, pass_a, 0)

    st[...] = jnp.zeros(st.shape, F32)

    def scan(it, carry):
        for n, lanes, off in ((it, slice(0, LANES), 0), (n_chunks - 1 - it, slice(LANES, 2 * LANES), kw)):
            drow = _aligned(n * SUBLANES, SUBLANES)
            for p in range(n_pairs):
                inc = dst[n, p, :, lanes]
                s_in = st[p, :, lanes]
                dst[n, p, :, lanes] = s_in
                dec = decs[pl.ds(drow, 1), off + p * LANES:off + (p + 1) * LANES]
                st[p, :, lanes] = dec * s_in + inc
        return carry

    lax.fori_loop(0, n_chunks, scan, 0)

    run_group([c1, c2], [None])

    def pass_c(i, carry):
        run_group([c1, c2], group_chunks(i))
        return carry

    lax.fori_loop(0, n_groups, pass_c, 0)


def _gla(gl_x, gl_m, up, bias, gain):
    b, t, _ = gl_x.shape
    assert (t // GLA_CHUNK) % GLA_GROUP == 0
    n_chunks = t // GLA_CHUNK + 1
    lp = n_chunks * GLA_CHUNK
    n_pairs = GLA_HEADS // 2
    return pl.pallas_call(
        _gla_body,
        grid=(b,),
        in_specs=[
            pl.BlockSpec((1, t, GLA_COLS), lambda bi: (bi, 0, 0)),
            _const_spec(gl_m.shape),
            _const_spec(up.shape),
            _const_spec(bias.shape),
            _const_spec(gain.shape),
        ],
        out_specs=[
            pl.BlockSpec((1, t, GLA_VW), lambda bi: (bi, 0, 0)),
            pl.BlockSpec((1, N_META, GLA_VW), lambda bi: (bi, 0, 0)),
        ],
        out_shape=[
            jax.ShapeDtypeStruct((b, t, GLA_VW), BF16),
            jax.ShapeDtypeStruct((b, N_META, GLA_VW), BF16),
        ],
        scratch_shapes=[
            pltpu.VMEM((GLA_CHUNK, GLA_COLS), BF16),
            pltpu.VMEM((lp, 2 * GLA_KW), BF16),
            pltpu.VMEM((lp, GLA_VW), F32),
            pltpu.VMEM((n_chunks, n_pairs, GLA_DV, 2 * LANES), F32),
            pltpu.VMEM((n_chunks * SUBLANES, 2 * GLA_KW), F32),
            pltpu.VMEM((n_pairs, GLA_DV, 2 * LANES), F32),
        ],
        compiler_params=pltpu.CompilerParams(
            dimension_semantics=("arbitrary",), vmem_limit_bytes=VMEM_LIMIT),
        name="gla",
    )(gl_x, gl_m, up, bias, gain)


def _mix_body(x_ref, na_ref, gl_ref, wo_ref, g_ref, h_ref):
    mix = (jnp.dot(na_ref[...], wo_ref[:NA_WIDTH], preferred_element_type=F32)
           + jnp.dot(gl_ref[...], wo_ref[NA_WIDTH:], preferred_element_type=F32))
    h_ref[...] = x_ref[...] + _rms(mix, g_ref[...])


def _mix(x2, o_na, o_gl, wo, g, tm):
    rows, d = x2.shape
    return pl.pallas_call(
        _mix_body,
        grid=(rows // tm,),
        in_specs=[
            pl.BlockSpec((tm, d), lambda i: (i, 0)),
            pl.BlockSpec((tm, NA_WIDTH), lambda i: (i, 0)),
            pl.BlockSpec((tm, GLA_VW), lambda i: (i, 0)),
            _const_spec(wo.shape),
            _const_spec(g.shape),
        ],
        out_specs=pl.BlockSpec((tm, d), lambda i: (i, 0)),
        out_shape=jax.ShapeDtypeStruct((rows, d), F32),
        compiler_params=pltpu.CompilerParams(
            dimension_semantics=("arbitrary",), vmem_limit_bytes=VMEM_LIMIT),
        name="mix",
    )(x2, o_na, o_gl, wo, g)


def _ffn_body(xm_ref, xp_ref, xn_ref, nam_ref, nap_ref, nan_ref, glm_ref, glp_ref, gln_ref, hmeta_ref,
              wo_ref, g0_ref, g1_ref, win_ref, cw_ref, wout_ref, g2_ref,
              o_ref, y_ref, perm_ref, h_ref, *, tm, n_tiles):
    t = pl.program_id(1)
    sl = SUBLANES
    nv = tm // sl
    n_col = perm_ref.shape[0]
    hr = HALO_ROWS

    na = jnp.concatenate([nap_ref[0], nam_ref[0], nan_ref[0]], axis=0)
    gl = jnp.concatenate([glp_ref[0], glm_ref[0], gln_ref[0]], axis=0)
    xe = jnp.concatenate([xp_ref[0], xm_ref[0], xn_ref[0]], axis=0)
    mixed = (jnp.dot(na, wo_ref[:NA_WIDTH], preferred_element_type=F32)
             + jnp.dot(gl, wo_ref[NA_WIDTH:], preferred_element_type=F32))
    h1e = xe + _rms(mixed, g0_ref[...])
    h_ref[...] = h1e[hr:hr + tm]
    h_prev = h1e[hr - sl:hr]
    h_next = h1e[hr + tm:hr + tm + sl]

    def restride(x, to_permuted):
        for c in range(n_col):
            perm_ref[c] = x[:, c * LANES:(c + 1) * LANES]
        groups = []
        for k in range(nv):
            if to_permuted:
                start, stride = k, nv
            else:
                start, stride = sl * ((sl * k) % nv) + (sl * k) // nv, sl
            groups.append(jnp.concatenate(
                [perm_ref[c, pl.ds(start, sl, stride=stride), :] for c in range(n_col)], axis=1))
        return jnp.concatenate(groups, axis=0)

    g1 = g1_ref[...]
    n2_main = restride(_rms(h_ref[...], g1), True)
    prev = jnp.where(t == 0, hmeta_ref[0], h_prev)
    sub = lax.broadcasted_iota(jnp.int32, (sl, 1), 0)
    slab = jnp.where(sub == 0, pltpu.roll(prev, 1, 0),
                     jnp.where(sub == sl - 1, pltpu.roll(h_next, sl - 1, 0), 0.0))
    keep = jnp.logical_or(sub < sl - 1, t < n_tiles - 1)
    n2 = jnp.concatenate([n2_main, jnp.where(keep, _rms(slab, g1), 0.0)], axis=0).astype(BF16)

    nb = D_FF // FF_BLK
    sub_b = lax.broadcasted_iota(jnp.int32, (sl, FF_BLK), 0)

    def proj(cb):
        va = jnp.dot(n2, win_ref[:, cb * FF_BLK:(cb + 1) * FF_BLK], preferred_element_type=F32)
        ga = jnp.dot(n2, win_ref[:, D_FF + cb * FF_BLK:D_FF + (cb + 1) * FF_BLK],
                     preferred_element_type=F32)
        return va, ga

    def conv(a, taps):
        main = a[:tm]
        hal = a[tm:]
        first_prev = jnp.where(sub_b == 0, hal, pltpu.roll(main[tm - sl:], 1, 0))
        last_next = jnp.where(sub_b == sl - 1, hal, pltpu.roll(main[:sl], sl - 1, 0))
        a_prev = jnp.concatenate([first_prev, main[:tm - sl]], axis=0)
        a_next = jnp.concatenate([main[sl:], last_next], axis=0)
        return a_prev * taps[0:1] + main * taps[1:2] + a_next * taps[2:3] + taps[3:4]

    def act(cb, va, ga):
        val = conv(va, cw_ref[:, cb * FF_BLK:(cb + 1) * FF_BLK])
        gate = conv(ga, cw_ref[:, D_FF + cb * FF_BLK:D_FF + (cb + 1) * FF_BLK])
        y_ref[:, cb * FF_BLK:(cb + 1) * FF_BLK] = (jax.nn.gelu(gate, approximate=True) * val).astype(BF16)

    pending = proj(0)
    for cb in range(nb):
        nxt = proj(cb + 1) if cb + 1 < nb else None
        act(cb, *pending)
        pending = nxt
    r = _rms(jnp.dot(y_ref[...], wout_ref[...], preferred_element_type=F32), g2_ref[...])
    o_ref[0] = h_ref[...] + restride(r, False)


def _ffn(x, o_na, o_gl, h1m, wo, g0, g1, win, cw, wout, g2, tm):
    b, t, d = x.shape
    n_tiles = t // tm
    hb = tm // HALO_ROWS
    last = t // HALO_ROWS - 1
    single = pl.Buffered(1)

    def with_halo(width):
        return [
            pl.BlockSpec((1, tm, width), lambda bi, ti: (bi, ti, 0)),
            pl.BlockSpec((1, HALO_ROWS, width), lambda bi, ti: (bi, jnp.maximum(ti * hb - 1, 0), 0)),
            pl.BlockSpec((1, HALO_ROWS, width), lambda bi, ti: (bi, jnp.minimum((ti + 1) * hb, last), 0)),
        ]

    return pl.pallas_call(
        functools.partial(_ffn_body, tm=tm, n_tiles=n_tiles),
        grid=(b, n_tiles),
        in_specs=with_halo(d) + with_halo(NA_WIDTH) + with_halo(GLA_VW) + [
            pl.BlockSpec((1, SUBLANES, d), lambda bi, ti: (bi, N_META // SUBLANES - 1, 0)),
            pl.BlockSpec(wo.shape, lambda bi, ti: (0, 0), pipeline_mode=single),
            _const_spec(g0.shape),
            _const_spec(g1.shape),
            pl.BlockSpec(win.shape, lambda bi, ti: (0, 0), pipeline_mode=single),
            _const_spec(cw.shape),
            pl.BlockSpec(wout.shape, lambda bi, ti: (0, 0), pipeline_mode=single),
            _const_spec(g2.shape),
        ],
        out_specs=pl.BlockSpec((1, tm, d), lambda bi, ti: (bi, ti, 0)),
        out_shape=jax.ShapeDtypeStruct((b, t, d), F32),
        scratch_shapes=[pltpu.VMEM((tm, D_FF), BF16), pltpu.VMEM((d // LANES, tm, LANES), F32),
                        pltpu.VMEM((tm, d), F32)],
        compiler_params=pltpu.CompilerParams(
            dimension_semantics=("arbitrary", "arbitrary"), vmem_limit_bytes=VMEM_LIMIT),
        name="ffn",
    )(x, x, x, o_na, o_na, o_na, o_gl, o_gl, o_gl, h1m, wo, g0, g1, win, cw, wout, g2)


def _na_bias_body(base_ref, o_ref):
    w, kw = GRID_W, NA_WIN_COLS
    cq = lax.broadcasted_iota(jnp.int32, (w, 2 * w), 0)
    kk = lax.broadcasted_iota(jnp.int32, (w, 2 * w), 1) % w
    cs = jnp.clip(cq - kw // 2, 0, w - kw)
    in_win = (kk >= cs) & (kk < cs + kw)
    for e in range(o_ref.shape[1]):
        rows = jnp.broadcast_to(base_ref[0, e:e + 1, :], (w, 2 * w))
        shifted = pltpu.roll(rows, 2 * w - (kw - 1), 1, stride=1, stride_axis=0)
        o_ref[0, e] = jnp.where(in_win, shifted, MASK_NEG)


def _na_bias_table(rpb):
    h, nr, nc = rpb.shape
    w = GRID_W
    assert 2 * w == LANES and nc <= w
    padded = jnp.pad(rpb.astype(F32), ((0, 0), (0, 0), (0, w - nc)))
    base = jnp.concatenate([padded[:, :-1], padded[:, 1:]], axis=-1)
    return pl.pallas_call(
        _na_bias_body,
        grid=(h,),
        in_specs=[pl.BlockSpec((1, nr - 1, 2 * w), lambda i: (i, 0, 0))],
        out_specs=pl.BlockSpec((1, nr - 1, w, 2 * w), lambda i: (i, 0, 0, 0)),
        out_shape=jax.ShapeDtypeStruct((h, nr - 1, w, 2 * w), F32),
        name="na_bias",
    )(base)


def kernel(x, meta_tokens, norm_mix_pre, w_in, na_rel_bias, na_out_gain, gla_gate_up_fwd,
           gla_gate_bias_fwd, gla_gate_up_bwd, gla_gate_bias_bwd, gla_out_gain, w_o, norm_mix_post,
           norm_ffn_pre, w_ffn_in, ffn_conv_w, ffn_conv_b, w_ffn_out, norm_ffn_post):
    b, t, d = x.shape
    depth = w_in.shape[0]
    assert depth == 1, "meta rows are only carried as far as a single layer needs them"
    assert t % GRID_W == 0 and t // GRID_W >= NA_WIN_ROWS and N_META == 2 * SUBLANES
    l = 0
    row = lambda a: a[l].reshape(1, -1).astype(F32)

    wna = w_in[l, :, :NA_COLS].astype(BF16)
    wgl = w_in[l, :, NA_COLS:].astype(BF16)
    wo = w_o[l].astype(BF16)
    win = w_ffn_in[l].astype(BF16)
    wout = w_ffn_out[l].astype(BF16)
    cw = jnp.concatenate([ffn_conv_w[l], ffn_conv_b[l][None]], axis=0).astype(F32)
    t2 = _na_bias_table(na_rel_bias[l])
    zpad = jnp.zeros((GLA_GATE_RANK, GLA_KW), BF16)
    gate_up = jnp.concatenate(
        [jnp.concatenate([gla_gate_up_fwd[l].astype(BF16), zpad], axis=1),
         jnp.concatenate([zpad, gla_gate_up_bwd[l].astype(BF16)], axis=1)], axis=0)
    gate_bias = jnp.concatenate([row(gla_gate_bias_fwd), row(gla_gate_bias_bwd)], axis=1)

    x2 = x.reshape(b * t, d)
    g_pre = row(norm_mix_pre)
    na_x, gl_x = _inproj(x2, g_pre, wna, wgl, 512)
    na_m, gl_m = _inproj(meta_tokens.astype(F32), g_pre, wna, wgl, N_META)
    na_x = na_x.reshape(b, t, NA_COLS)
    gl_x = gl_x.reshape(b, t, GLA_COLS)

    na_gain = row(na_out_gain)
    o_na = _na(na_x, na_m, t2, na_gain, 4)
    o_na_m = _na_meta(na_m, na_gain)
    o_gl, o_gl_m = _gla(gl_x, gl_m, gate_up, gate_bias, row(gla_out_gain))

    g_post = row(norm_mix_post)
    h1m = _mix(jnp.broadcast_to(meta_tokens.astype(F32)[None], (b, N_META, d)).reshape(b * N_META, d),
               jnp.broadcast_to(o_na_m[None], (b, N_META, NA_WIDTH)).reshape(b * N_META, NA_WIDTH),
               o_gl_m.reshape(b * N_META, GLA_VW), wo, g_post, b * N_META)
    return _ffn(x, o_na, o_gl, h1m.reshape(b, N_META, d), wo, g_post, row(norm_ffn_pre), win, cw, wout,
                row(norm_ffn_post), 512)
```

```python
import functools

import jax
import jax.numpy as jnp
from jax import lax
from jax.experimental import pallas as pl
from jax.experimental.pallas import tpu as pltpu

F32 = jnp.float32
BF16 = jnp.bfloat16

N_META = 16
GRID_W = 64
NA_WIN_ROWS = 8
NA_WIN_COLS = 16
NA_HEADS = 8
NA_HEAD_DIM = 64
NA_WIDTH = NA_HEADS * NA_HEAD_DIM
GLA_HEADS = 4
GLA_DK = 64
GLA_DV = 128
GLA_KW = GLA_HEADS * GLA_DK
GLA_VW = GLA_HEADS * GLA_DV
GLA_GATE_RANK = 16
GLA_GATE_TAU = 16.0
GLA_CHUNK = 64
GLA_PAD = (-N_META) % GLA_CHUNK
NA_COLS = 3 * NA_WIDTH
GLA_COLS = 2 * GLA_KW + 2 * GLA_VW + 2 * GLA_GATE_RANK
D_FF = 2816
FF_BLK = 256
CONV_W = 3
RMS_EPS = 1e-6
MASK_NEG = -1e30
LOG2E = 1.4426950408889634
NA_Q_SCALE = NA_HEAD_DIM ** -0.5 * LOG2E
WEIGHT_CAST_STEPS = 8

LANES = 128
SUBLANES = 8
HALO_ROWS = 16
VMEM_LIMIT = 56 * 1024 * 1024

_CONTRACT_LAST = (((1,), (1,)), ((), ()))
_CONTRACT_FIRST = (((0,), (0,)), ((), ()))


def _rms(x, g):
    return x * lax.rsqrt(jnp.mean(x * x, axis=-1, keepdims=True) + RMS_EPS) * g


def _aligned(v, m):
    return v if isinstance(v, int) else pl.multiple_of(v, m)


def _const_spec(shape):
    nd = len(shape)
    return pl.BlockSpec(shape, lambda *_: (0,) * nd)


def _inproj_body(x_ref, g_ref, wna_ref, wgl_ref, na_ref, gl_ref, *, parts):
    g = g_ref[...]
    pm = x_ref.shape[0] // parts
    u = _rms(x_ref[0:pm], g).astype(BF16)
    for p in range(parts):
        rows = slice(p * pm, (p + 1) * pm)
        na_ref[rows] = jnp.dot(u, wna_ref[...], preferred_element_type=F32).astype(BF16)
        u_next = _rms(x_ref[(p + 1) * pm:(p + 2) * pm], g).astype(BF16) if p + 1 < parts else None
        gl_ref[rows] = jnp.dot(u, wgl_ref[...], preferred_element_type=F32).astype(BF16)
        u = u_next


def _inproj(x2, g, wna, wgl, tm, parts):
    rows, d = x2.shape
    single = pl.Buffered(1)
    return pl.pallas_call(
        functools.partial(_inproj_body, parts=parts),
        grid=(rows // tm,),
        in_specs=[
            pl.BlockSpec((tm, d), lambda i: (i, 0)),
            _const_spec(g.shape),
            pl.BlockSpec(wna.shape, lambda i: (0, 0), pipeline_mode=single),
            pl.BlockSpec(wgl.shape, lambda i: (0, 0), pipeline_mode=single),
        ],
        out_specs=[
            pl.BlockSpec((tm, NA_COLS), lambda i: (i, 0)),
            pl.BlockSpec((tm, GLA_COLS), lambda i: (i, 0)),
        ],
        out_shape=[
            jax.ShapeDtypeStruct((rows, NA_COLS), BF16),
            jax.ShapeDtypeStruct((rows, GLA_COLS), BF16),
        ],
        compiler_params=pltpu.CompilerParams(
            dimension_semantics=("arbitrary",), vmem_limit_bytes=VMEM_LIMIT),
        name="inproj",
    )(x2, g, wna, wgl)


def _split_heads_rows(pair, lo):
    zero = jnp.zeros_like(pair)
    return jnp.concatenate([jnp.where(lo, pair, zero), jnp.where(lo, zero, pair)], axis=0)


def _na_body(q_ref, k_ref, v_ref, km_ref, vm_ref, t2_ref, gain_ref, o_ref, *, rq, n_rows):
    j = pl.program_id(1)
    w = GRID_W
    kh = NA_WIN_ROWS
    lo = lax.broadcasted_iota(jnp.int32, (w, LANES), 1) < NA_HEAD_DIM
    n_pairs = NA_HEADS // 2
    units = [(i, p) for i in range(rq) for p in range(n_pairs)]

    def stage1(i, p):
        r = j * rq + i
        rs = jnp.clip(r - kh // 2, 0, n_rows - kh)
        e0 = rs - r + (NA_WIN_ROWS - 1)
        k0 = pl.multiple_of(rs * w, w)
        cols = slice(p * LANES, (p + 1) * LANES)
        qp = q_ref[0, i * w:(i + 1) * w, cols]
        q2 = _split_heads_rows(qp, lo)
        kw = k_ref[0, pl.ds(k0, kh * w), cols]
        s = lax.dot_general(q2, kw, _CONTRACT_LAST, preferred_element_type=F32)
        bias = jnp.concatenate(
            [jnp.concatenate([t2_ref[2 * p + hh, e0 + 2 * jj] for jj in range(kh // 2)], axis=1)
             for hh in range(2)], axis=0)
        s = s + bias
        sm = lax.dot_general(q2, km_ref[:, cols], _CONTRACT_LAST, preferred_element_type=F32)
        m = jnp.maximum(jnp.max(s, axis=-1, keepdims=True), jnp.max(sm, axis=-1, keepdims=True))
        return s, sm, m, k0

    def stage2(s, sm, m):
        pw = jnp.exp2(s - m)
        pm = jnp.exp2(sm - m)
        l = jnp.sum(pw, axis=-1, keepdims=True) + jnp.sum(pm, axis=-1, keepdims=True)
        return pw.astype(BF16), pm.astype(BF16), l

    def stage3(p, k0, pw, pm, l):
        cols = slice(p * LANES, (p + 1) * LANES)
        vw = v_ref[0, pl.ds(k0, kh * w), cols]
        o2 = (jnp.dot(pw, vw, preferred_element_type=F32)
              + jnp.dot(pm, vm_ref[:, cols], preferred_element_type=F32))
        o2 = o2 / l
        return jnp.where(lo, o2[:w], o2[w:])

    n = len(units)
    r1, r2, outs = {}, {}, {}
    for step in range(n + 2):
        if step < n:
            r1[step] = stage1(*units[step])
        if 0 <= step - 1 < n:
            s, sm, m, k0 = r1.pop(step - 1)
            r2[step - 1] = stage2(s, sm, m) + (k0,)
        if 0 <= step - 2 < n:
            pw, pm, l, k0 = r2.pop(step - 2)
            outs[units[step - 2]] = stage3(units[step - 2][1], k0, pw, pm, l)
    for i in range(rq):
        ssq = jnp.zeros((w, 1), F32)
        for p in range(n_pairs):
            ssq = ssq + jnp.sum(outs[(i, p)] * outs[(i, p)], axis=-1, keepdims=True)
        inv = lax.rsqrt(ssq * (1.0 / NA_WIDTH) + RMS_EPS)
        for p in range(n_pairs):
            cols = slice(p * LANES, (p + 1) * LANES)
            o_ref[0, i * w:(i + 1) * w, cols] = (outs[(i, p)] * inv * gain_ref[:, cols]).astype(BF16)


def _na(na_x, na_m, t2, gain, rq):
    b, t, _ = na_x.shape
    n_rows = t // GRID_W
    nw = NA_WIDTH
    return pl.pallas_call(
        functools.partial(_na_body, rq=rq, n_rows=n_rows),
        grid=(b, n_rows // rq),
        in_specs=[
            pl.BlockSpec((1, rq * GRID_W, nw), lambda bi, j: (bi, j, 0)),
            pl.BlockSpec((1, t, nw), lambda bi, j: (bi, 0, 1)),
            pl.BlockSpec((1, t, nw), lambda bi, j: (bi, 0, 2)),
            pl.BlockSpec((N_META, nw), lambda bi, j: (0, 1)),
            pl.BlockSpec((N_META, nw), lambda bi, j: (0, 2)),
            _const_spec(t2.shape),
            _const_spec(gain.shape),
        ],
        out_specs=pl.BlockSpec((1, rq * GRID_W, nw), lambda bi, j: (bi, j, 0)),
        out_shape=jax.ShapeDtypeStruct((b, t, nw), BF16),
        compiler_params=pltpu.CompilerParams(
            dimension_semantics=("arbitrary", "arbitrary"), vmem_limit_bytes=VMEM_LIMIT),
        name="na",
    )(na_x, na_x, na_x, na_m, na_m, t2, gain)


def _na_meta_body(q_ref, k_ref, v_ref, gain_ref, o_ref):
    lane = lax.broadcasted_iota(jnp.int32, (N_META, NA_WIDTH), 1)
    q = q_ref[...]
    k = k_ref[...]
    v = v_ref[...]
    om = jnp.zeros((N_META, NA_WIDTH), F32)
    for h in range(NA_HEADS):
        in_head = (lane >= h * NA_HEAD_DIM) & (lane < (h + 1) * NA_HEAD_DIM)
        qh = jnp.where(in_head, q, jnp.zeros_like(q))
        s = lax.dot_general(qh, k, _CONTRACT_LAST, preferred_element_type=F32)
        m = jnp.max(s, axis=-1, keepdims=True)
        pw = jnp.exp2(s - m)
        pw = pw / jnp.sum(pw, axis=-1, keepdims=True)
        oh = jnp.dot(pw.astype(BF16), v, preferred_element_type=F32)
        om = jnp.where(in_head, oh, om)
    o_ref[...] = _rms(om, gain_ref[...]).astype(BF16)


def _na_meta(na_m, gain):
    nw = NA_WIDTH
    return pl.pallas_call(
        _na_meta_body,
        grid=(1,),
        in_specs=[
            pl.BlockSpec((N_META, nw), lambda i: (0, 0)),
            pl.BlockSpec((N_META, nw), lambda i: (0, 1)),
            pl.BlockSpec((N_META, nw), lambda i: (0, 2)),
            _const_spec(gain.shape),
        ],
        out_specs=pl.BlockSpec((N_META, nw), lambda i: (0, 0)),
        out_shape=jax.ShapeDtypeStruct((N_META, nw), BF16),
        name="na_meta",
    )(na_m, na_m, na_m, gain)


_GQ, _GK, _GV, _GG, _GZ = 0, GLA_KW, 2 * GLA_KW, 2 * GLA_KW + GLA_VW, 2 * GLA_KW + 2 * GLA_VW


GLA_GROUP = 8


def _gla_body(x_ref, m_ref, up_ref, bias_ref, gain_ref, ox_ref, om_ref, c0, qd, oi, dst, decs, st):
    c = GLA_CHUNK
    kw = GLA_KW
    n_pairs = GLA_HEADS // 2
    t = x_ref.shape[1]
    n_chunks = t // c + 1
    c0[0:GLA_PAD, :] = jnp.zeros((GLA_PAD, GLA_COLS), BF16)
    c0[GLA_PAD:, :] = m_ref[...]

    ti = lax.broadcasted_iota(jnp.int32, (c, c), 0)
    si = lax.broadcasted_iota(jnp.int32, (c, c), 1)
    tri = jnp.where(si <= ti, 1.0, 0.0).astype(BF16)
    lo_c = lax.broadcasted_iota(jnp.int32, (c, LANES), 1) < GLA_DK
    lo_s = lax.broadcasted_iota(jnp.int32, (GLA_DV, 2 * LANES), 1) % LANES < GLA_DK
    t2 = lax.broadcasted_iota(jnp.int32, (2 * c, c), 0) % c
    s2 = lax.broadcasted_iota(jnp.int32, (2 * c, c), 1)
    keep_f = s2 <= t2
    up = up_ref[...]
    bias = bias_ref[...]
    gain = gain_ref[...]

    def reader(n):
        if n is None:
            return lambda a, b: c0[:, a:b]
        r0 = _aligned((n - 1) * c, c)
        return lambda a, b: x_ref[0, pl.ds(r0, c), a:b]

    def cidx(n):
        return 0 if n is None else n

    def prow(n):
        return pl.ds(_aligned(cidx(n) * c, c), c)

    def a0(n):
        return (jnp.dot(reader(n)(_GZ, _GZ + 2 * GLA_GATE_RANK), up, preferred_element_type=F32),)

    def a1(n, gate):
        gate = gate + bias
        la = (jnp.minimum(gate, 0.0) - jnp.log1p(jnp.exp(-jnp.abs(gate)))) * (1.0 / GLA_GATE_TAU)
        if n is None:
            la = jnp.where(lax.broadcasted_iota(jnp.int32, (c, 1), 0) >= GLA_PAD, la, 0.0)
        hi = la.astype(BF16)
        low = (la - hi.astype(F32)).astype(BF16)
        cs = jnp.dot(tri, jnp.concatenate([hi, low], axis=1), preferred_element_type=F32)
        return la, cs

    def a2(n, la, cs):
        rd = reader(n)
        pre = cs[:, :2 * kw] + cs[:, 2 * kw:]
        b_f = pre[:, :kw]
        bl_f = b_f[c - 1:c]
        bl_b = pre[c - 1:c, kw:]
        b_b = bl_b - pre[:, kw:] + la[:, kw:]
        q = rd(_GQ, _GQ + kw).astype(F32) * (GLA_DK ** -0.5)
        k = rd(_GK, _GK + kw).astype(F32)
        qd_f = (q * jnp.exp(b_f)).astype(BF16)
        qd_b = (q * jnp.exp(b_b)).astype(BF16)
        ki_f = (k * jnp.exp(-b_f)).astype(BF16)
        ki_b = (k * jnp.exp(-b_b)).astype(BF16)
        ke = jnp.concatenate([k * jnp.exp(bl_f - b_f), k * jnp.exp(bl_b - b_b)], axis=1).astype(BF16)
        dec = jnp.concatenate([jnp.exp(bl_f), jnp.exp(bl_b)], axis=1)
        qd[prow(n), :kw] = qd_f
        qd[prow(n), kw:] = qd_b
        drow = _aligned(cidx(n) * SUBLANES, SUBLANES)
        decs[pl.ds(drow, SUBLANES), :] = jnp.broadcast_to(dec, (SUBLANES, 2 * kw))
        araw = []
        for p in range(n_pairs):
            cols = slice(p * LANES, (p + 1) * LANES)
            a_f = lax.dot_general(_split_heads_rows(qd_f[:, cols], lo_c), ki_f[:, cols], _CONTRACT_LAST,
                                  preferred_element_type=F32)
            a_b = lax.dot_general(_split_heads_rows(qd_b[:, cols], lo_c), ki_b[:, cols], _CONTRACT_LAST,
                                  preferred_element_type=F32)
            araw.append((a_f, a_b))
        return araw, ke

    def a3(n, araw, ke):
        rd = reader(n)
        amat = [jnp.where(keep_f, a_f, a_b).astype(BF16) for a_f, a_b in araw]
        intra, incr = [], []
        for p in range(n_pairs):
            kcat = jnp.concatenate([ke[:, p * LANES:(p + 1) * LANES],
                                    ke[:, kw + p * LANES:kw + (p + 1) * LANES]], axis=1)
            for hh in range(2):
                h = 2 * p + hh
                vh = rd(_GV + h * GLA_DV, _GV + (h + 1) * GLA_DV)
                intra.append(jnp.dot(amat[p][hh * c:(hh + 1) * c], vh, preferred_element_type=F32))
                incr.append(lax.dot_general(vh, kcat, _CONTRACT_FIRST, preferred_element_type=F32))
        return intra, incr

    def a4(n, intra, incr):
        for h in range(GLA_HEADS):
            oi[prow(n), h * GLA_DV:(h + 1) * GLA_DV] = intra[h]
        for p in range(n_pairs):
            dst[cidx(n), p] = jnp.where(lo_s, incr[2 * p], incr[2 * p + 1])

    def c1(n):
        inter = []
        for p in range(n_pairs):
            qf = qd[prow(n), p * LANES:(p + 1) * LANES]
            qb = qd[prow(n), kw + p * LANES:kw + (p + 1) * LANES]
            q2 = jnp.concatenate([_split_heads_rows(qf, lo_c), _split_heads_rows(qb, lo_c)], axis=1)
            inter.append(lax.dot_general(q2, dst[cidx(n), p].astype(BF16), _CONTRACT_LAST,
                                         preferred_element_type=F32))
        return (inter,)

    def c2(n, inter):
        rd = reader(n)
        for h in range(GLA_HEADS):
            p, hh = divmod(h, 2)
            hc = slice(h * GLA_DV, (h + 1) * GLA_DV)
            o = oi[prow(n), hc] + inter[p][hh * c:(hh + 1) * c]
            g = rd(_GG + h * GLA_DV, _GG + (h + 1) * GLA_DV).astype(F32)
            res = (_rms(o, gain) * (g * jax.nn.sigmoid(g))).astype(BF16)
            if n is None:
                om_ref[0, :, hc] = res[GLA_PAD:]
            else:
                ox_ref[0, pl.ds(_aligned((n - 1) * c, c), c), hc] = res

    def run_group(stages, chunks):
        vals = [()] * len(chunks)
        for step in range(len(chunks) + len(stages) - 1):
            for k, stage in enumerate(stages):
                u = step - k
                if 0 <= u < len(chunks):
                    vals[u] = stage(chunks[u], *vals[u]) or ()

    n_groups = (n_chunks - 1) // GLA_GROUP

    def group_chunks(i):
        return [1 + GLA_GROUP * i + u for u in range(GLA_GROUP)]

    stages_a = [a0, a1, a2, a3, a4]
    run_group(stages_a, [None])

    def pass_a(i, carry):
        run_group(stages_a, group_chunks(i))
        return carry

    lax.fori_loop(0, n_groups, pass_a, 0)

    st[...] = jnp.zeros(st.shape, F32)

    def scan(it, carry):
        for n, lanes, off in ((it, slice(0, LANES), 0), (n_chunks - 1 - it, slice(LANES, 2 * LANES), kw)):
            drow = _aligned(n * SUBLANES, SUBLANES)
            for p in range(n_pairs):
                inc = dst[n, p, :, lanes]
                s_in = st[p, :, lanes]
                dst[n, p, :, lanes] = s_in
                dec = decs[pl.ds(drow, 1), off + p * LANES:off + (p + 1) * LANES]
                st[p, :, lanes] = dec * s_in + inc
        return carry

    lax.fori_loop(0, n_chunks, scan, 0)

    run_group([c1, c2], [None])

    def pass_c(i, carry):
        run_group([c1, c2], group_chunks(i))
        return carry

    lax.fori_loop(0, n_groups, pass_c, 0)


def _gla(gl_x, gl_m, up, bias, gain):
    b, t, _ = gl_x.shape
    assert (t // GLA_CHUNK) % GLA_GROUP == 0
    n_chunks = t // GLA_CHUNK + 1
    lp = n_chunks * GLA_CHUNK
    n_pairs = GLA_HEADS // 2
    return pl.pallas_call(
        _gla_body,
        grid=(b,),
        in_specs=[
            pl.BlockSpec((1, t, GLA_COLS), lambda bi: (bi, 0, 0)),
            _const_spec(gl_m.shape),
            _const_spec(up.shape),
            _const_spec(bias.shape),
            _const_spec(gain.shape),
        ],
        out_specs=[
            pl.BlockSpec((1, t, GLA_VW), lambda bi: (bi, 0, 0)),
            pl.BlockSpec((1, N_META, GLA_VW), lambda bi: (bi, 0, 0)),
        ],
        out_shape=[
            jax.ShapeDtypeStruct((b, t, GLA_VW), BF16),
            jax.ShapeDtypeStruct((b, N_META, GLA_VW), BF16),
        ],
        scratch_shapes=[
            pltpu.VMEM((GLA_CHUNK, GLA_COLS), BF16),
            pltpu.VMEM((lp, 2 * GLA_KW), BF16),
            pltpu.VMEM((lp, GLA_VW), F32),
            pltpu.VMEM((n_chunks, n_pairs, GLA_DV, 2 * LANES), F32),
            pltpu.VMEM((n_chunks * SUBLANES, 2 * GLA_KW), F32),
            pltpu.VMEM((n_pairs, GLA_DV, 2 * LANES), F32),
        ],
        compiler_params=pltpu.CompilerParams(
            dimension_semantics=("arbitrary",), vmem_limit_bytes=VMEM_LIMIT),
        name="gla",
    )(gl_x, gl_m, up, bias, gain)


def _mix_body(x_ref, na_ref, gl_ref, wo_ref, g_ref, h_ref):
    mix = (jnp.dot(na_ref[...], wo_ref[:NA_WIDTH], preferred_element_type=F32)
           + jnp.dot(gl_ref[...], wo_ref[NA_WIDTH:], preferred_element_type=F32))
    h_ref[...] = x_ref[...] + _rms(mix, g_ref[...])


def _mix(x2, o_na, o_gl, wo, g, tm):
    rows, d = x2.shape
    return pl.pallas_call(
        _mix_body,
        grid=(rows // tm,),
        in_specs=[
            pl.BlockSpec((tm, d), lambda i: (i, 0)),
            pl.BlockSpec((tm, NA_WIDTH), lambda i: (i, 0)),
            pl.BlockSpec((tm, GLA_VW), lambda i: (i, 0)),
            _const_spec(wo.shape),
            _const_spec(g.shape),
        ],
        out_specs=pl.BlockSpec((tm, d), lambda i: (i, 0)),
        out_shape=jax.ShapeDtypeStruct((rows, d), F32),
        compiler_params=pltpu.CompilerParams(
            dimension_semantics=("arbitrary",), vmem_limit_bytes=VMEM_LIMIT),
        name="mix",
    )(x2, o_na, o_gl, wo, g)


def _ffn_body(xm_ref, xp_ref, xn_ref, nam_ref, nap_ref, nan_ref, glm_ref, glp_ref, gln_ref, hmeta_ref,
              wo_ref, g0_ref, g1_ref, win_ref, cw_ref, wout_ref, g2_ref,
              o_ref, y_ref, perm_ref, h_ref, *, tm, n_tiles):
    t = pl.program_id(1)
    sl = SUBLANES
    nv = tm // sl
    n_col = perm_ref.shape[0]
    hr = HALO_ROWS

    na = jnp.concatenate([nap_ref[0], nam_ref[0], nan_ref[0]], axis=0)
    gl = jnp.concatenate([glp_ref[0], glm_ref[0], gln_ref[0]], axis=0)
    xe = jnp.concatenate([xp_ref[0], xm_ref[0], xn_ref[0]], axis=0)
    mixed = (jnp.dot(na, wo_ref[:NA_WIDTH], preferred_element_type=F32)
             + jnp.dot(gl, wo_ref[NA_WIDTH:], preferred_element_type=F32))
    h1e = xe + _rms(mixed, g0_ref[...])
    h_ref[...] = h1e[hr:hr + tm]
    h_prev = h1e[hr - sl:hr]
    h_next = h1e[hr + tm:hr + tm + sl]

    def restride(x, to_permuted):
        for c in range(n_col):
            perm_ref[c] = x[:, c * LANES:(c + 1) * LANES]
        groups = []
        for k in range(nv):
            if to_permuted:
                start, stride = k, nv
            else:
                start, stride = sl * ((sl * k) % nv) + (sl * k) // nv, sl
            groups.append(jnp.concatenate(
                [perm_ref[c, pl.ds(start, sl, stride=stride), :] for c in range(n_col)], axis=1))
        return jnp.concatenate(groups, axis=0)

    g1 = g1_ref[...]
    n2_main = restride(_rms(h_ref[...], g1), True)
    prev = jnp.where(t == 0, hmeta_ref[0], h_prev)
    sub = lax.broadcasted_iota(jnp.int32, (sl, 1), 0)
    slab = jnp.where(sub == 0, pltpu.roll(prev, 1, 0),
                     jnp.where(sub == sl - 1, pltpu.roll(h_next, sl - 1, 0), 0.0))
    keep = jnp.logical_or(sub < sl - 1, t < n_tiles - 1)
    n2 = jnp.concatenate([n2_main, jnp.where(keep, _rms(slab, g1), 0.0)], axis=0).astype(BF16)

    nb = D_FF // FF_BLK
    sub_b = lax.broadcasted_iota(jnp.int32, (sl, FF_BLK), 0)

    def proj(cb):
        va = jnp.dot(n2, win_ref[:, cb * FF_BLK:(cb + 1) * FF_BLK], preferred_element_type=F32)
        ga = jnp.dot(n2, win_ref[:, D_FF + cb * FF_BLK:D_FF + (cb + 1) * FF_BLK],
                     preferred_element_type=F32)
        return va, ga

    def conv(a, taps):
        main = a[:tm]
        hal = a[tm:]
        first_prev = jnp.where(sub_b == 0, hal, pltpu.roll(main[tm - sl:], 1, 0))
        last_next = jnp.where(sub_b == sl - 1, hal, pltpu.roll(main[:sl], sl - 1, 0))
        a_prev = jnp.concatenate([first_prev, main[:tm - sl]], axis=0)
        a_next = jnp.concatenate([main[sl:], last_next], axis=0)
        return a_prev * taps[0:1] + main * taps[1:2] + a_next * taps[2:3] + taps[3:4]

    def act(cb, va, ga):
        val = conv(va, cw_ref[:, cb * FF_BLK:(cb + 1) * FF_BLK])
        gate = conv(ga, cw_ref[:, D_FF + cb * FF_BLK:D_FF + (cb + 1) * FF_BLK])
        y_ref[:, cb * FF_BLK:(cb + 1) * FF_BLK] = (jax.nn.gelu(gate, approximate=True) * val).astype(BF16)

    pending = proj(0)
    for cb in range(nb):
        nxt = proj(cb + 1) if cb + 1 < nb else None
        act(cb, *pending)
        pending = nxt
    r = _rms(jnp.dot(y_ref[...], wout_ref[...], preferred_element_type=F32), g2_ref[...])
    o_ref[0] = h_ref[...] + restride(r, False)


def _ffn(x, o_na, o_gl, h1m, wo, g0, g1, win, cw, wout, g2, tm):
    b, t, d = x.shape
    n_tiles = t // tm
    hb = tm // HALO_ROWS
    last = t // HALO_ROWS - 1
    single = pl.Buffered(1)

    def with_halo(width):
        return [
            pl.BlockSpec((1, tm, width), lambda bi, ti: (bi, ti, 0)),
            pl.BlockSpec((1, HALO_ROWS, width), lambda bi, ti: (bi, jnp.maximum(ti * hb - 1, 0), 0)),
            pl.BlockSpec((1, HALO_ROWS, width), lambda bi, ti: (bi, jnp.minimum((ti + 1) * hb, last), 0)),
        ]

    return pl.pallas_call(
        functools.partial(_ffn_body, tm=tm, n_tiles=n_tiles),
        grid=(b, n_tiles),
        in_specs=with_halo(d) + with_halo(NA_WIDTH) + with_halo(GLA_VW) + [
            pl.BlockSpec((1, SUBLANES, d), lambda bi, ti: (bi, N_META // SUBLANES - 1, 0)),
            pl.BlockSpec(wo.shape, lambda bi, ti: (0, 0), pipeline_mode=single),
            _const_spec(g0.shape),
            _const_spec(g1.shape),
            pl.BlockSpec(win.shape, lambda bi, ti: (0, 0), pipeline_mode=single),
            _const_spec(cw.shape),
            pl.BlockSpec(wout.shape, lambda bi, ti: (0, 0), pipeline_mode=single),
            _const_spec(g2.shape),
        ],
        out_specs=pl.BlockSpec((1, tm, d), lambda bi, ti: (bi, ti, 0)),
        out_shape=jax.ShapeDtypeStruct((b, t, d), F32),
        scratch_shapes=[pltpu.VMEM((tm, D_FF), BF16), pltpu.VMEM((d // LANES, tm, LANES), F32),
                        pltpu.VMEM((tm, d), F32)],
        compiler_params=pltpu.CompilerParams(
            dimension_semantics=("arbitrary", "arbitrary"), vmem_limit_bytes=VMEM_LIMIT),
        name="ffn",
    )(x, x, x, o_na, o_na, o_na, o_gl, o_gl, o_gl, h1m, wo, g0, g1, win, cw, wout, g2)


def _cast_body(win_ref, wo_ref, wfi_ref, wfo_ref, qs_ref, wna_o, wgl_o, wo_o, wfi_o, wfo_o):
    w = win_ref[...]
    wna_o[...] = (w[:, :NA_COLS] * qs_ref[...]).astype(BF16)
    wgl_o[...] = w[:, NA_COLS:].astype(BF16)
    wo_o[...] = wo_ref[...].astype(BF16)
    wfi_o[...] = wfi_ref[...].astype(BF16)
    wfo_o[...] = wfo_ref[...].astype(BF16)


def _cast_weights(w_in, w_o, w_ffn_in, w_ffn_out):
    steps = WEIGHT_CAST_STEPS
    qscale = jnp.concatenate([jnp.full((1, NA_WIDTH), NA_Q_SCALE, F32),
                              jnp.ones((1, NA_COLS - NA_WIDTH), F32)], axis=1)

    def rows_of(w, cols=None):
        return pl.BlockSpec((w.shape[0] // steps, cols or w.shape[1]), lambda i: (i, 0))

    return pl.pallas_call(
        _cast_body,
        grid=(steps,),
        in_specs=[rows_of(w_in), rows_of(w_o), rows_of(w_ffn_in), rows_of(w_ffn_out), _const_spec(qscale.shape)],
        out_specs=[rows_of(w_in, NA_COLS), rows_of(w_in, GLA_COLS), rows_of(w_o), rows_of(w_ffn_in),
                   rows_of(w_ffn_out)],
        out_shape=[
            jax.ShapeDtypeStruct((w_in.shape[0], NA_COLS), BF16),
            jax.ShapeDtypeStruct((w_in.shape[0], GLA_COLS), BF16),
            jax.ShapeDtypeStruct(w_o.shape, BF16),
            jax.ShapeDtypeStruct(w_ffn_in.shape, BF16),
            jax.ShapeDtypeStruct(w_ffn_out.shape, BF16),
        ],
        name="cast_weights",
    )(w_in, w_o, w_ffn_in, w_ffn_out, qscale)


def _na_bias_body(base_ref, o_ref):
    w, kw = GRID_W, NA_WIN_COLS
    cq = lax.broadcasted_iota(jnp.int32, (w, 2 * w), 0)
    kk = lax.broadcasted_iota(jnp.int32, (w, 2 * w), 1) % w
    cs = jnp.clip(cq - kw // 2, 0, w - kw)
    in_win = (kk >= cs) & (kk < cs + kw)
    for e in range(o_ref.shape[1]):
        rows = jnp.broadcast_to(base_ref[0, e:e + 1, :], (w, 2 * w))
        shifted = pltpu.roll(rows, 2 * w - (kw - 1), 1, stride=1, stride_axis=0)
        o_ref[0, e] = jnp.where(in_win, shifted * LOG2E, MASK_NEG)


def _na_bias_table(rpb):
    h, nr, nc = rpb.shape
    w = GRID_W
    assert 2 * w == LANES and nc <= w
    padded = jnp.pad(rpb.astype(F32), ((0, 0), (0, 0), (0, w - nc)))
    base = jnp.concatenate([padded[:, :-1], padded[:, 1:]], axis=-1)
    return pl.pallas_call(
        _na_bias_body,
        grid=(h,),
        in_specs=[pl.BlockSpec((1, nr - 1, 2 * w), lambda i: (i, 0, 0))],
        out_specs=pl.BlockSpec((1, nr - 1, w, 2 * w), lambda i: (i, 0, 0, 0)),
        out_shape=jax.ShapeDtypeStruct((h, nr - 1, w, 2 * w), F32),
        name="na_bias",
    )(base)


def kernel(x, meta_tokens, norm_mix_pre, w_in, na_rel_bias, na_out_gain, gla_gate_up_fwd,
           gla_gate_bias_fwd, gla_gate_up_bwd, gla_gate_bias_bwd, gla_out_gain, w_o, norm_mix_post,
           norm_ffn_pre, w_ffn_in, ffn_conv_w, ffn_conv_b, w_ffn_out, norm_ffn_post):
    b, t, d = x.shape
    depth = w_in.shape[0]
    assert depth == 1, "meta rows are only carried as far as a single layer needs them"
    assert t % GRID_W == 0 and t // GRID_W >= NA_WIN_ROWS and N_META == 2 * SUBLANES
    l = 0
    row = lambda a: a[l].reshape(1, -1).astype(F32)

    wna, wgl, wo, win, wout = _cast_weights(w_in[l], w_o[l], w_ffn_in[l], w_ffn_out[l])
    cw =jnp.concatenate([ffn_conv_w[l], ffn_conv_b[l][None]], axis=0).astype(F32)
    t2 = _na_bias_table(na_rel_bias[l])
    zpad = jnp.zeros((GLA_GATE_RANK, GLA_KW), BF16)
    gate_up = jnp.concatenate(
        [jnp.concatenate([gla_gate_up_fwd[l].astype(BF16), zpad], axis=1),
         jnp.concatenate([zpad, gla_gate_up_bwd[l].astype(BF16)], axis=1)], axis=0)
    gate_bias = jnp.concatenate([row(gla_gate_bias_fwd), row(gla_gate_bias_bwd)], axis=1)

    x2 = x.reshape(b * t, d)
    g_pre = row(norm_mix_pre)
    na_x, gl_x = _inproj(x2, g_pre, wna, wgl, 1024, 2)
    na_m, gl_m = _inproj(meta_tokens.astype(F32), g_pre, wna, wgl, N_META, 1)
    na_x = na_x.reshape(b, t, NA_COLS)
    gl_x = gl_x.reshape(b, t, GLA_COLS)

    na_gain = row(na_out_gain)
    o_na = _na(na_x, na_m, t2, na_gain, 4)
    o_na_m = _na_meta(na_m, na_gain)
    o_gl, o_gl_m = _gla(gl_x, gl_m, gate_up, gate_bias, row(gla_out_gain))

    g_post = row(norm_mix_post)
    h1m = _mix(jnp.broadcast_to(meta_tokens.astype(F32)[None], (b, N_META, d)).reshape(b * N_META, d),
               jnp.broadcast_to(o_na_m[None], (b, N_META, NA_WIDTH)).reshape(b * N_META, NA_WIDTH),
               o_gl_m.reshape(b * N_META, GLA_VW), wo, g_post, b * N_META)
    return _ffn(x, o_na, o_gl, h1m.reshape(b, N_META, d), wo, g_post, row(norm_ffn_pre), win, cw, wout,
                row(norm_ffn_post), 512)
```

```python
import functools

import jax
import jax.numpy as jnp
from jax import lax
from jax.experimental import pallas as pl
from jax.experimental.pallas import tpu as pltpu

F32 = jnp.float32
BF16 = jnp.bfloat16

N_META = 16
GRID_W = 64
NA_WIN_ROWS = 8
NA_WIN_COLS = 16
NA_HEADS = 8
NA_HEAD_DIM = 64
NA_WIDTH = NA_HEADS * NA_HEAD_DIM
GLA_HEADS = 4
GLA_DK = 64
GLA_DV = 128
GLA_KW = GLA_HEADS * GLA_DK
GLA_VW = GLA_HEADS * GLA_DV
GLA_GATE_RANK = 16
GLA_GATE_TAU = 16.0
GLA_CHUNK = 64
GLA_PAD = (-N_META) % GLA_CHUNK
NA_COLS = 3 * NA_WIDTH
GLA_COLS = 2 * GLA_KW + 2 * GLA_VW + 2 * GLA_GATE_RANK
D_FF = 2816
FF_BLK = 256
PERM_PITCH_PAD = 8
CONV_W = 3
RMS_EPS = 1e-6
MASK_NEG = -1e30
LOG2E = 1.4426950408889634
NA_Q_SCALE = NA_HEAD_DIM ** -0.5 * LOG2E
WEIGHT_CAST_STEPS = 8

LANES = 128
SUBLANES = 8
HALO_ROWS = 16
VMEM_LIMIT = 56 * 1024 * 1024

_CONTRACT_LAST = (((1,), (1,)), ((), ()))
_CONTRACT_FIRST = (((0,), (0,)), ((), ()))


def _rms(x, g):
    return x * lax.rsqrt(jnp.mean(x * x, axis=-1, keepdims=True) + RMS_EPS) * g


def _aligned(v, m):
    return v if isinstance(v, int) else pl.multiple_of(v, m)


def _const_spec(shape):
    nd = len(shape)
    return pl.BlockSpec(shape, lambda *_: (0,) * nd)


def _inproj_body(x_ref, g_ref, wt_ref, na_ref, gl_ref, *, parts):
    g = g_ref[...]
    pm = x_ref.shape[0] // parts
    u = _rms(x_ref[0:pm], g).astype(BF16)
    for p in range(parts):
        rows = slice(p * pm, (p + 1) * pm)
        na_ref[rows] = lax.dot_general(u, wt_ref[:NA_COLS], _CONTRACT_LAST,
                                       preferred_element_type=F32).astype(BF16)
        u_next = _rms(x_ref[(p + 1) * pm:(p + 2) * pm], g).astype(BF16) if p + 1 < parts else None
        gl_ref[rows] = lax.dot_general(u, wt_ref[NA_COLS:], _CONTRACT_LAST,
                                       preferred_element_type=F32).astype(BF16)
        u = u_next


def _inproj(x2, g, w_in_t, tm, parts):
    rows, d = x2.shape
    single = pl.Buffered(1)
    return pl.pallas_call(
        functools.partial(_inproj_body, parts=parts),
        grid=(rows // tm,),
        in_specs=[
            pl.BlockSpec((tm, d), lambda i: (i, 0)),
            _const_spec(g.shape),
            pl.BlockSpec(w_in_t.shape, lambda i: (0, 0), pipeline_mode=single),
        ],
        out_specs=[
            pl.BlockSpec((tm, NA_COLS), lambda i: (i, 0)),
            pl.BlockSpec((tm, GLA_COLS), lambda i: (i, 0)),
        ],
        out_shape=[
            jax.ShapeDtypeStruct((rows, NA_COLS), BF16),
            jax.ShapeDtypeStruct((rows, GLA_COLS), BF16),
        ],
        compiler_params=pltpu.CompilerParams(
            dimension_semantics=("arbitrary",), vmem_limit_bytes=VMEM_LIMIT),
        name="inproj",
    )(x2, g, w_in_t)


def _split_heads_rows(pair, lo):
    zero = jnp.zeros_like(pair)
    return jnp.concatenate([jnp.where(lo, pair, zero), jnp.where(lo, zero, pair)], axis=0)


def _na_body(q_ref, k_ref, v_ref, km_ref, vm_ref, t2_ref, gain_ref, o_ref, *, rq, n_rows):
    j = pl.program_id(1)
    w = GRID_W
    kh = NA_WIN_ROWS
    lo = lax.broadcasted_iota(jnp.int32, (w, LANES), 1) < NA_HEAD_DIM
    n_pairs = NA_HEADS // 2
    units = [(i, p) for i in range(rq) for p in range(n_pairs)]

    def stage1(i, p):
        r = j * rq + i
        rs = jnp.clip(r - kh // 2, 0, n_rows - kh)
        e0 = rs - r + (NA_WIN_ROWS - 1)
        k0 = pl.multiple_of(rs * w, w)
        cols = slice(p * LANES, (p + 1) * LANES)
        qp = q_ref[0, i * w:(i + 1) * w, cols]
        q2 = _split_heads_rows(qp, lo)
        kw = k_ref[0, pl.ds(k0, kh * w), cols]
        s = lax.dot_general(q2, kw, _CONTRACT_LAST, preferred_element_type=F32)
        bias = jnp.concatenate(
            [jnp.concatenate([t2_ref[2 * p + hh, e0 + 2 * jj] for jj in range(kh // 2)], axis=1)
             for hh in range(2)], axis=0)
        s = s + bias
        sm = lax.dot_general(q2, km_ref[:, cols], _CONTRACT_LAST, preferred_element_type=F32)
        m = jnp.maximum(jnp.max(s, axis=-1, keepdims=True), jnp.max(sm, axis=-1, keepdims=True))
        return s, sm, m, k0

    def stage2(s, sm, m):
        pw = jnp.exp2(s - m)
        pm = jnp.exp2(sm - m)
        l = jnp.sum(pw, axis=-1, keepdims=True) + jnp.sum(pm, axis=-1, keepdims=True)
        return pw.astype(BF16), pm.astype(BF16), l

    def stage3(p, k0, pw, pm, l):
        cols = slice(p * LANES, (p + 1) * LANES)
        vw = v_ref[0, pl.ds(k0, kh * w), cols]
        o2 = (jnp.dot(pw, vw, preferred_element_type=F32)
              + jnp.dot(pm, vm_ref[:, cols], preferred_element_type=F32))
        o2 = o2 / l
        return jnp.where(lo, o2[:w], o2[w:])

    n = len(units)
    r1, r2, outs = {}, {}, {}
    for step in range(n + 2):
        if step < n:
            r1[step] = stage1(*units[step])
        if 0 <= step - 1 < n:
            s, sm, m, k0 = r1.pop(step - 1)
            r2[step - 1] = stage2(s, sm, m) + (k0,)
        if 0 <= step - 2 < n:
            pw, pm, l, k0 = r2.pop(step - 2)
            outs[units[step - 2]] = stage3(units[step - 2][1], k0, pw, pm, l)
    for i in range(rq):
        ssq = jnp.zeros((w, 1), F32)
        for p in range(n_pairs):
            ssq = ssq + jnp.sum(outs[(i, p)] * outs[(i, p)], axis=-1, keepdims=True)
        inv = lax.rsqrt(ssq * (1.0 / NA_WIDTH) + RMS_EPS)
        for p in range(n_pairs):
            cols = slice(p * LANES, (p + 1) * LANES)
            o_ref[0, i * w:(i + 1) * w, cols] = (outs[(i, p)] * inv * gain_ref[:, cols]).astype(BF16)


def _na(na_x, na_m, t2, gain, rq):
    b, t, _ = na_x.shape
    n_rows = t // GRID_W
    nw = NA_WIDTH
    return pl.pallas_call(
        functools.partial(_na_body, rq=rq, n_rows=n_rows),
        grid=(b, n_rows // rq),
        in_specs=[
            pl.BlockSpec((1, rq * GRID_W, nw), lambda bi, j: (bi, j, 0)),
            pl.BlockSpec((1, t, nw), lambda bi, j: (bi, 0, 1)),
            pl.BlockSpec((1, t, nw), lambda bi, j: (bi, 0, 2)),
            pl.BlockSpec((N_META, nw), lambda bi, j: (0, 1)),
            pl.BlockSpec((N_META, nw), lambda bi, j: (0, 2)),
            _const_spec(t2.shape),
            _const_spec(gain.shape),
        ],
        out_specs=pl.BlockSpec((1, rq * GRID_W, nw), lambda bi, j: (bi, j, 0)),
        out_shape=jax.ShapeDtypeStruct((b, t, nw), BF16),
        compiler_params=pltpu.CompilerParams(
            dimension_semantics=("arbitrary", "arbitrary"), vmem_limit_bytes=VMEM_LIMIT),
        name="na",
    )(na_x, na_x, na_x, na_m, na_m, t2, gain)


def _na_meta_body(q_ref, k_ref, v_ref, gain_ref, o_ref):
    lane = lax.broadcasted_iota(jnp.int32, (N_META, NA_WIDTH), 1)
    q = q_ref[...]
    k = k_ref[...]
    v = v_ref[...]
    om = jnp.zeros((N_META, NA_WIDTH), F32)
    for h in range(NA_HEADS):
        in_head = (lane >= h * NA_HEAD_DIM) & (lane < (h + 1) * NA_HEAD_DIM)
        qh = jnp.where(in_head, q, jnp.zeros_like(q))
        s = lax.dot_general(qh, k, _CONTRACT_LAST, preferred_element_type=F32)
        m = jnp.max(s, axis=-1, keepdims=True)
        pw = jnp.exp2(s - m)
        pw = pw / jnp.sum(pw, axis=-1, keepdims=True)
        oh = jnp.dot(pw.astype(BF16), v, preferred_element_type=F32)
        om = jnp.where(in_head, oh, om)
    o_ref[...] = _rms(om, gain_ref[...]).astype(BF16)


def _na_meta(na_m, gain):
    nw = NA_WIDTH
    return pl.pallas_call(
        _na_meta_body,
        grid=(1,),
        in_specs=[
            pl.BlockSpec((N_META, nw), lambda i: (0, 0)),
            pl.BlockSpec((N_META, nw), lambda i: (0, 1)),
            pl.BlockSpec((N_META, nw), lambda i: (0, 2)),
            _const_spec(gain.shape),
        ],
        out_specs=pl.BlockSpec((N_META, nw), lambda i: (0, 0)),
        out_shape=jax.ShapeDtypeStruct((N_META, nw), BF16),
        name="na_meta",
    )(na_m, na_m, na_m, gain)


_GQ, _GK, _GV, _GG, _GZ = 0, GLA_KW, 2 * GLA_KW, 2 * GLA_KW + GLA_VW, 2 * GLA_KW + 2 * GLA_VW


GLA_GROUP = 8


def _gla_body(x_ref, m_ref, up_ref, bias_ref, gain_ref, ox_ref, om_ref, c0, qd, oi, dst, decs, st):
    c = GLA_CHUNK
    kw = GLA_KW
    n_pairs = GLA_HEADS // 2
    t = x_ref.shape[1]
    n_chunks = t // c + 1
    c0[0:GLA_PAD, :] = jnp.zeros((GLA_PAD, GLA_COLS), BF16)
    c0[GLA_PAD:, :] = m_ref[...]

    ti = lax.broadcasted_iota(jnp.int32, (c, c), 0)
    si = lax.broadcasted_iota(jnp.int32, (c, c), 1)
    tri = jnp.where(si <= ti, 1.0, 0.0).astype(BF16)
    lo_c = lax.broadcasted_iota(jnp.int32, (c, LANES), 1) < GLA_DK
    lo_s = lax.broadcasted_iota(jnp.int32, (GLA_DV, 2 * LANES), 1) % LANES < GLA_DK
    t2 = lax.broadcasted_iota(jnp.int32, (2 * c, c), 0) % c
    s2 = lax.broadcasted_iota(jnp.int32, (2 * c, c), 1)
    keep_f = s2 <= t2
    up = up_ref[...]
    bias = bias_ref[...]
    gain = gain_ref[...]

    def reader(n):
        if n is None:
            return lambda a, b: c0[:, a:b]
        r0 = _aligned((n - 1) * c, c)
        return lambda a, b: x_ref[0, pl.ds(r0, c), a:b]

    def cidx(n):
        return 0 if n is None else n

    def prow(n):
        return pl.ds(_aligned(cidx(n) * c, c), c)

    def a0(n):
        return (jnp.dot(reader(n)(_GZ, _GZ + 2 * GLA_GATE_RANK), up, preferred_element_type=F32),)

    def a1(n, gate):
        gate = gate + bias
        la = (jnp.minimum(gate, 0.0) - jnp.log1p(jnp.exp(-jnp.abs(gate)))) * (1.0 / GLA_GATE_TAU)
        if n is None:
            la = jnp.where(lax.broadcasted_iota(jnp.int32, (c, 1), 0) >= GLA_PAD, la, 0.0)
        hi = la.astype(BF16)
        low = (la - hi.astype(F32)).astype(BF16)
        cs = jnp.dot(tri, jnp.concatenate([hi, low], axis=1), preferred_element_type=F32)
        return la, cs

    def a2(n, la, cs):
        rd = reader(n)
        pre = cs[:, :2 * kw] + cs[:, 2 * kw:]
        b_f = pre[:, :kw]
        bl_f = b_f[c - 1:c]
        bl_b = pre[c - 1:c, kw:]
        b_b = bl_b - pre[:, kw:] + la[:, kw:]
        q = rd(_GQ, _GQ + kw).astype(F32) * (GLA_DK ** -0.5)
        k = rd(_GK, _GK + kw).astype(F32)
        qd_f = (q * jnp.exp(b_f)).astype(BF16)
        qd_b = (q * jnp.exp(b_b)).astype(BF16)
        ki_f = (k * jnp.exp(-b_f)).astype(BF16)
        ki_b = (k * jnp.exp(-b_b)).astype(BF16)
        ke = jnp.concatenate([k * jnp.exp(bl_f - b_f), k * jnp.exp(bl_b - b_b)], axis=1).astype(BF16)
        dec = jnp.concatenate([jnp.exp(bl_f), jnp.exp(bl_b)], axis=1)
        qd[prow(n), :kw] = qd_f
        qd[prow(n), kw:] = qd_b
        drow = _aligned(cidx(n) * SUBLANES, SUBLANES)
        decs[pl.ds(drow, SUBLANES), :] = jnp.broadcast_to(dec, (SUBLANES, 2 * kw))
        araw = []
        for p in range(n_pairs):
            cols = slice(p * LANES, (p + 1) * LANES)
            a_f = lax.dot_general(_split_heads_rows(qd_f[:, cols], lo_c), ki_f[:, cols], _CONTRACT_LAST,
                                  preferred_element_type=F32)
            a_b = lax.dot_general(_split_heads_rows(qd_b[:, cols], lo_c), ki_b[:, cols], _CONTRACT_LAST,
                                  preferred_element_type=F32)
            araw.append((a_f, a_b))
        return araw, ke

    def a3(n, araw, ke):
        rd = reader(n)
        amat = [jnp.where(keep_f, a_f, a_b).astype(BF16) for a_f, a_b in araw]
        intra, incr = [], []
        for p in range(n_pairs):
            kcat = jnp.concatenate([ke[:, p * LANES:(p + 1) * LANES],
                                    ke[:, kw + p * LANES:kw + (p + 1) * LANES]], axis=1)
            for hh in range(2):
                h = 2 * p + hh
                vh = rd(_GV + h * GLA_DV, _GV + (h + 1) * GLA_DV)
                intra.append(jnp.dot(amat[p][hh * c:(hh + 1) * c], vh, preferred_element_type=F32))
                incr.append(lax.dot_general(vh, kcat, _CONTRACT_FIRST, preferred_element_type=F32))
        return intra, incr

    def a4(n, intra, incr):
        for h in range(GLA_HEADS):
            oi[prow(n), h * GLA_DV:(h + 1) * GLA_DV] = intra[h]
        for p in range(n_pairs):
            dst[cidx(n), p] = jnp.where(lo_s, incr[2 * p], incr[2 * p + 1])

    def c1(n):
        inter = []
        for p in range(n_pairs):
            qf = qd[prow(n), p * LANES:(p + 1) * LANES]
            qb = qd[prow(n), kw + p * LANES:kw + (p + 1) * LANES]
            q2 = jnp.concatenate([_split_heads_rows(qf, lo_c), _split_heads_rows(qb, lo_c)], axis=1)
            inter.append(lax.dot_general(q2, dst[cidx(n), p].astype(BF16), _CONTRACT_LAST,
                                         preferred_element_type=F32))
        return (inter,)

    def c2(n, inter):
        rd = reader(n)
        for h in range(GLA_HEADS):
            p, hh = divmod(h, 2)
            hc = slice(h * GLA_DV, (h + 1) * GLA_DV)
            o = oi[prow(n), hc] + inter[p][hh * c:(hh + 1) * c]
            g = rd(_GG + h * GLA_DV, _GG + (h + 1) * GLA_DV).astype(F32)
            res = (_rms(o, gain) * (g * jax.nn.sigmoid(g))).astype(BF16)
            if n is None:
                om_ref[0, :, hc] = res[GLA_PAD:]
            else:
                ox_ref[0, pl.ds(_aligned((n - 1) * c, c), c), hc] = res

    def run_group(stages, chunks):
        vals = [()] * len(chunks)
        for step in range(len(chunks) + len(stages) - 1):
            for k, stage in enumerate(stages):
                u = step - k
                if 0 <= u < len(chunks):
                    vals[u] = stage(chunks[u], *vals[u]) or ()

    n_groups = (n_chunks - 1) // GLA_GROUP

    def group_chunks(i):
        return [1 + GLA_GROUP * i + u for u in range(GLA_GROUP)]

    stages_a = [a0, a1, a2, a3, a4]
    run_group(stages_a, [None])

    def pass_a(i, carry):
        run_group(stages_a, group_chunks(i))
        return carry

    lax.fori_loop(0, n_groups, pass_a, 0)

    st[...] = jnp.zeros(st.shape, F32)

    def scan(it, carry):
        for n, lanes, off in ((it, slice(0, LANES), 0), (n_chunks - 1 - it, slice(LANES, 2 * LANES), kw)):
            drow = _aligned(n * SUBLANES, SUBLANES)
            for p in range(n_pairs):
                inc = dst[n, p, :, lanes]
                s_in = st[p, :, lanes]
                dst[n, p, :, lanes] = s_in
                dec = decs[pl.ds(drow, 1), off + p * LANES:off + (p + 1) * LANES]
                st[p, :, lanes] = dec * s_in + inc
        return carry

    lax.fori_loop(0, n_chunks, scan, 0)

    run_group([c1, c2], [None])

    def pass_c(i, carry):
        run_group([c1, c2], group_chunks(i))
        return carry

    lax.fori_loop(0, n_groups, pass_c, 0)


def _gla(gl_x, gl_m, up, bias, gain):
    b, t, _ = gl_x.shape
    assert (t // GLA_CHUNK) % GLA_GROUP == 0
    n_chunks = t // GLA_CHUNK + 1
    lp = n_chunks * GLA_CHUNK
    n_pairs = GLA_HEADS // 2
    return pl.pallas_call(
        _gla_body,
        grid=(b,),
        in_specs=[
            pl.BlockSpec((1, t, GLA_COLS), lambda bi: (bi, 0, 0)),
            _const_spec(gl_m.shape),
            _const_spec(up.shape),
            _const_spec(bias.shape),
            _const_spec(gain.shape),
        ],
        out_specs=[
            pl.BlockSpec((1, t, GLA_VW), lambda bi: (bi, 0, 0)),
            pl.BlockSpec((1, N_META, GLA_VW), lambda bi: (bi, 0, 0)),
        ],
        out_shape=[
            jax.ShapeDtypeStruct((b, t, GLA_VW), BF16),
            jax.ShapeDtypeStruct((b, N_META, GLA_VW), BF16),
        ],
        scratch_shapes=[
            pltpu.VMEM((GLA_CHUNK, GLA_COLS), BF16),
            pltpu.VMEM((lp, 2 * GLA_KW), BF16),
            pltpu.VMEM((lp, GLA_VW), F32),
            pltpu.VMEM((n_chunks, n_pairs, GLA_DV, 2 * LANES), F32),
            pltpu.VMEM((n_chunks * SUBLANES, 2 * GLA_KW), F32),
            pltpu.VMEM((n_pairs, GLA_DV, 2 * LANES), F32),
        ],
        compiler_params=pltpu.CompilerParams(
            dimension_semantics=("arbitrary",), vmem_limit_bytes=VMEM_LIMIT),
        name="gla",
    )(gl_x, gl_m, up, bias, gain)


def _mix_body(x_ref, na_ref, gl_ref, wo_ref, g_ref, h_ref):
    mix = (jnp.dot(na_ref[...], wo_ref[:NA_WIDTH], preferred_element_type=F32)
           + jnp.dot(gl_ref[...], wo_ref[NA_WIDTH:], preferred_element_type=F32))
    h_ref[...] = x_ref[...] + _rms(mix, g_ref[...])


def _mix(x2, o_na, o_gl, wo, g, tm):
    rows, d = x2.shape
    return pl.pallas_call(
        _mix_body,
        grid=(rows // tm,),
        in_specs=[
            pl.BlockSpec((tm, d), lambda i: (i, 0)),
            pl.BlockSpec((tm, NA_WIDTH), lambda i: (i, 0)),
            pl.BlockSpec((tm, GLA_VW), lambda i: (i, 0)),
            _const_spec(wo.shape),
            _const_spec(g.shape),
        ],
        out_specs=pl.BlockSpec((tm, d), lambda i: (i, 0)),
        out_shape=jax.ShapeDtypeStruct((rows, d), F32),
        compiler_params=pltpu.CompilerParams(
            dimension_semantics=("arbitrary",), vmem_limit_bytes=VMEM_LIMIT),
        name="mix",
    )(x2, o_na, o_gl, wo, g)


def _ffn_body(xm_ref, xp_ref, xn_ref, nam_ref, nap_ref, nan_ref, glm_ref, glp_ref, gln_ref, hmeta_ref,
              wo_ref, g0_ref, g1_ref, win_ref, cw_ref, wout_ref, g2_ref,
              o_ref, y_ref, perm_ref, h_ref, *, tm, n_tiles):
    t = pl.program_id(1)
    sl = SUBLANES
    nv = tm // sl
    n_col = perm_ref.shape[0]
    hr = HALO_ROWS

    na = jnp.concatenate([nap_ref[0], nam_ref[0], nan_ref[0]], axis=0)
    gl = jnp.concatenate([glp_ref[0], glm_ref[0], gln_ref[0]], axis=0)
    xe = jnp.concatenate([xp_ref[0], xm_ref[0], xn_ref[0]], axis=0)
    mixed = (jnp.dot(na, wo_ref[:NA_WIDTH], preferred_element_type=F32)
             + jnp.dot(gl, wo_ref[NA_WIDTH:], preferred_element_type=F32))
    h1e = xe + _rms(mixed, g0_ref[...])
    h_ref[...] = h1e[hr:hr + tm]
    h_prev = h1e[hr - sl:hr]
    h_next = h1e[hr + tm:hr + tm + sl]

    pitch = perm_ref.shape[1] // sl

    def restride(x, to_permuted):
        groups = []
        if to_permuted:
            for c in range(n_col):
                for s in range(sl):
                    perm_ref[c, s * pitch:s * pitch + nv] = x[s * nv:(s + 1) * nv, c * LANES:(c + 1) * LANES]
        else:
            for c in range(n_col):
                perm_ref[c, 0:tm] = x[:, c * LANES:(c + 1) * LANES]
        for k in range(nv):
            if to_permuted:
                start, stride = k, pitch
            else:
                start, stride = sl * ((sl * k) % nv) + (sl * k) // nv, sl
            groups.append(jnp.concatenate(
                [perm_ref[c, pl.ds(start, sl, stride=stride), :] for c in range(n_col)], axis=1))
        return jnp.concatenate(groups, axis=0)

    g1 = g1_ref[...]
    n2_main = restride(_rms(h_ref[...], g1), True)
    prev = jnp.where(t == 0, hmeta_ref[0], h_prev)
    sub = lax.broadcasted_iota(jnp.int32, (sl, 1), 0)
    slab = jnp.where(sub == 0, pltpu.roll(prev, 1, 0),
                     jnp.where(sub == sl - 1, pltpu.roll(h_next, sl - 1, 0), 0.0))
    keep = jnp.logical_or(sub < sl - 1, t < n_tiles - 1)
    n2 = jnp.concatenate([n2_main, jnp.where(keep, _rms(slab, g1), 0.0)], axis=0).astype(BF16)

    nb = D_FF // FF_BLK
    sub_b = lax.broadcasted_iota(jnp.int32, (sl, FF_BLK), 0)

    def proj(cb):
        va = jnp.dot(n2, win_ref[:, cb * FF_BLK:(cb + 1) * FF_BLK], preferred_element_type=F32)
        ga = jnp.dot(n2, win_ref[:, D_FF + cb * FF_BLK:D_FF + (cb + 1) * FF_BLK],
                     preferred_element_type=F32)
        return va, ga

    def conv(a, taps):
        main = a[:tm]
        hal = a[tm:]
        first_prev = jnp.where(sub_b == 0, hal, pltpu.roll(main[tm - sl:], 1, 0))
        last_next = jnp.where(sub_b == sl - 1, hal, pltpu.roll(main[:sl], sl - 1, 0))
        a_prev = jnp.concatenate([first_prev, main[:tm - sl]], axis=0)
        a_next = jnp.concatenate([main[sl:], last_next], axis=0)
        return a_prev * taps[0:1] + main * taps[1:2] + a_next * taps[2:3] + taps[3:4]

    def act(cb, va, ga):
        val = conv(va, cw_ref[:, cb * FF_BLK:(cb + 1) * FF_BLK])
        gate = conv(ga, cw_ref[:, D_FF + cb * FF_BLK:D_FF + (cb + 1) * FF_BLK])
        y_ref[:, cb * FF_BLK:(cb + 1) * FF_BLK] = (jax.nn.gelu(gate, approximate=True) * val).astype(BF16)

    pending = proj(0)
    for cb in range(nb):
        nxt = proj(cb + 1) if cb + 1 < nb else None
        act(cb, *pending)
        pending = nxt
    r = _rms(jnp.dot(y_ref[...], wout_ref[...], preferred_element_type=F32), g2_ref[...])
    o_ref[0] = h_ref[...] + restride(r, False)


def _ffn(x, o_na, o_gl, h1m, wo, g0, g1, win, cw, wout, g2, tm):
    b, t, d = x.shape
    n_tiles = t // tm
    hb = tm // HALO_ROWS
    last = t // HALO_ROWS - 1
    single = pl.Buffered(1)

    def with_halo(width):
        return [
            pl.BlockSpec((1, tm, width), lambda bi, ti: (bi, ti, 0)),
            pl.BlockSpec((1, HALO_ROWS, width), lambda bi, ti: (bi, jnp.maximum(ti * hb - 1, 0), 0)),
            pl.BlockSpec((1, HALO_ROWS, width), lambda bi, ti: (bi, jnp.minimum((ti + 1) * hb, last), 0)),
        ]

    return pl.pallas_call(
        functools.partial(_ffn_body, tm=tm, n_tiles=n_tiles),
        grid=(b, n_tiles),
        in_specs=with_halo(d) + with_halo(NA_WIDTH) + with_halo(GLA_VW) + [
            pl.BlockSpec((1, SUBLANES, d), lambda bi, ti: (bi, N_META // SUBLANES - 1, 0)),
            pl.BlockSpec(wo.shape, lambda bi, ti: (0, 0), pipeline_mode=single),
            _const_spec(g0.shape),
            _const_spec(g1.shape),
            pl.BlockSpec(win.shape, lambda bi, ti: (0, 0), pipeline_mode=single),
            _const_spec(cw.shape),
            pl.BlockSpec(wout.shape, lambda bi, ti: (0, 0), pipeline_mode=single),
            _const_spec(g2.shape),
        ],
        out_specs=pl.BlockSpec((1, tm, d), lambda bi, ti: (bi, ti, 0)),
        out_shape=jax.ShapeDtypeStruct((b, t, d), F32),
        scratch_shapes=[pltpu.VMEM((tm, D_FF), BF16),
                        pltpu.VMEM((d // LANES, tm + SUBLANES * PERM_PITCH_PAD, LANES), F32),
                        pltpu.VMEM((tm, d), F32)],
        compiler_params=pltpu.CompilerParams(
            dimension_semantics=("arbitrary", "arbitrary"), vmem_limit_bytes=VMEM_LIMIT),
        name="ffn",
    )(x, x, x, o_na, o_na, o_na, o_gl, o_gl, o_gl, h1m, wo, g0, g1, win, cw, wout, g2)


def _cast_body(*refs):
    n = len(refs) // 2
    for src, dst in zip(refs[:n], refs[n:]):
        dst[...] = src[...].astype(BF16)


def _cast_inproj_body(w_ref, o_ref):
    rows = w_ref.shape[0]
    col = pl.program_id(0) * rows + lax.broadcasted_iota(jnp.int32, (rows, 1), 0)
    o_ref[...] = (w_ref[...] * jnp.where(col < NA_WIDTH, NA_Q_SCALE, 1.0)).astype(BF16)


def _cast_weights(body, ws, steps, name):
    def rows_of(w):
        return pl.BlockSpec((w.shape[0] // steps, w.shape[1]), lambda i: (i, 0))

    assert all(w.shape[0] % (steps * 2 * SUBLANES) == 0 for w in ws)
    return pl.pallas_call(
        body,
        grid=(steps,),
        in_specs=[rows_of(w) for w in ws],
        out_specs=[rows_of(w) for w in ws],
        out_shape=[jax.ShapeDtypeStruct(w.shape, BF16) for w in ws],
        name=name,
    )(*ws)


def _na_bias_body(base_ref, o_ref):
    w, kw = GRID_W, NA_WIN_COLS
    cq = lax.broadcasted_iota(jnp.int32, (w, 2 * w), 0)
    kk = lax.broadcasted_iota(jnp.int32, (w, 2 * w), 1) % w
    cs = jnp.clip(cq - kw // 2, 0, w - kw)
    in_win = (kk >= cs) & (kk < cs + kw)
    for e in range(o_ref.shape[1]):
        rows = jnp.broadcast_to(base_ref[0, e:e + 1, :], (w, 2 * w))
        shifted = pltpu.roll(rows, 2 * w - (kw - 1), 1, stride=1, stride_axis=0)
        o_ref[0, e] = jnp.where(in_win, shifted * LOG2E, MASK_NEG)


def _na_bias_table(rpb):
    h, nr, nc = rpb.shape
    w = GRID_W
    assert 2 * w == LANES and nc <= w
    padded = jnp.pad(rpb.astype(F32), ((0, 0), (0, 0), (0, w - nc)))
    base = jnp.concatenate([padded[:, :-1], padded[:, 1:]], axis=-1)
    return pl.pallas_call(
        _na_bias_body,
        grid=(h,),
        in_specs=[pl.BlockSpec((1, nr - 1, 2 * w), lambda i: (i, 0, 0))],
        out_specs=pl.BlockSpec((1, nr - 1, w, 2 * w), lambda i: (i, 0, 0, 0)),
        out_shape=jax.ShapeDtypeStruct((h, nr - 1, w, 2 * w), F32),
        name="na_bias",
    )(base)


def kernel(x, meta_tokens, norm_mix_pre, w_in, na_rel_bias, na_out_gain, gla_gate_up_fwd,
           gla_gate_bias_fwd, gla_gate_up_bwd, gla_gate_bias_bwd, gla_out_gain, w_o, norm_mix_post,
           norm_ffn_pre, w_ffn_in, ffn_conv_w, ffn_conv_b, w_ffn_out, norm_ffn_post):
    b, t, d = x.shape
    depth = w_in.shape[0]
    assert depth == 1, "meta rows are only carried as far as a single layer needs them"
    assert t % GRID_W == 0 and t // GRID_W >= NA_WIN_ROWS and N_META == 2 * SUBLANES
    l = 0
    row = lambda a: a[l].reshape(1, -1).astype(F32)

    (w_in_t,) = _cast_weights(_cast_inproj_body, (jnp.swapaxes(w_in[l], 0, 1),), 2, "cast_inproj")
    wo, win, wout = _cast_weights(_cast_body, (w_o[l], w_ffn_in[l], w_ffn_out[l]), WEIGHT_CAST_STEPS,
                                  "cast_weights")
    cw =jnp.concatenate([ffn_conv_w[l], ffn_conv_b[l][None]], axis=0).astype(F32)
    t2 = _na_bias_table(na_rel_bias[l])
    zpad = jnp.zeros((GLA_GATE_RANK, GLA_KW), BF16)
    gate_up = jnp.concatenate(
        [jnp.concatenate([gla_gate_up_fwd[l].astype(BF16), zpad], axis=1),
         jnp.concatenate([zpad, gla_gate_up_bwd[l].astype(BF16)], axis=1)], axis=0)
    gate_bias = jnp.concatenate([row(gla_gate_bias_fwd), row(gla_gate_bias_bwd)], axis=1)

    x2 = x.reshape(b * t, d)
    g_pre = row(norm_mix_pre)
    na_x, gl_x = _inproj(x2, g_pre, w_in_t, 1024, 2)
    na_m, gl_m = _inproj(meta_tokens.astype(F32), g_pre, w_in_t, N_META, 1)
    na_x = na_x.reshape(b, t, NA_COLS)
    gl_x = gl_x.reshape(b, t, GLA_COLS)

    na_gain = row(na_out_gain)
    o_na = _na(na_x, na_m, t2, na_gain, 8)
    o_na_m = _na_meta(na_m, na_gain)
    o_gl, o_gl_m = _gla(gl_x, gl_m, gate_up, gate_bias, row(gla_out_gain))

    g_post = row(norm_mix_post)
    h1m = _mix(jnp.broadcast_to(meta_tokens.astype(F32)[None], (b, N_META, d)).reshape(b * N_META, d),
               jnp.broadcast_to(o_na_m[None], (b, N_META, NA_WIDTH)).reshape(b * N_META, NA_WIDTH),
               o_gl_m.reshape(b * N_META, GLA_VW), wo, g_post, b * N_META)
    return _ffn(x, o_na, o_gl, h1m.reshape(b, N_META, d), wo, g_post, row(norm_ffn_pre), win, cw, wout,
                row(norm_ffn_post), 512)
```

```python
import functools

import jax
import jax.numpy as jnp
from jax import lax
from jax.experimental import pallas as pl
from jax.experimental.pallas import tpu as pltpu

F32 = jnp.float32
BF16 = jnp.bfloat16

N_META = 16
GRID_W = 64
NA_WIN_ROWS = 8
NA_WIN_COLS = 16
NA_HEADS = 8
NA_HEAD_DIM = 64
NA_WIDTH = NA_HEADS * NA_HEAD_DIM
GLA_HEADS = 4
GLA_DK = 64
GLA_DV = 128
GLA_KW = GLA_HEADS * GLA_DK
GLA_VW = GLA_HEADS * GLA_DV
GLA_GATE_RANK = 16
GLA_GATE_TAU = 16.0
GLA_CHUNK = 64
GLA_PAD = (-N_META) % GLA_CHUNK
NA_COLS = 3 * NA_WIDTH
GLA_COLS = 2 * GLA_KW + 2 * GLA_VW + 2 * GLA_GATE_RANK
D_FF = 2816
FF_BLK = 256
PERM_PITCH_PAD = 8
CONV_W = 3
RMS_EPS = 1e-6
MASK_NEG = -1e30
LOG2E = 1.4426950408889634
NA_Q_SCALE = NA_HEAD_DIM ** -0.5 * LOG2E
WEIGHT_CAST_STEPS = 8
NA_SKEW = 1

LANES = 128
SUBLANES = 8
HALO_ROWS = 16
VMEM_LIMIT = 56 * 1024 * 1024

_CONTRACT_LAST = (((1,), (1,)), ((), ()))
_CONTRACT_FIRST = (((0,), (0,)), ((), ()))


def _rms(x, g):
    return x * lax.rsqrt(jnp.mean(x * x, axis=-1, keepdims=True) + RMS_EPS) * g


def _aligned(v, m):
    return v if isinstance(v, int) else pl.multiple_of(v, m)


def _interleave(gens, skew, newest_first=False):
    live = [True] * len(gens)
    tick = 0
    order = list(range(len(gens)))
    if newest_first:
        order.reverse()
    while any(live):
        for u in order:
            if live[u] and tick >= u * skew:
                try:
                    next(gens[u])
                except StopIteration:
                    live[u] = False
        tick += 1


def _const_spec(shape):
    nd = len(shape)
    return pl.BlockSpec(shape, lambda *_: (0,) * nd)


def _inproj_body(x_ref, g_ref, wt_ref, na_ref, gl_ref, *, parts):
    g = g_ref[...]
    pm = x_ref.shape[0] // parts
    u = _rms(x_ref[0:pm], g).astype(BF16)
    for p in range(parts):
        rows = slice(p * pm, (p + 1) * pm)
        na_ref[rows] = lax.dot_general(u, wt_ref[:NA_COLS], _CONTRACT_LAST,
                                       preferred_element_type=F32).astype(BF16)
        u_next = _rms(x_ref[(p + 1) * pm:(p + 2) * pm], g).astype(BF16) if p + 1 < parts else None
        gl_ref[rows] = lax.dot_general(u, wt_ref[NA_COLS:], _CONTRACT_LAST,
                                       preferred_element_type=F32).astype(BF16)
        u = u_next


def _inproj(x2, g, w_in_t, tm, parts):
    rows, d = x2.shape
    single = pl.Buffered(1)
    return pl.pallas_call(
        functools.partial(_inproj_body, parts=parts),
        grid=(rows // tm,),
        in_specs=[
            pl.BlockSpec((tm, d), lambda i: (i, 0)),
            _const_spec(g.shape),
            pl.BlockSpec(w_in_t.shape, lambda i: (0, 0), pipeline_mode=single),
        ],
        out_specs=[
            pl.BlockSpec((tm, NA_COLS), lambda i: (i, 0)),
            pl.BlockSpec((tm, GLA_COLS), lambda i: (i, 0)),
        ],
        out_shape=[
            jax.ShapeDtypeStruct((rows, NA_COLS), BF16),
            jax.ShapeDtypeStruct((rows, GLA_COLS), BF16),
        ],
        compiler_params=pltpu.CompilerParams(
            dimension_semantics=("arbitrary",), vmem_limit_bytes=VMEM_LIMIT),
        name="inproj",
    )(x2, g, w_in_t)


def _split_heads_rows(pair, lo):
    zero = jnp.zeros_like(pair)
    return jnp.concatenate([jnp.where(lo, pair, zero), jnp.where(lo, zero, pair)], axis=0)


def _na_body(q_ref, k_ref, v_ref, km_ref, vm_ref, t2_ref, gain_ref, o_ref, *, rq, n_rows):
    j = pl.program_id(1)
    w = GRID_W
    kh = NA_WIN_ROWS
    lo = lax.broadcasted_iota(jnp.int32, (w, LANES), 1) < NA_HEAD_DIM
    n_pairs = NA_HEADS // 2
    units = [(i, p) for i in range(rq) for p in range(n_pairs)]

    outs = {}

    def unit(i, p):
        r = j * rq + i
        rs = jnp.clip(r - kh // 2, 0, n_rows - kh)
        e0 = rs - r + (NA_WIN_ROWS - 1)
        k0 = pl.multiple_of(rs * w, w)
        cols = slice(p * LANES, (p + 1) * LANES)
        qp = q_ref[0, i * w:(i + 1) * w, cols]
        q2 = _split_heads_rows(qp, lo)
        kw = k_ref[0, pl.ds(k0, kh * w), cols]
        s = lax.dot_general(q2, kw, _CONTRACT_LAST, preferred_element_type=F32)
        bias = jnp.concatenate(
            [jnp.concatenate([t2_ref[2 * p + hh, e0 + 2 * jj] for jj in range(kh // 2)], axis=1)
             for hh in range(2)], axis=0)
        s = s + bias
        sm = lax.dot_general(q2, km_ref[:, cols], _CONTRACT_LAST, preferred_element_type=F32)
        m = jnp.maximum(jnp.max(s, axis=-1, keepdims=True), jnp.max(sm, axis=-1, keepdims=True))
        yield
        pw = jnp.exp2(s - m)
        pm = jnp.exp2(sm - m)
        l = jnp.sum(pw, axis=-1, keepdims=True) + jnp.sum(pm, axis=-1, keepdims=True)
        pw = pw.astype(BF16)
        pm = pm.astype(BF16)
        yield
        o2 = (jnp.dot(pw, v_ref[0, pl.ds(k0, kh * w), cols], preferred_element_type=F32)
              + jnp.dot(pm, vm_ref[:, cols], preferred_element_type=F32))
        o2 = o2 / l
        outs[(i, p)] = jnp.where(lo, o2[:w], o2[w:])

    def finish_row(i):
        ssq = jnp.zeros((w, 1), F32)
        for p in range(n_pairs):
            ssq = ssq + jnp.sum(outs[(i, p)] * outs[(i, p)], axis=-1, keepdims=True)
        inv = lax.rsqrt(ssq * (1.0 / NA_WIDTH) + RMS_EPS)
        for p in range(n_pairs):
            cols = slice(p * LANES, (p + 1) * LANES)
            o_ref[0, i * w:(i + 1) * w, cols] = (outs.pop((i, p)) * inv * gain_ref[:, cols]).astype(BF16)

    _interleave([unit(i, p) for i, p in units], NA_SKEW, newest_first=True)
    for i in range(rq):
        finish_row(i)


def _na(na_x, na_m, t2, gain, rq):
    b, t, _ = na_x.shape
    n_rows = t // GRID_W
    nw = NA_WIDTH
    return pl.pallas_call(
        functools.partial(_na_body, rq=rq, n_rows=n_rows),
        grid=(b, n_rows // rq),
        in_specs=[
            pl.BlockSpec((1, rq * GRID_W, nw), lambda bi, j: (bi, j, 0)),
            pl.BlockSpec((1, t, nw), lambda bi, j: (bi, 0, 1)),
            pl.BlockSpec((1, t, nw), lambda bi, j: (bi, 0, 2)),
            pl.BlockSpec((N_META, nw), lambda bi, j: (0, 1)),
            pl.BlockSpec((N_META, nw), lambda bi, j: (0, 2)),
            _const_spec(t2.shape),
            _const_spec(gain.shape),
        ],
        out_specs=pl.BlockSpec((1, rq * GRID_W, nw), lambda bi, j: (bi, j, 0)),
        out_shape=jax.ShapeDtypeStruct((b, t, nw), BF16),
        compiler_params=pltpu.CompilerParams(
            dimension_semantics=("arbitrary", "arbitrary"), vmem_limit_bytes=VMEM_LIMIT),
        name="na",
    )(na_x, na_x, na_x, na_m, na_m, t2, gain)


def _na_meta_body(q_ref, k_ref, v_ref, gain_ref, o_ref):
    lane = lax.broadcasted_iota(jnp.int32, (N_META, NA_WIDTH), 1)
    q = q_ref[...]
    k = k_ref[...]
    v = v_ref[...]
    om = jnp.zeros((N_META, NA_WIDTH), F32)
    for h in range(NA_HEADS):
        in_head = (lane >= h * NA_HEAD_DIM) & (lane < (h + 1) * NA_HEAD_DIM)
        qh = jnp.where(in_head, q, jnp.zeros_like(q))
        s = lax.dot_general(qh, k, _CONTRACT_LAST, preferred_element_type=F32)
        m = jnp.max(s, axis=-1, keepdims=True)
        pw = jnp.exp2(s - m)
        pw = pw / jnp.sum(pw, axis=-1, keepdims=True)
        oh = jnp.dot(pw.astype(BF16), v, preferred_element_type=F32)
        om = jnp.where(in_head, oh, om)
    o_ref[...] = _rms(om, gain_ref[...]).astype(BF16)


def _na_meta(na_m, gain):
    nw = NA_WIDTH
    return pl.pallas_call(
        _na_meta_body,
        grid=(1,),
        in_specs=[
            pl.BlockSpec((N_META, nw), lambda i: (0, 0)),
            pl.BlockSpec((N_META, nw), lambda i: (0, 1)),
            pl.BlockSpec((N_META, nw), lambda i: (0, 2)),
            _const_spec(gain.shape),
        ],
        out_specs=pl.BlockSpec((N_META, nw), lambda i: (0, 0)),
        out_shape=jax.ShapeDtypeStruct((N_META, nw), BF16),
        name="na_meta",
    )(na_m, na_m, na_m, gain)


_GQ, _GK, _GV, _GG, _GZ = 0, GLA_KW, 2 * GLA_KW, 2 * GLA_KW + GLA_VW, 2 * GLA_KW + 2 * GLA_VW


GLA_GROUP = 8
GLA_SKEW = 2


def _gla_body(x_ref, m_ref, up_ref, bias_ref, gain_ref, ox_ref, om_ref, c0, qd, oi, dst, decs, st):
    c = GLA_CHUNK
    kw = GLA_KW
    n_pairs = GLA_HEADS // 2
    t = x_ref.shape[1]
    n_chunks = t // c + 1
    c0[0:GLA_PAD, :] = jnp.zeros((GLA_PAD, GLA_COLS), BF16)
    c0[GLA_PAD:, :] = m_ref[...]

    ti = lax.broadcasted_iota(jnp.int32, (c, c), 0)
    si = lax.broadcasted_iota(jnp.int32, (c, c), 1)
    tri = jnp.where(si <= ti, 1.0, 0.0).astype(BF16)
    lo_c = lax.broadcasted_iota(jnp.int32, (c, LANES), 1) < GLA_DK
    lo_s = lax.broadcasted_iota(jnp.int32, (GLA_DV, 2 * LANES), 1) % LANES < GLA_DK
    t2 = lax.broadcasted_iota(jnp.int32, (2 * c, c), 0) % c
    s2 = lax.broadcasted_iota(jnp.int32, (2 * c, c), 1)
    keep_f = s2 <= t2
    up = up_ref[...]
    bias = bias_ref[...]
    gain = gain_ref[...]

    def reader(n):
        if n is None:
            return lambda a, b: c0[:, a:b]
        r0 = _aligned((n - 1) * c, c)
        return lambda a, b: x_ref[0, pl.ds(r0, c), a:b]

    def cidx(n):
        return 0 if n is None else n

    def prow(n):
        return pl.ds(_aligned(cidx(n) * c, c), c)

    def logsig_decay(gate, n):
        la = (jnp.minimum(gate, 0.0) - jnp.log1p(jnp.exp(-jnp.abs(gate)))) * (1.0 / GLA_GATE_TAU)
        if n is None:
            la = jnp.where(lax.broadcasted_iota(jnp.int32, (c, 1), 0) >= GLA_PAD, la, 0.0)
        hi = la.astype(BF16)
        return la, hi, (la - hi.astype(F32)).astype(BF16)

    def pass_a_chunk(n):
        rd = reader(n)
        gate = jnp.dot(rd(_GZ, _GZ + 2 * GLA_GATE_RANK), up, preferred_element_type=F32) + bias
        yield
        la_f, hi_f, low_f = logsig_decay(gate[:, :kw], n)
        yield
        la_b, hi_b, low_b = logsig_decay(gate[:, kw:], n)
        yield
        cs = jnp.dot(tri, jnp.concatenate([hi_f, hi_b, low_f, low_b], axis=1), preferred_element_type=F32)
        yield
        pre = cs[:, :2 * kw] + cs[:, 2 * kw:]
        b_f = pre[:, :kw]
        bl_f = b_f[c - 1:c]
        bl_b = pre[c - 1:c, kw:]
        b_b = bl_b - pre[:, kw:] + la_b
        q = rd(_GQ, _GQ + kw).astype(F32) * (GLA_DK ** -0.5)
        k = rd(_GK, _GK + kw).astype(F32)
        yield
        qd_f = (q * jnp.exp(b_f)).astype(BF16)
        ki_f = (k * jnp.exp(-b_f)).astype(BF16)
        qd[prow(n), :kw] = qd_f
        yield
        qd_b = (q * jnp.exp(b_b)).astype(BF16)
        ki_b = (k * jnp.exp(-b_b)).astype(BF16)
        qd[prow(n), kw:] = qd_b
        yield
        araw = []
        for p in range(n_pairs):
            cols = slice(p * LANES, (p + 1) * LANES)
            a_f = lax.dot_general(_split_heads_rows(qd_f[:, cols], lo_c), ki_f[:, cols], _CONTRACT_LAST,
                                  preferred_element_type=F32)
            yield
            a_b = lax.dot_general(_split_heads_rows(qd_b[:, cols], lo_c), ki_b[:, cols], _CONTRACT_LAST,
                                  preferred_element_type=F32)
            araw.append((a_f, a_b))
            yield
        ke = jnp.concatenate([k * jnp.exp(bl_f - b_f), k * jnp.exp(bl_b - b_b)], axis=1).astype(BF16)
        dec = jnp.concatenate([jnp.exp(bl_f), jnp.exp(bl_b)], axis=1)
        drow = _aligned(cidx(n) * SUBLANES, SUBLANES)
        decs[pl.ds(drow, SUBLANES), :] = jnp.broadcast_to(dec, (SUBLANES, 2 * kw))
        yield
        for p in range(n_pairs):
            amat = jnp.where(keep_f, araw[p][0], araw[p][1]).astype(BF16)
            kcat = jnp.concatenate([ke[:, p * LANES:(p + 1) * LANES],
                                    ke[:, kw + p * LANES:kw + (p + 1) * LANES]], axis=1)
            incr = []
            for hh in range(2):
                h = 2 * p + hh
                vh = rd(_GV + h * GLA_DV, _GV + (h + 1) * GLA_DV)
                yield
                oi[prow(n), h * GLA_DV:(h + 1) * GLA_DV] = jnp.dot(
                    amat[hh * c:(hh + 1) * c], vh, preferred_element_type=F32)
                yield
                incr.append(lax.dot_general(vh, kcat, _CONTRACT_FIRST, preferred_element_type=F32))
            yield
            dst[cidx(n), p] = jnp.where(lo_s, incr[0], incr[1])

    def pass_c_chunk(n):
        rd = reader(n)
        for p in range(n_pairs):
            qf = qd[prow(n), p * LANES:(p + 1) * LANES]
            qb = qd[prow(n), kw + p * LANES:kw + (p + 1) * LANES]
            q2 = jnp.concatenate([_split_heads_rows(qf, lo_c), _split_heads_rows(qb, lo_c)], axis=1)
            inter = lax.dot_general(q2, dst[cidx(n), p].astype(BF16), _CONTRACT_LAST,
                                    preferred_element_type=F32)
            yield
            for hh in range(2):
                h = 2 * p + hh
                hc = slice(h * GLA_DV, (h + 1) * GLA_DV)
                o = oi[prow(n), hc] + inter[hh * c:(hh + 1) * c]
                g = rd(_GG + h * GLA_DV, _GG + (h + 1) * GLA_DV).astype(F32)
                res = (_rms(o, gain) * (g * jax.nn.sigmoid(g))).astype(BF16)
                if n is None:
                    om_ref[0, :, hc] = res[GLA_PAD:]
                else:
                    ox_ref[0, pl.ds(_aligned((n - 1) * c, c), c), hc] = res
                yield

    def run_group(make_gen, chunks, skew):
        _interleave([make_gen(n) for n in chunks], skew)

    first_group = [None] + list(range(1, GLA_GROUP + 1))
    n_groups = (n_chunks - 1) // GLA_GROUP - 1

    def group_chunks(i):
        return [1 + GLA_GROUP * (i + 1) + u for u in range(GLA_GROUP)]

    run_group(pass_a_chunk, first_group, GLA_SKEW)

    def pass_a(i, carry):
        run_group(pass_a_chunk, group_chunks(i), GLA_SKEW)
        return carry

    lax.fori_loop(0, n_groups, pass_a, 0)

    st[...] = jnp.zeros(st.shape, F32)

    def scan(it, carry):
        for n, lanes, off in ((it, slice(0, LANES), 0), (n_chunks - 1 - it, slice(LANES, 2 * LANES), kw)):
            drow = _aligned(n * SUBLANES, SUBLANES)
            for p in range(n_pairs):
                inc = dst[n, p, :, lanes]
                s_in = st[p, :, lanes]
                dst[n, p, :, lanes] = s_in
                dec = decs[pl.ds(drow, 1), off + p * LANES:off + (p + 1) * LANES]
                st[p, :, lanes] = dec * s_in + inc
        return carry

    lax.fori_loop(0, n_chunks, scan, 0)

    run_group(pass_c_chunk, first_group, 1)

    def pass_c(i, carry):
        run_group(pass_c_chunk, group_chunks(i), 1)
        return carry

    lax.fori_loop(0, n_groups, pass_c, 0)


def _gla(gl_x, gl_m, up, bias, gain):
    b, t, _ = gl_x.shape
    assert (t // GLA_CHUNK) % GLA_GROUP == 0
    n_chunks = t // GLA_CHUNK + 1
    lp = n_chunks * GLA_CHUNK
    n_pairs = GLA_HEADS // 2
    return pl.pallas_call(
        _gla_body,
        grid=(b,),
        in_specs=[
            pl.BlockSpec((1, t, GLA_COLS), lambda bi: (bi, 0, 0)),
            _const_spec(gl_m.shape),
            _const_spec(up.shape),
            _const_spec(bias.shape),
            _const_spec(gain.shape),
        ],
        out_specs=[
            pl.BlockSpec((1, t, GLA_VW), lambda bi: (bi, 0, 0)),
            pl.BlockSpec((1, N_META, GLA_VW), lambda bi: (bi, 0, 0)),
        ],
        out_shape=[
            jax.ShapeDtypeStruct((b, t, GLA_VW), BF16),
            jax.ShapeDtypeStruct((b, N_META, GLA_VW), BF16),
        ],
        scratch_shapes=[
            pltpu.VMEM((GLA_CHUNK, GLA_COLS), BF16),
            pltpu.VMEM((lp, 2 * GLA_KW), BF16),
            pltpu.VMEM((lp, GLA_VW), F32),
            pltpu.VMEM((n_chunks, n_pairs, GLA_DV, 2 * LANES), F32),
            pltpu.VMEM((n_chunks * SUBLANES, 2 * GLA_KW), F32),
            pltpu.VMEM((n_pairs, GLA_DV, 2 * LANES), F32),
        ],
        compiler_params=pltpu.CompilerParams(
            dimension_semantics=("arbitrary",), vmem_limit_bytes=VMEM_LIMIT),
        name="gla",
    )(gl_x, gl_m, up, bias, gain)


def _mix_body(x_ref, na_ref, gl_ref, wo_ref, g_ref, h_ref):
    mix = (jnp.dot(na_ref[...], wo_ref[:NA_WIDTH], preferred_element_type=F32)
           + jnp.dot(gl_ref[...], wo_ref[NA_WIDTH:], preferred_element_type=F32))
    h_ref[...] = x_ref[...] + _rms(mix, g_ref[...])


def _mix(x2, o_na, o_gl, wo, g, tm):
    rows, d = x2.shape
    return pl.pallas_call(
        _mix_body,
        grid=(rows // tm,),
        in_specs=[
            pl.BlockSpec((tm, d), lambda i: (i, 0)),
            pl.BlockSpec((tm, NA_WIDTH), lambda i: (i, 0)),
            pl.BlockSpec((tm, GLA_VW), lambda i: (i, 0)),
            _const_spec(wo.shape),
            _const_spec(g.shape),
        ],
        out_specs=pl.BlockSpec((tm, d), lambda i: (i, 0)),
        out_shape=jax.ShapeDtypeStruct((rows, d), F32),
        compiler_params=pltpu.CompilerParams(
            dimension_semantics=("arbitrary",), vmem_limit_bytes=VMEM_LIMIT),
        name="mix",
    )(x2, o_na, o_gl, wo, g)


def _ffn_body(xm_ref, xp_ref, xn_ref, nam_ref, nap_ref, nan_ref, glm_ref, glp_ref, gln_ref, hmeta_ref,
              wo_ref, g0_ref, g1_ref, win_ref, cw_ref, wout_ref, g2_ref,
              o_ref, y_ref, perm_ref, h_ref, *, tm, n_tiles):
    t = pl.program_id(1)
    sl = SUBLANES
    nv = tm // sl
    n_col = perm_ref.shape[0]
    hr = HALO_ROWS

    na = jnp.concatenate([nap_ref[0], nam_ref[0], nan_ref[0]], axis=0)
    gl = jnp.concatenate([glp_ref[0], glm_ref[0], gln_ref[0]], axis=0)
    xe = jnp.concatenate([xp_ref[0], xm_ref[0], xn_ref[0]], axis=0)
    mixed = (jnp.dot(na, wo_ref[:NA_WIDTH], preferred_element_type=F32)
             + jnp.dot(gl, wo_ref[NA_WIDTH:], preferred_element_type=F32))
    h1e = xe + _rms(mixed, g0_ref[...])
    h_ref[...] = h1e[hr:hr + tm]
    h_prev = h1e[hr - sl:hr]
    h_next = h1e[hr + tm:hr + tm + sl]

    pitch = perm_ref.shape[1] // sl

    def restride(x, to_permuted):
        groups = []
        if to_permuted:
            for c in range(n_col):
                for s in range(sl):
                    perm_ref[c, s * pitch:s * pitch + nv] = x[s * nv:(s + 1) * nv, c * LANES:(c + 1) * LANES]
        else:
            for c in range(n_col):
                perm_ref[c, 0:tm] = x[:, c * LANES:(c + 1) * LANES]
        for k in range(nv):
            if to_permuted:
                start, stride = k, pitch
            else:
                start, stride = sl * ((sl * k) % nv) + (sl * k) // nv, sl
            groups.append(jnp.concatenate(
                [perm_ref[c, pl.ds(start, sl, stride=stride), :] for c in range(n_col)], axis=1))
        return jnp.concatenate(groups, axis=0)

    g1 = g1_ref[...]
    n2_main = restride(_rms(h_ref[...], g1), True)
    prev = jnp.where(t == 0, hmeta_ref[0], h_prev)
    sub = lax.broadcasted_iota(jnp.int32, (sl, 1), 0)
    slab = jnp.where(sub == 0, pltpu.roll(prev, 1, 0),
                     jnp.where(sub == sl - 1, pltpu.roll(h_next, sl - 1, 0), 0.0))
    keep = jnp.logical_or(sub < sl - 1, t < n_tiles - 1)
    n2 = jnp.concatenate([n2_main, jnp.where(keep, _rms(slab, g1), 0.0)], axis=0).astype(BF16)

    nb = D_FF // FF_BLK
    sub_b = lax.broadcasted_iota(jnp.int32, (sl, FF_BLK), 0)

    def proj(cb):
        va = jnp.dot(n2, win_ref[:, cb * FF_BLK:(cb + 1) * FF_BLK], preferred_element_type=F32)
        ga = jnp.dot(n2, win_ref[:, D_FF + cb * FF_BLK:D_FF + (cb + 1) * FF_BLK],
                     preferred_element_type=F32)
        return va, ga

    def conv(a, taps):
        main = a[:tm]
        hal = a[tm:]
        first_prev = jnp.where(sub_b == 0, hal, pltpu.roll(main[tm - sl:], 1, 0))
        last_next = jnp.where(sub_b == sl - 1, hal, pltpu.roll(main[:sl], sl - 1, 0))
        a_prev = jnp.concatenate([first_prev, main[:tm - sl]], axis=0)
        a_next = jnp.concatenate([main[sl:], last_next], axis=0)
        return a_prev * taps[0:1] + main * taps[1:2] + a_next * taps[2:3] + taps[3:4]

    def act(cb, va, ga):
        val = conv(va, cw_ref[:, cb * FF_BLK:(cb + 1) * FF_BLK])
        gate = conv(ga, cw_ref[:, D_FF + cb * FF_BLK:D_FF + (cb + 1) * FF_BLK])
        y_ref[:, cb * FF_BLK:(cb + 1) * FF_BLK] = (jax.nn.gelu(gate, approximate=True) * val).astype(BF16)

    pending = proj(0)
    for cb in range(nb):
        nxt = proj(cb + 1) if cb + 1 < nb else None
        act(cb, *pending)
        pending = nxt
    r = _rms(jnp.dot(y_ref[...], wout_ref[...], preferred_element_type=F32), g2_ref[...])
    o_ref[0] = h_ref[...] + restride(r, False)


def _ffn(x, o_na, o_gl, h1m, wo, g0, g1, win, cw, wout, g2, tm):
    b, t, d = x.shape
    n_tiles = t // tm
    hb = tm // HALO_ROWS
    last = t // HALO_ROWS - 1
    single = pl.Buffered(1)

    def with_halo(width):
        return [
            pl.BlockSpec((1, tm, width), lambda bi, ti: (bi, ti, 0)),
            pl.BlockSpec((1, HALO_ROWS, width), lambda bi, ti: (bi, jnp.maximum(ti * hb - 1, 0), 0)),
            pl.BlockSpec((1, HALO_ROWS, width), lambda bi, ti: (bi, jnp.minimum((ti + 1) * hb, last), 0)),
        ]

    return pl.pallas_call(
        functools.partial(_ffn_body, tm=tm, n_tiles=n_tiles),
        grid=(b, n_tiles),
        in_specs=with_halo(d) + with_halo(NA_WIDTH) + with_halo(GLA_VW) + [
            pl.BlockSpec((1, SUBLANES, d), lambda bi, ti: (bi, N_META // SUBLANES - 1, 0)),
            pl.BlockSpec(wo.shape, lambda bi, ti: (0, 0), pipeline_mode=single),
            _const_spec(g0.shape),
            _const_spec(g1.shape),
            pl.BlockSpec(win.shape, lambda bi, ti: (0, 0), pipeline_mode=single),
            _const_spec(cw.shape),
            pl.BlockSpec(wout.shape, lambda bi, ti: (0, 0), pipeline_mode=single),
            _const_spec(g2.shape),
        ],
        out_specs=pl.BlockSpec((1, tm, d), lambda bi, ti: (bi, ti, 0)),
        out_shape=jax.ShapeDtypeStruct((b, t, d), F32),
        scratch_shapes=[pltpu.VMEM((tm, D_FF), BF16),
                        pltpu.VMEM((d // LANES, tm + SUBLANES * PERM_PITCH_PAD, LANES), F32),
                        pltpu.VMEM((tm, d), F32)],
        compiler_params=pltpu.CompilerParams(
            dimension_semantics=("arbitrary", "arbitrary"), vmem_limit_bytes=VMEM_LIMIT),
        name="ffn",
    )(x, x, x, o_na, o_na, o_na, o_gl, o_gl, o_gl, h1m, wo, g0, g1, win, cw, wout, g2)


def _cast_body(*refs):
    n = len(refs) // 2
    for src, dst in zip(refs[:n], refs[n:]):
        dst[...] = src[...].astype(BF16)


def _cast_inproj_body(w_ref, o_ref):
    rows = w_ref.shape[0]
    col = pl.program_id(0) * rows + lax.broadcasted_iota(jnp.int32, (rows, 1), 0)
    o_ref[...] = (w_ref[...] * jnp.where(col < NA_WIDTH, NA_Q_SCALE, 1.0)).astype(BF16)


def _cast_weights(body, ws, steps, name):
    def rows_of(w):
        return pl.BlockSpec((w.shape[0] // steps, w.shape[1]), lambda i: (i, 0))

    assert all(w.shape[0] % (steps * 2 * SUBLANES) == 0 for w in ws)
    return pl.pallas_call(
        body,
        grid=(steps,),
        in_specs=[rows_of(w) for w in ws],
        out_specs=[rows_of(w) for w in ws],
        out_shape=[jax.ShapeDtypeStruct(w.shape, BF16) for w in ws],
        name=name,
    )(*ws)


def _na_bias_body(base_ref, o_ref):
    w, kw = GRID_W, NA_WIN_COLS
    cq = lax.broadcasted_iota(jnp.int32, (w, 2 * w), 0)
    kk = lax.broadcasted_iota(jnp.int32, (w, 2 * w), 1) % w
    cs = jnp.clip(cq - kw // 2, 0, w - kw)
    in_win = (kk >= cs) & (kk < cs + kw)
    for e in range(o_ref.shape[1]):
        rows = jnp.broadcast_to(base_ref[0, e:e + 1, :], (w, 2 * w))
        shifted = pltpu.roll(rows, 2 * w - (kw - 1), 1, stride=1, stride_axis=0)
        o_ref[0, e] = jnp.where(in_win, shifted * LOG2E, MASK_NEG)


def _na_bias_table(rpb):
    h, nr, nc = rpb.shape
    w = GRID_W
    assert 2 * w == LANES and nc <= w
    padded = jnp.pad(rpb.astype(F32), ((0, 0), (0, 0), (0, w - nc)))
    base = jnp.concatenate([padded[:, :-1], padded[:, 1:]], axis=-1)
    return pl.pallas_call(
        _na_bias_body,
        grid=(h,),
        in_specs=[pl.BlockSpec((1, nr - 1, 2 * w), lambda i: (i, 0, 0))],
        out_specs=pl.BlockSpec((1, nr - 1, w, 2 * w), lambda i: (i, 0, 0, 0)),
        out_shape=jax.ShapeDtypeStruct((h, nr - 1, w, 2 * w), F32),
        name="na_bias",
    )(base)


def kernel(x, meta_tokens, norm_mix_pre, w_in, na_rel_bias, na_out_gain, gla_gate_up_fwd,
           gla_gate_bias_fwd, gla_gate_up_bwd, gla_gate_bias_bwd, gla_out_gain, w_o, norm_mix_post,
           norm_ffn_pre, w_ffn_in, ffn_conv_w, ffn_conv_b, w_ffn_out, norm_ffn_post):
    b, t, d = x.shape
    depth = w_in.shape[0]
    assert depth == 1, "meta rows are only carried as far as a single layer needs them"
    assert t % GRID_W == 0 and t // GRID_W >= NA_WIN_ROWS and N_META == 2 * SUBLANES
    l = 0
    row = lambda a: a[l].reshape(1, -1).astype(F32)

    (w_in_t,) = _cast_weights(_cast_inproj_body, (jnp.swapaxes(w_in[l], 0, 1),), 2, "cast_inproj")
    wo, win, wout = _cast_weights(_cast_body, (w_o[l], w_ffn_in[l], w_ffn_out[l]), WEIGHT_CAST_STEPS,
                                  "cast_weights")
    cw =jnp.concatenate([ffn_conv_w[l], ffn_conv_b[l][None]], axis=0).astype(F32)
    t2 = _na_bias_table(na_rel_bias[l])
    zpad = jnp.zeros((GLA_GATE_RANK, GLA_KW), BF16)
    gate_up = jnp.concatenate(
        [jnp.concatenate([gla_gate_up_fwd[l].astype(BF16), zpad], axis=1),
         jnp.concatenate([zpad, gla_gate_up_bwd[l].astype(BF16)], axis=1)], axis=0)
    gate_bias = jnp.concatenate([row(gla_gate_bias_fwd), row(gla_gate_bias_bwd)], axis=1)

    x2 = x.reshape(b * t, d)
    g_pre = row(norm_mix_pre)
    na_x, gl_x = _inproj(x2, g_pre, w_in_t, 1024, 2)
    na_m, gl_m = _inproj(meta_tokens.astype(F32), g_pre, w_in_t, N_META, 1)
    na_x = na_x.reshape(b, t, NA_COLS)
    gl_x = gl_x.reshape(b, t, GLA_COLS)

    na_gain = row(na_out_gain)
    o_na = _na(na_x, na_m, t2, na_gain, 8)
    o_na_m = _na_meta(na_m, na_gain)
    o_gl, o_gl_m = _gla(gl_x, gl_m, gate_up, gate_bias, row(gla_out_gain))

    g_post = row(norm_mix_post)
    h1m = _mix(jnp.broadcast_to(meta_tokens.astype(F32)[None], (b, N_META, d)).reshape(b * N_META, d),
               jnp.broadcast_to(o_na_m[None], (b, N_META, NA_WIDTH)).reshape(b * N_META, NA_WIDTH),
               o_gl_m.reshape(b * N_META, GLA_VW), wo, g_post, b * N_META)
    return _ffn(x, o_na, o_gl, h1m.reshape(b, N_META, d), wo, g_post, row(norm_ffn_pre), win, cw, wout,
                row(norm_ffn_post), 512)
```

```python
import functools

import jax
import jax.numpy as jnp
from jax import lax
from jax.experimental import pallas as pl
from jax.experimental.pallas import tpu as pltpu

F32 = jnp.float32
BF16 = jnp.bfloat16

N_META = 16
GRID_W = 64
NA_WIN_ROWS = 8
NA_WIN_COLS = 16
NA_HEADS = 8
NA_HEAD_DIM = 64
NA_WIDTH = NA_HEADS * NA_HEAD_DIM
GLA_HEADS = 4
GLA_DK = 64
GLA_DV = 128
GLA_KW = GLA_HEADS * GLA_DK
GLA_VW = GLA_HEADS * GLA_DV
GLA_GATE_RANK = 16
GLA_GATE_TAU = 16.0
GLA_CHUNK = 64
GLA_PAD = (-N_META) % GLA_CHUNK
NA_COLS = 3 * NA_WIDTH
GLA_COLS = 2 * GLA_KW + 2 * GLA_VW + 2 * GLA_GATE_RANK
D_FF = 2816
FF_BLK = 256
PERM_PITCH_PAD = 8
CONV_W = 3
RMS_EPS = 1e-6
MASK_NEG = -1e30
LOG2E = 1.4426950408889634
NA_Q_SCALE = NA_HEAD_DIM ** -0.5 * LOG2E
WEIGHT_CAST_STEPS = 8
NA_SKEW = 1

LANES = 128
SUBLANES = 8
HALO_ROWS = 16
VMEM_LIMIT = 56 * 1024 * 1024

_CONTRACT_LAST = (((1,), (1,)), ((), ()))
_CONTRACT_FIRST = (((0,), (0,)), ((), ()))


def _rms(x, g):
    return x * lax.rsqrt(jnp.mean(x * x, axis=-1, keepdims=True) + RMS_EPS) * g


def _aligned(v, m):
    return v if isinstance(v, int) else pl.multiple_of(v, m)


def _interleave(gens, skew, newest_first=False):
    live = [True] * len(gens)
    tick = 0
    order = list(range(len(gens)))
    if newest_first:
        order.reverse()
    while any(live):
        for u in order:
            if live[u] and tick >= u * skew:
                try:
                    next(gens[u])
                except StopIteration:
                    live[u] = False
        tick += 1


def _const_spec(shape):
    nd = len(shape)
    return pl.BlockSpec(shape, lambda *_: (0,) * nd)


def _inproj_body(x_ref, g_ref, wt_ref, na_ref, gl_ref, *, parts):
    g = g_ref[...]
    pm = x_ref.shape[0] // parts
    u = _rms(x_ref[0:pm], g).astype(BF16)
    for p in range(parts):
        rows = slice(p * pm, (p + 1) * pm)
        na_ref[rows] = lax.dot_general(u, wt_ref[:NA_COLS], _CONTRACT_LAST,
                                       preferred_element_type=F32).astype(BF16)
        u_next = _rms(x_ref[(p + 1) * pm:(p + 2) * pm], g).astype(BF16) if p + 1 < parts else None
        gl_ref[rows] = lax.dot_general(u, wt_ref[NA_COLS:], _CONTRACT_LAST,
                                       preferred_element_type=F32).astype(BF16)
        u = u_next


def _inproj(x2, g, w_in_t, tm, parts):
    rows, d = x2.shape
    single = pl.Buffered(1)
    return pl.pallas_call(
        functools.partial(_inproj_body, parts=parts),
        grid=(rows // tm,),
        in_specs=[
            pl.BlockSpec((tm, d), lambda i: (i, 0)),
            _const_spec(g.shape),
            pl.BlockSpec(w_in_t.shape, lambda i: (0, 0), pipeline_mode=single),
        ],
        out_specs=[
            pl.BlockSpec((tm, NA_COLS), lambda i: (i, 0)),
            pl.BlockSpec((tm, GLA_COLS), lambda i: (i, 0)),
        ],
        out_shape=[
            jax.ShapeDtypeStruct((rows, NA_COLS), BF16),
            jax.ShapeDtypeStruct((rows, GLA_COLS), BF16),
        ],
        compiler_params=pltpu.CompilerParams(
            dimension_semantics=("arbitrary",), vmem_limit_bytes=VMEM_LIMIT),
        name="inproj",
    )(x2, g, w_in_t)


def _split_heads_rows(pair, lo):
    zero = jnp.zeros_like(pair)
    return jnp.concatenate([jnp.where(lo, pair, zero), jnp.where(lo, zero, pair)], axis=0)


def _na_meta(q_ref, k_ref, v_ref, gain_ref, o_ref):
    lane = lax.broadcasted_iota(jnp.int32, (N_META, NA_WIDTH), 1)
    q = q_ref[...]
    k = k_ref[...]
    v = v_ref[...]
    om = jnp.zeros((N_META, NA_WIDTH), F32)
    for h in range(NA_HEADS):
        in_head = (lane >= h * NA_HEAD_DIM) & (lane < (h + 1) * NA_HEAD_DIM)
        qh = jnp.where(in_head, q, jnp.zeros_like(q))
        s = lax.dot_general(qh, k, _CONTRACT_LAST, preferred_element_type=F32)
        m = jnp.max(s, axis=-1, keepdims=True)
        pw = jnp.exp2(s - m)
        pw = pw / jnp.sum(pw, axis=-1, keepdims=True)
        oh = jnp.dot(pw.astype(BF16), v, preferred_element_type=F32)
        om = jnp.where(in_head, oh, om)
    o_ref[...] = _rms(om, gain_ref[...]).astype(BF16)


def _na_bias_table(base_ref, o_ref):
    w, kw = GRID_W, NA_WIN_COLS
    cq = lax.broadcasted_iota(jnp.int32, (w, 2 * w), 0)
    kk = lax.broadcasted_iota(jnp.int32, (w, 2 * w), 1) % w
    cs = jnp.clip(cq - kw // 2, 0, w - kw)
    in_win = (kk >= cs) & (kk < cs + kw)
    for h in range(o_ref.shape[0]):
        for e in range(o_ref.shape[1]):
            rows = jnp.broadcast_to(base_ref[h, e:e + 1, :], (w, 2 * w))
            shifted = pltpu.roll(rows, 2 * w - (kw - 1), 1, stride=1, stride_axis=0)
            o_ref[h, e] = jnp.where(in_win, shifted * LOG2E, MASK_NEG)


def _na_body(q_ref, k_ref, v_ref, qm_ref, km_ref, vm_ref, base_ref, gain_ref, o_ref, om_ref, t2_ref,
             *, rq, n_rows):
    j = pl.program_id(1)

    @pl.when(jnp.logical_and(pl.program_id(0) == 0, j == 0))
    def _():
        _na_bias_table(base_ref, t2_ref)
        _na_meta(qm_ref, km_ref, vm_ref, gain_ref, om_ref)

    w = GRID_W
    kh = NA_WIN_ROWS
    lo = lax.broadcasted_iota(jnp.int32, (w, LANES), 1) < NA_HEAD_DIM
    n_pairs = NA_HEADS // 2
    units = [(i, p) for i in range(rq) for p in range(n_pairs)]

    outs = {}

    def unit(i, p):
        r = j * rq + i
        rs = jnp.clip(r - kh // 2, 0, n_rows - kh)
        e0 = rs - r + (NA_WIN_ROWS - 1)
        k0 = pl.multiple_of(rs * w, w)
        cols = slice(p * LANES, (p + 1) * LANES)
        qp = q_ref[0, i * w:(i + 1) * w, cols]
        q2 = _split_heads_rows(qp, lo)
        kw = k_ref[0, pl.ds(k0, kh * w), cols]
        s = lax.dot_general(q2, kw, _CONTRACT_LAST, preferred_element_type=F32)
        bias = jnp.concatenate(
            [jnp.concatenate([t2_ref[2 * p + hh, e0 + 2 * jj] for jj in range(kh // 2)], axis=1)
             for hh in range(2)], axis=0)
        s = s + bias
        sm = lax.dot_general(q2, km_ref[:, cols], _CONTRACT_LAST, preferred_element_type=F32)
        m = jnp.maximum(jnp.max(s, axis=-1, keepdims=True), jnp.max(sm, axis=-1, keepdims=True))
        yield
        pw = jnp.exp2(s - m)
        pm = jnp.exp2(sm - m)
        l = jnp.sum(pw, axis=-1, keepdims=True) + jnp.sum(pm, axis=-1, keepdims=True)
        pw = pw.astype(BF16)
        pm = pm.astype(BF16)
        yield
        o2 = (jnp.dot(pw, v_ref[0, pl.ds(k0, kh * w), cols], preferred_element_type=F32)
              + jnp.dot(pm, vm_ref[:, cols], preferred_element_type=F32))
        o2 = o2 / l
        outs[(i, p)] = jnp.where(lo, o2[:w], o2[w:])

    def finish_row(i):
        ssq = jnp.zeros((w, 1), F32)
        for p in range(n_pairs):
            ssq = ssq + jnp.sum(outs[(i, p)] * outs[(i, p)], axis=-1, keepdims=True)
        inv = lax.rsqrt(ssq * (1.0 / NA_WIDTH) + RMS_EPS)
        for p in range(n_pairs):
            cols = slice(p * LANES, (p + 1) * LANES)
            o_ref[0, i * w:(i + 1) * w, cols] = (outs.pop((i, p)) * inv * gain_ref[:, cols]).astype(BF16)

    _interleave([unit(i, p) for i, p in units], NA_SKEW, newest_first=True)
    for i in range(rq):
        finish_row(i)


def _na(na_x, na_m, rpb, gain, rq):
    b, t, _ = na_x.shape
    n_rows = t // GRID_W
    nw = NA_WIDTH
    h, nr, nc = rpb.shape
    w = GRID_W
    assert 2 * w == LANES and nc <= w
    padded = jnp.pad(rpb.astype(F32), ((0, 0), (0, 0), (0, w - nc)))
    base = jnp.concatenate([padded[:, :-1], padded[:, 1:]], axis=-1)
    return pl.pallas_call(
        functools.partial(_na_body, rq=rq, n_rows=n_rows),
        grid=(b, n_rows // rq),
        in_specs=[
            pl.BlockSpec((1, rq * GRID_W, nw), lambda bi, j: (bi, j, 0)),
            pl.BlockSpec((1, t, nw), lambda bi, j: (bi, 0, 1)),
            pl.BlockSpec((1, t, nw), lambda bi, j: (bi, 0, 2)),
            pl.BlockSpec((N_META, nw), lambda bi, j: (0, 0)),
            pl.BlockSpec((N_META, nw), lambda bi, j: (0, 1)),
            pl.BlockSpec((N_META, nw), lambda bi, j: (0, 2)),
            _const_spec(base.shape),
            _const_spec(gain.shape),
        ],
        out_specs=[
            pl.BlockSpec((1, rq * GRID_W, nw), lambda bi, j: (bi, j, 0)),
            pl.BlockSpec((N_META, nw), lambda bi, j: (0, 0)),
        ],
        out_shape=[
            jax.ShapeDtypeStruct((b, t, nw), BF16),
            jax.ShapeDtypeStruct((N_META, nw), BF16),
        ],
        scratch_shapes=[pltpu.VMEM((h, nr - 1, w, 2 * w), F32)],
        compiler_params=pltpu.CompilerParams(
            dimension_semantics=("arbitrary", "arbitrary"), vmem_limit_bytes=VMEM_LIMIT),
        name="na",
    )(na_x, na_x, na_x, na_m, na_m, na_m, base, gain)


_GQ, _GK, _GV, _GG, _GZ = 0, GLA_KW, 2 * GLA_KW, 2 * GLA_KW + GLA_VW, 2 * GLA_KW + 2 * GLA_VW


GLA_GROUP = 8
GLA_SKEW = 2


def _gla_body(x_ref, m_ref, up_ref, bias_ref, gain_ref, ox_ref, om_ref, c0, qd, oi, dst, decs, st):
    c = GLA_CHUNK
    kw = GLA_KW
    n_pairs = GLA_HEADS // 2
    t = x_ref.shape[1]
    n_chunks = t // c + 1
    c0[0:GLA_PAD, :] = jnp.zeros((GLA_PAD, GLA_COLS), BF16)
    c0[GLA_PAD:, :] = m_ref[...]

    ti = lax.broadcasted_iota(jnp.int32, (c, c), 0)
    si = lax.broadcasted_iota(jnp.int32, (c, c), 1)
    tri = jnp.where(si <= ti, 1.0, 0.0).astype(BF16)
    lo_c = lax.broadcasted_iota(jnp.int32, (c, LANES), 1) < GLA_DK
    lo_s = lax.broadcasted_iota(jnp.int32, (GLA_DV, 2 * LANES), 1) % LANES < GLA_DK
    t2 = lax.broadcasted_iota(jnp.int32, (2 * c, c), 0) % c
    s2 = lax.broadcasted_iota(jnp.int32, (2 * c, c), 1)
    keep_f = s2 <= t2
    up = up_ref[...]
    bias = bias_ref[...]
    gain = gain_ref[...]

    def reader(n):
        if n is None:
            return lambda a, b: c0[:, a:b]
        r0 = _aligned((n - 1) * c, c)
        return lambda a, b: x_ref[0, pl.ds(r0, c), a:b]

    def cidx(n):
        return 0 if n is None else n

    def prow(n):
        return pl.ds(_aligned(cidx(n) * c, c), c)

    def logsig_decay(gate, n):
        la = (jnp.minimum(gate, 0.0) - jnp.log1p(jnp.exp(-jnp.abs(gate)))) * (1.0 / GLA_GATE_TAU)
        if n is None:
            la = jnp.where(lax.broadcasted_iota(jnp.int32, (c, 1), 0) >= GLA_PAD, la, 0.0)
        hi = la.astype(BF16)
        return la, hi, (la - hi.astype(F32)).astype(BF16)

    def pass_a_chunk(n):
        rd = reader(n)
        gate = jnp.dot(rd(_GZ, _GZ + 2 * GLA_GATE_RANK), up, preferred_element_type=F32) + bias
        yield
        la_f, hi_f, low_f = logsig_decay(gate[:, :kw], n)
        yield
        la_b, hi_b, low_b = logsig_decay(gate[:, kw:], n)
        yield
        cs = jnp.dot(tri, jnp.concatenate([hi_f, hi_b, low_f, low_b], axis=1), preferred_element_type=F32)
        yield
        pre = cs[:, :2 * kw] + cs[:, 2 * kw:]
        b_f = pre[:, :kw]
        bl_f = b_f[c - 1:c]
        bl_b = pre[c - 1:c, kw:]
        b_b = bl_b - pre[:, kw:] + la_b
        q = rd(_GQ, _GQ + kw).astype(F32) * (GLA_DK ** -0.5)
        k = rd(_GK, _GK + kw).astype(F32)
        yield
        qd_f = (q * jnp.exp(b_f)).astype(BF16)
        ki_f = (k * jnp.exp(-b_f)).astype(BF16)
        qd[prow(n), :kw] = qd_f
        yield
        qd_b = (q * jnp.exp(b_b)).astype(BF16)
        ki_b = (k * jnp.exp(-b_b)).astype(BF16)
        qd[prow(n), kw:] = qd_b
        yield
        araw = []
        for p in range(n_pairs):
            cols = slice(p * LANES, (p + 1) * LANES)
            a_f = lax.dot_general(_split_heads_rows(qd_f[:, cols], lo_c), ki_f[:, cols], _CONTRACT_LAST,
                                  preferred_element_type=F32)
            yield
            a_b = lax.dot_general(_split_heads_rows(qd_b[:, cols], lo_c), ki_b[:, cols], _CONTRACT_LAST,
                                  preferred_element_type=F32)
            araw.append((a_f, a_b))
            yield
        ke = jnp.concatenate([k * jnp.exp(bl_f - b_f), k * jnp.exp(bl_b - b_b)], axis=1).astype(BF16)
        dec = jnp.concatenate([jnp.exp(bl_f), jnp.exp(bl_b)], axis=1)
        drow = _aligned(cidx(n) * SUBLANES, SUBLANES)
        decs[pl.ds(drow, SUBLANES), :] = jnp.broadcast_to(dec, (SUBLANES, 2 * kw))
        yield
        for p in range(n_pairs):
            amat = jnp.where(keep_f, araw[p][0], araw[p][1]).astype(BF16)
            kcat = jnp.concatenate([ke[:, p * LANES:(p + 1) * LANES],
                                    ke[:, kw + p * LANES:kw + (p + 1) * LANES]], axis=1)
            incr = []
            for hh in range(2):
                h = 2 * p + hh
                vh = rd(_GV + h * GLA_DV, _GV + (h + 1) * GLA_DV)
                yield
                oi[prow(n), h * GLA_DV:(h + 1) * GLA_DV] = jnp.dot(
                    amat[hh * c:(hh + 1) * c], vh, preferred_element_type=F32)
                yield
                incr.append(lax.dot_general(vh, kcat, _CONTRACT_FIRST, preferred_element_type=F32))
            yield
            dst[cidx(n), p] = jnp.where(lo_s, incr[0], incr[1])

    def pass_c_chunk(n):
        rd = reader(n)
        for p in range(n_pairs):
            qf = qd[prow(n), p * LANES:(p + 1) * LANES]
            qb = qd[prow(n), kw + p * LANES:kw + (p + 1) * LANES]
            q2 = jnp.concatenate([_split_heads_rows(qf, lo_c), _split_heads_rows(qb, lo_c)], axis=1)
            inter = lax.dot_general(q2, dst[cidx(n), p].astype(BF16), _CONTRACT_LAST,
                                    preferred_element_type=F32)
            yield
            for hh in range(2):
                h = 2 * p + hh
                hc = slice(h * GLA_DV, (h + 1) * GLA_DV)
                o = oi[prow(n), hc] + inter[hh * c:(hh + 1) * c]
                g = rd(_GG + h * GLA_DV, _GG + (h + 1) * GLA_DV).astype(F32)
                res = (_rms(o, gain) * (g * jax.nn.sigmoid(g))).astype(BF16)
                if n is None:
                    om_ref[0, :, hc] = res[GLA_PAD:]
                else:
                    ox_ref[0, pl.ds(_aligned((n - 1) * c, c), c), hc] = res
                yield

    def run_group(make_gen, chunks, skew):
        _interleave([make_gen(n) for n in chunks], skew)

    first_group = [None] + list(range(1, GLA_GROUP + 1))
    n_groups = (n_chunks - 1) // GLA_GROUP - 1

    def group_chunks(i):
        return [1 + GLA_GROUP * (i + 1) + u for u in range(GLA_GROUP)]

    run_group(pass_a_chunk, first_group, GLA_SKEW)

    def pass_a(i, carry):
        run_group(pass_a_chunk, group_chunks(i), GLA_SKEW)
        return carry

    lax.fori_loop(0, n_groups, pass_a, 0)

    st[...] = jnp.zeros(st.shape, F32)

    def scan(it, carry):
        for n, lanes, off in ((it, slice(0, LANES), 0), (n_chunks - 1 - it, slice(LANES, 2 * LANES), kw)):
            drow = _aligned(n * SUBLANES, SUBLANES)
            for p in range(n_pairs):
                inc = dst[n, p, :, lanes]
                s_in = st[p, :, lanes]
                dst[n, p, :, lanes] = s_in
                dec = decs[pl.ds(drow, 1), off + p * LANES:off + (p + 1) * LANES]
                st[p, :, lanes] = dec * s_in + inc
        return carry

    lax.fori_loop(0, n_chunks, scan, 0)

    run_group(pass_c_chunk, first_group, 1)

    def pass_c(i, carry):
        run_group(pass_c_chunk, group_chunks(i), 1)
        return carry

    lax.fori_loop(0, n_groups, pass_c, 0)


def _gla(gl_x, gl_m, up, bias, gain):
    b, t, _ = gl_x.shape
    assert (t // GLA_CHUNK) % GLA_GROUP == 0
    n_chunks = t // GLA_CHUNK + 1
    lp = n_chunks * GLA_CHUNK
    n_pairs = GLA_HEADS // 2
    return pl.pallas_call(
        _gla_body,
        grid=(b,),
        in_specs=[
            pl.BlockSpec((1, t, GLA_COLS), lambda bi: (bi, 0, 0)),
            _const_spec(gl_m.shape),
            _const_spec(up.shape),
            _const_spec(bias.shape),
            _const_spec(gain.shape),
        ],
        out_specs=[
            pl.BlockSpec((1, t, GLA_VW), lambda bi: (bi, 0, 0)),
            pl.BlockSpec((1, N_META, GLA_VW), lambda bi: (bi, 0, 0)),
        ],
        out_shape=[
            jax.ShapeDtypeStruct((b, t, GLA_VW), BF16),
            jax.ShapeDtypeStruct((b, N_META, GLA_VW), BF16),
        ],
        scratch_shapes=[
            pltpu.VMEM((GLA_CHUNK, GLA_COLS), BF16),
            pltpu.VMEM((lp, 2 * GLA_KW), BF16),
            pltpu.VMEM((lp, GLA_VW), F32),
            pltpu.VMEM((n_chunks, n_pairs, GLA_DV, 2 * LANES), F32),
            pltpu.VMEM((n_chunks * SUBLANES, 2 * GLA_KW), F32),
            pltpu.VMEM((n_pairs, GLA_DV, 2 * LANES), F32),
        ],
        compiler_params=pltpu.CompilerParams(
            dimension_semantics=("arbitrary",), vmem_limit_bytes=VMEM_LIMIT),
        name="gla",
    )(gl_x, gl_m, up, bias, gain)


def _ffn_body(xm_ref, xp_ref, xn_ref, nam_ref, nap_ref, nan_ref, glm_ref, glp_ref, gln_ref,
              xmeta_ref, nameta_ref, glmeta_ref, wo_ref, g0_ref, g1_ref, win_ref, cw_ref, wout_ref, g2_ref,
              o_ref, y_ref, perm_ref, h_ref, *, tm, n_tiles):
    t = pl.program_id(1)
    sl = SUBLANES
    nv = tm // sl
    n_col = perm_ref.shape[0]
    hr = HALO_ROWS

    first = t == 0
    na = jnp.concatenate([jnp.where(first, nameta_ref[...], nap_ref[0]), nam_ref[0], nan_ref[0]],
                         axis=0)
    gl = jnp.concatenate([jnp.where(first, glmeta_ref[0], glp_ref[0]), glm_ref[0], gln_ref[0]], axis=0)
    xe = jnp.concatenate([jnp.where(first, xmeta_ref[...], xp_ref[0]), xm_ref[0], xn_ref[0]],
                         axis=0)
    mixed = (jnp.dot(na, wo_ref[:NA_WIDTH], preferred_element_type=F32)
             + jnp.dot(gl, wo_ref[NA_WIDTH:], preferred_element_type=F32))
    h1e = xe + _rms(mixed, g0_ref[...])
    h_ref[...] = h1e[hr:hr + tm]
    h_prev = h1e[hr - sl:hr]
    h_next = h1e[hr + tm:hr + tm + sl]

    pitch = perm_ref.shape[1] // sl

    def restride(x, to_permuted):
        groups = []
        if to_permuted:
            for c in range(n_col):
                for s in range(sl):
                    perm_ref[c, s * pitch:s * pitch + nv] = x[s * nv:(s + 1) * nv, c * LANES:(c + 1) * LANES]
        else:
            for c in range(n_col):
                perm_ref[c, 0:tm] = x[:, c * LANES:(c + 1) * LANES]
        for k in range(nv):
            if to_permuted:
                start, stride = k, pitch
            else:
                start, stride = sl * ((sl * k) % nv) + (sl * k) // nv, sl
            groups.append(jnp.concatenate(
                [perm_ref[c, pl.ds(start, sl, stride=stride), :] for c in range(n_col)], axis=1))
        return jnp.concatenate(groups, axis=0)

    g1 = g1_ref[...]
    n2_main = restride(_rms(h_ref[...], g1), True)
    sub = lax.broadcasted_iota(jnp.int32, (sl, 1), 0)
    slab = jnp.where(sub == 0, pltpu.roll(h_prev, 1, 0),
                     jnp.where(sub == sl - 1, pltpu.roll(h_next, sl - 1, 0), 0.0))
    keep = jnp.logical_or(sub < sl - 1, t < n_tiles - 1)
    n2 = jnp.concatenate([n2_main, jnp.where(keep, _rms(slab, g1), 0.0)], axis=0).astype(BF16)

    nb = D_FF // FF_BLK
    sub_b = lax.broadcasted_iota(jnp.int32, (sl, FF_BLK), 0)

    def proj(cb):
        va = jnp.dot(n2, win_ref[:, cb * FF_BLK:(cb + 1) * FF_BLK], preferred_element_type=F32)
        ga = jnp.dot(n2, win_ref[:, D_FF + cb * FF_BLK:D_FF + (cb + 1) * FF_BLK],
                     preferred_element_type=F32)
        return va, ga

    def conv(a, taps):
        main = a[:tm]
        hal = a[tm:]
        first_prev = jnp.where(sub_b == 0, hal, pltpu.roll(main[tm - sl:], 1, 0))
        last_next = jnp.where(sub_b == sl - 1, hal, pltpu.roll(main[:sl], sl - 1, 0))
        a_prev = jnp.concatenate([first_prev, main[:tm - sl]], axis=0)
        a_next = jnp.concatenate([main[sl:], last_next], axis=0)
        return a_prev * taps[0:1] + main * taps[1:2] + a_next * taps[2:3] + taps[3:4]

    def act(cb, va, ga):
        val = conv(va, cw_ref[:, cb * FF_BLK:(cb + 1) * FF_BLK])
        gate = conv(ga, cw_ref[:, D_FF + cb * FF_BLK:D_FF + (cb + 1) * FF_BLK])
        y_ref[:, cb * FF_BLK:(cb + 1) * FF_BLK] = (jax.nn.gelu(gate, approximate=True) * val).astype(BF16)

    pending = proj(0)
    for cb in range(nb):
        nxt = proj(cb + 1) if cb + 1 < nb else None
        act(cb, *pending)
        pending = nxt
    r = _rms(jnp.dot(y_ref[...], wout_ref[...], preferred_element_type=F32), g2_ref[...])
    o_ref[0] = h_ref[...] + restride(r, False)


def _ffn(x, o_na, o_gl, x_meta, o_na_m, o_gl_m, wo, g0, g1, win, cw, wout, g2, tm):
    b, t, d = x.shape
    n_tiles = t // tm
    hb = tm // HALO_ROWS
    last = t // HALO_ROWS - 1
    single = pl.Buffered(1)

    def with_halo(width):
        return [
            pl.BlockSpec((1, tm, width), lambda bi, ti: (bi, ti, 0)),
            pl.BlockSpec((1, HALO_ROWS, width), lambda bi, ti: (bi, jnp.maximum(ti * hb - 1, 0), 0)),
            pl.BlockSpec((1, HALO_ROWS, width), lambda bi, ti: (bi, jnp.minimum((ti + 1) * hb, last), 0)),
        ]

    return pl.pallas_call(
        functools.partial(_ffn_body, tm=tm, n_tiles=n_tiles),
        grid=(b, n_tiles),
        in_specs=with_halo(d) + with_halo(NA_WIDTH) + with_halo(GLA_VW) + [
            _const_spec(x_meta.shape),
            _const_spec(o_na_m.shape),
            pl.BlockSpec((1, N_META, GLA_VW), lambda bi, ti: (bi, 0, 0)),
            pl.BlockSpec(wo.shape, lambda bi, ti: (0, 0), pipeline_mode=single),
            _const_spec(g0.shape),
            _const_spec(g1.shape),
            pl.BlockSpec(win.shape, lambda bi, ti: (0, 0), pipeline_mode=single),
            _const_spec(cw.shape),
            pl.BlockSpec(wout.shape, lambda bi, ti: (0, 0), pipeline_mode=single),
            _const_spec(g2.shape),
        ],
        out_specs=pl.BlockSpec((1, tm, d), lambda bi, ti: (bi, ti, 0)),
        out_shape=jax.ShapeDtypeStruct((b, t, d), F32),
        scratch_shapes=[pltpu.VMEM((tm, D_FF), BF16),
                        pltpu.VMEM((d // LANES, tm + SUBLANES * PERM_PITCH_PAD, LANES), F32),
                        pltpu.VMEM((tm, d), F32)],
        compiler_params=pltpu.CompilerParams(
            dimension_semantics=("arbitrary", "arbitrary"), vmem_limit_bytes=VMEM_LIMIT),
        name="ffn",
    )(x, x, x, o_na, o_na, o_na, o_gl, o_gl, o_gl, x_meta, o_na_m, o_gl_m, wo, g0, g1, win, cw, wout, g2)


def _cast_body(*refs):
    n = len(refs) // 2
    for src, dst in zip(refs[:n], refs[n:]):
        dst[...] = src[...].astype(BF16)


def _cast_inproj_body(w_ref, o_ref):
    rows = w_ref.shape[0]
    col = pl.program_id(0) * rows + lax.broadcasted_iota(jnp.int32, (rows, 1), 0)
    o_ref[...] = (w_ref[...] * jnp.where(col < NA_WIDTH, NA_Q_SCALE, 1.0)).astype(BF16)


def _cast_weights(body, ws, steps, name):
    def rows_of(w):
        return pl.BlockSpec((w.shape[0] // steps, w.shape[1]), lambda i: (i, 0))

    assert all(w.shape[0] % (steps * 2 * SUBLANES) == 0 for w in ws)
    return pl.pallas_call(
        body,
        grid=(steps,),
        in_specs=[rows_of(w) for w in ws],
        out_specs=[rows_of(w) for w in ws],
        out_shape=[jax.ShapeDtypeStruct(w.shape, BF16) for w in ws],
        name=name,
    )(*ws)


def kernel(x, meta_tokens, norm_mix_pre, w_in, na_rel_bias, na_out_gain, gla_gate_up_fwd,
           gla_gate_bias_fwd, gla_gate_up_bwd, gla_gate_bias_bwd, gla_out_gain, w_o, norm_mix_post,
           norm_ffn_pre, w_ffn_in, ffn_conv_w, ffn_conv_b, w_ffn_out, norm_ffn_post):
    b, t, d = x.shape
    depth = w_in.shape[0]
    assert depth == 1, "meta rows are only carried as far as a single layer needs them"
    assert t % GRID_W == 0 and t // GRID_W >= NA_WIN_ROWS and N_META == 2 * SUBLANES
    l = 0
    row = lambda a: a[l].reshape(1, -1).astype(F32)

    (w_in_t,) = _cast_weights(_cast_inproj_body, (jnp.swapaxes(w_in[l], 0, 1),), 2, "cast_inproj")
    wo, win, wout = _cast_weights(_cast_body, (w_o[l], w_ffn_in[l], w_ffn_out[l]), WEIGHT_CAST_STEPS,
                                  "cast_weights")
    cw =jnp.concatenate([ffn_conv_w[l], ffn_conv_b[l][None]], axis=0).astype(F32)
    zpad = jnp.zeros((GLA_GATE_RANK, GLA_KW), BF16)
    gate_up = jnp.concatenate(
        [jnp.concatenate([gla_gate_up_fwd[l].astype(BF16), zpad], axis=1),
         jnp.concatenate([zpad, gla_gate_up_bwd[l].astype(BF16)], axis=1)], axis=0)
    gate_bias = jnp.concatenate([row(gla_gate_bias_fwd), row(gla_gate_bias_bwd)], axis=1)

    x2 = x.reshape(b * t, d)
    g_pre = row(norm_mix_pre)
    na_x, gl_x = _inproj(x2, g_pre, w_in_t, 1024, 2)
    na_m, gl_m = _inproj(meta_tokens.astype(F32), g_pre, w_in_t, N_META, 1)
    na_x = na_x.reshape(b, t, NA_COLS)
    gl_x = gl_x.reshape(b, t, GLA_COLS)

    na_gain = row(na_out_gain)
    o_na, o_na_m = _na(na_x, na_m, na_rel_bias[l], na_gain, 8)
    o_gl, o_gl_m = _gla(gl_x, gl_m, gate_up, gate_bias, row(gla_out_gain))

    g_post = row(norm_mix_post)
    assert N_META == HALO_ROWS
    return _ffn(x, o_na, o_gl, meta_tokens.astype(F32), o_na_m, o_gl_m, wo, g_post, row(norm_ffn_pre), win, cw, wout,
                row(norm_ffn_post), 512)
```

```python
import functools

import jax
import jax.numpy as jnp
from jax import lax
from jax.experimental import pallas as pl
from jax.experimental.pallas import tpu as pltpu

F32 = jnp.float32
BF16 = jnp.bfloat16

N_META = 16
GRID_W = 64
NA_WIN_ROWS = 8
NA_WIN_COLS = 16
NA_HEADS = 8
NA_HEAD_DIM = 64
NA_WIDTH = NA_HEADS * NA_HEAD_DIM
GLA_HEADS = 4
GLA_DK = 64
GLA_DV = 128
GLA_KW = GLA_HEADS * GLA_DK
GLA_VW = GLA_HEADS * GLA_DV
GLA_GATE_RANK = 16
GLA_GATE_TAU = 16.0
GLA_CHUNK = 64
GLA_PAD = (-N_META) % GLA_CHUNK
NA_COLS = 3 * NA_WIDTH
GLA_COLS = 2 * GLA_KW + 2 * GLA_VW + 2 * GLA_GATE_RANK
D_FF = 2816
FF_BLK = 256
PERM_PITCH_PAD = 8
CONV_W = 3
RMS_EPS = 1e-6
MASK_NEG = -1e30
LOG2E = 1.4426950408889634
NA_Q_SCALE = NA_HEAD_DIM ** -0.5 * LOG2E
WEIGHT_CAST_STEPS = 8
NA_SKEW = 1

LANES = 128
SUBLANES = 8
HALO_ROWS = 16
VMEM_LIMIT = 56 * 1024 * 1024

_CONTRACT_LAST = (((1,), (1,)), ((), ()))
_CONTRACT_FIRST = (((0,), (0,)), ((), ()))


def _rms(x, g):
    return x * lax.rsqrt(jnp.mean(x * x, axis=-1, keepdims=True) + RMS_EPS) * g


def _aligned(v, m):
    return v if isinstance(v, int) else pl.multiple_of(v, m)


def _interleave(gens, skew, newest_first=False):
    live = [True] * len(gens)
    tick = 0
    order = list(range(len(gens)))
    if newest_first:
        order.reverse()
    while any(live):
        for u in order:
            if live[u] and tick >= u * skew:
                try:
                    next(gens[u])
                except StopIteration:
                    live[u] = False
        tick += 1


def _stream_cast(src_hbm, dst, stage, sems, chunks, scale_rows=None):
    rows = src_hbm.shape[0] // chunks

    def copy(c):
        slot = c % 2
        return pltpu.make_async_copy(src_hbm.at[pl.ds(c * rows, rows)], stage.at[slot], sems.at[slot])

    copy(0).start()
    for c in range(chunks):
        if c + 1 < chunks:
            copy(c + 1).start()
        copy(c).wait()
        w = stage[c % 2]
        if scale_rows is not None:
            w = w * scale_rows(c * rows, rows)
        dst[c * rows:(c + 1) * rows] = w.astype(BF16)


def _const_spec(shape):
    nd = len(shape)
    return pl.BlockSpec(shape, lambda *_: (0,) * nd)


def _inproj_body(x_ref, g_ref, wt_ref, na_ref, gl_ref, *, parts):
    g = g_ref[...]
    pm = x_ref.shape[0] // parts
    u = _rms(x_ref[0:pm], g).astype(BF16)
    for p in range(parts):
        rows = slice(p * pm, (p + 1) * pm)
        na_ref[rows] = lax.dot_general(u, wt_ref[:NA_COLS], _CONTRACT_LAST,
                                       preferred_element_type=F32).astype(BF16)
        u_next = _rms(x_ref[(p + 1) * pm:(p + 2) * pm], g).astype(BF16) if p + 1 < parts else None
        gl_ref[rows] = lax.dot_general(u, wt_ref[NA_COLS:], _CONTRACT_LAST,
                                       preferred_element_type=F32).astype(BF16)
        u = u_next


def _inproj(x2, g, w_in_t, tm, parts):
    rows, d = x2.shape
    single = pl.Buffered(1)
    return pl.pallas_call(
        functools.partial(_inproj_body, parts=parts),
        grid=(rows // tm,),
        in_specs=[
            pl.BlockSpec((tm, d), lambda i: (i, 0)),
            _const_spec(g.shape),
            pl.BlockSpec(w_in_t.shape, lambda i: (0, 0), pipeline_mode=single),
        ],
        out_specs=[
            pl.BlockSpec((tm, NA_COLS), lambda i: (i, 0)),
            pl.BlockSpec((tm, GLA_COLS), lambda i: (i, 0)),
        ],
        out_shape=[
            jax.ShapeDtypeStruct((rows, NA_COLS), BF16),
            jax.ShapeDtypeStruct((rows, GLA_COLS), BF16),
        ],
        compiler_params=pltpu.CompilerParams(
            dimension_semantics=("arbitrary",), vmem_limit_bytes=VMEM_LIMIT),
        name="inproj",
    )(x2, g, w_in_t)


def _split_heads_rows(pair, lo):
    zero = jnp.zeros_like(pair)
    return jnp.concatenate([jnp.where(lo, pair, zero), jnp.where(lo, zero, pair)], axis=0)


def _na_meta(q_ref, k_ref, v_ref, gain_ref, o_ref):
    lane = lax.broadcasted_iota(jnp.int32, (N_META, NA_WIDTH), 1)
    q = q_ref[...]
    k = k_ref[...]
    v = v_ref[...]
    om = jnp.zeros((N_META, NA_WIDTH), F32)
    for h in range(NA_HEADS):
        in_head = (lane >= h * NA_HEAD_DIM) & (lane < (h + 1) * NA_HEAD_DIM)
        qh = jnp.where(in_head, q, jnp.zeros_like(q))
        s = lax.dot_general(qh, k, _CONTRACT_LAST, preferred_element_type=F32)
        m = jnp.max(s, axis=-1, keepdims=True)
        pw = jnp.exp2(s - m)
        pw = pw / jnp.sum(pw, axis=-1, keepdims=True)
        oh = jnp.dot(pw.astype(BF16), v, preferred_element_type=F32)
        om = jnp.where(in_head, oh, om)
    o_ref[...] = _rms(om, gain_ref[...]).astype(BF16)


def _na_bias_table(base_ref, o_ref):
    w, kw = GRID_W, NA_WIN_COLS
    cq = lax.broadcasted_iota(jnp.int32, (w, 2 * w), 0)
    kk = lax.broadcasted_iota(jnp.int32, (w, 2 * w), 1) % w
    cs = jnp.clip(cq - kw // 2, 0, w - kw)
    in_win = (kk >= cs) & (kk < cs + kw)
    for h in range(o_ref.shape[0]):
        for e in range(o_ref.shape[1]):
            rows = jnp.broadcast_to(base_ref[h, e:e + 1, :], (w, 2 * w))
            shifted = pltpu.roll(rows, 2 * w - (kw - 1), 1, stride=1, stride_axis=0)
            o_ref[h, e] = jnp.where(in_win, shifted * LOG2E, MASK_NEG)


def _na_body(q_ref, k_ref, v_ref, qm_ref, km_ref, vm_ref, base_ref, gain_ref, o_ref, om_ref, t2_ref,
             *, rq, n_rows):
    j = pl.program_id(1)

    @pl.when(jnp.logical_and(pl.program_id(0) == 0, j == 0))
    def _():
        _na_bias_table(base_ref, t2_ref)
        _na_meta(qm_ref, km_ref, vm_ref, gain_ref, om_ref)

    w = GRID_W
    kh = NA_WIN_ROWS
    lo = lax.broadcasted_iota(jnp.int32, (w, LANES), 1) < NA_HEAD_DIM
    n_pairs = NA_HEADS // 2
    units = [(i, p) for i in range(rq) for p in range(n_pairs)]

    outs = {}

    def unit(i, p):
        r = j * rq + i
        rs = jnp.clip(r - kh // 2, 0, n_rows - kh)
        e0 = rs - r + (NA_WIN_ROWS - 1)
        k0 = pl.multiple_of(rs * w, w)
        cols = slice(p * LANES, (p + 1) * LANES)
        qp = q_ref[0, i * w:(i + 1) * w, cols]
        q2 = _split_heads_rows(qp, lo)
        kw = k_ref[0, pl.ds(k0, kh * w), cols]
        s = lax.dot_general(q2, kw, _CONTRACT_LAST, preferred_element_type=F32)
        bias = jnp.concatenate(
            [jnp.concatenate([t2_ref[2 * p + hh, e0 + 2 * jj] for jj in range(kh // 2)], axis=1)
             for hh in range(2)], axis=0)
        s = s + bias
        sm = lax.dot_general(q2, km_ref[:, cols], _CONTRACT_LAST, preferred_element_type=F32)
        m = jnp.maximum(jnp.max(s, axis=-1, keepdims=True), jnp.max(sm, axis=-1, keepdims=True))
        yield
        pw = jnp.exp2(s - m)
        pm = jnp.exp2(sm - m)
        l = jnp.sum(pw, axis=-1, keepdims=True) + jnp.sum(pm, axis=-1, keepdims=True)
        pw = pw.astype(BF16)
        pm = pm.astype(BF16)
        yield
        o2 = (jnp.dot(pw, v_ref[0, pl.ds(k0, kh * w), cols], preferred_element_type=F32)
              + jnp.dot(pm, vm_ref[:, cols], preferred_element_type=F32))
        o2 = o2 / l
        outs[(i, p)] = jnp.where(lo, o2[:w], o2[w:])

    def finish_row(i):
        ssq = jnp.zeros((w, 1), F32)
        for p in range(n_pairs):
            ssq = ssq + jnp.sum(outs[(i, p)] * outs[(i, p)], axis=-1, keepdims=True)
        inv = lax.rsqrt(ssq * (1.0 / NA_WIDTH) + RMS_EPS)
        for p in range(n_pairs):
            cols = slice(p * LANES, (p + 1) * LANES)
            o_ref[0, i * w:(i + 1) * w, cols] = (outs.pop((i, p)) * inv * gain_ref[:, cols]).astype(BF16)

    _interleave([unit(i, p) for i, p in units], NA_SKEW, newest_first=True)
    for i in range(rq):
        finish_row(i)


def _na(na_x, na_m, rpb, gain, rq):
    b, t, _ = na_x.shape
    n_rows = t // GRID_W
    nw = NA_WIDTH
    h, nr, nc = rpb.shape
    w = GRID_W
    assert 2 * w == LANES and nc <= w
    padded = jnp.pad(rpb.astype(F32), ((0, 0), (0, 0), (0, w - nc)))
    base = jnp.concatenate([padded[:, :-1], padded[:, 1:]], axis=-1)
    return pl.pallas_call(
        functools.partial(_na_body, rq=rq, n_rows=n_rows),
        grid=(b, n_rows // rq),
        in_specs=[
            pl.BlockSpec((1, rq * GRID_W, nw), lambda bi, j: (bi, j, 0)),
            pl.BlockSpec((1, t, nw), lambda bi, j: (bi, 0, 1)),
            pl.BlockSpec((1, t, nw), lambda bi, j: (bi, 0, 2)),
            pl.BlockSpec((N_META, nw), lambda bi, j: (0, 0)),
            pl.BlockSpec((N_META, nw), lambda bi, j: (0, 1)),
            pl.BlockSpec((N_META, nw), lambda bi, j: (0, 2)),
            _const_spec(base.shape),
            _const_spec(gain.shape),
        ],
        out_specs=[
            pl.BlockSpec((1, rq * GRID_W, nw), lambda bi, j: (bi, j, 0)),
            pl.BlockSpec((N_META, nw), lambda bi, j: (0, 0)),
        ],
        out_shape=[
            jax.ShapeDtypeStruct((b, t, nw), BF16),
            jax.ShapeDtypeStruct((N_META, nw), BF16),
        ],
        scratch_shapes=[pltpu.VMEM((h, nr - 1, w, 2 * w), F32)],
        compiler_params=pltpu.CompilerParams(
            dimension_semantics=("arbitrary", "arbitrary"), vmem_limit_bytes=VMEM_LIMIT),
        name="na",
    )(na_x, na_x, na_x, na_m, na_m, na_m, base, gain)


_GQ, _GK, _GV, _GG, _GZ = 0, GLA_KW, 2 * GLA_KW, 2 * GLA_KW + GLA_VW, 2 * GLA_KW + 2 * GLA_VW


GLA_GROUP = 8
GLA_SKEW = 2


def _gla_body(x_ref, m_ref, up_ref, bias_ref, gain_ref, ox_ref, om_ref, c0, qd, oi, dst, decs, st):
    c = GLA_CHUNK
    kw = GLA_KW
    n_pairs = GLA_HEADS // 2
    t = x_ref.shape[1]
    n_chunks = t // c + 1
    c0[0:GLA_PAD, :] = jnp.zeros((GLA_PAD, GLA_COLS), BF16)
    c0[GLA_PAD:, :] = m_ref[...]

    ti = lax.broadcasted_iota(jnp.int32, (c, c), 0)
    si = lax.broadcasted_iota(jnp.int32, (c, c), 1)
    tri = jnp.where(si <= ti, 1.0, 0.0).astype(BF16)
    lo_c = lax.broadcasted_iota(jnp.int32, (c, LANES), 1) < GLA_DK
    lo_s = lax.broadcasted_iota(jnp.int32, (GLA_DV, 2 * LANES), 1) % LANES < GLA_DK
    t2 = lax.broadcasted_iota(jnp.int32, (2 * c, c), 0) % c
    s2 = lax.broadcasted_iota(jnp.int32, (2 * c, c), 1)
    keep_f = s2 <= t2
    up = up_ref[...]
    bias = bias_ref[...]
    gain = gain_ref[...]

    def reader(n):
        if n is None:
            return lambda a, b: c0[:, a:b]
        r0 = _aligned((n - 1) * c, c)
        return lambda a, b: x_ref[0, pl.ds(r0, c), a:b]

    def cidx(n):
        return 0 if n is None else n

    def prow(n):
        return pl.ds(_aligned(cidx(n) * c, c), c)

    def logsig_decay(gate, n):
        la = (jnp.minimum(gate, 0.0) - jnp.log1p(jnp.exp(-jnp.abs(gate)))) * (1.0 / GLA_GATE_TAU)
        if n is None:
            la = jnp.where(lax.broadcasted_iota(jnp.int32, (c, 1), 0) >= GLA_PAD, la, 0.0)
        hi = la.astype(BF16)
        return la, hi, (la - hi.astype(F32)).astype(BF16)

    def pass_a_chunk(n):
        rd = reader(n)
        gate = jnp.dot(rd(_GZ, _GZ + 2 * GLA_GATE_RANK), up, preferred_element_type=F32) + bias
        yield
        la_f, hi_f, low_f = logsig_decay(gate[:, :kw], n)
        yield
        la_b, hi_b, low_b = logsig_decay(gate[:, kw:], n)
        yield
        cs = jnp.dot(tri, jnp.concatenate([hi_f, hi_b, low_f, low_b], axis=1), preferred_element_type=F32)
        yield
        pre = cs[:, :2 * kw] + cs[:, 2 * kw:]
        b_f = pre[:, :kw]
        bl_f = b_f[c - 1:c]
        bl_b = pre[c - 1:c, kw:]
        b_b = bl_b - pre[:, kw:] + la_b
        q = rd(_GQ, _GQ + kw).astype(F32) * (GLA_DK ** -0.5)
        k = rd(_GK, _GK + kw).astype(F32)
        yield
        qd_f = (q * jnp.exp(b_f)).astype(BF16)
        ki_f = (k * jnp.exp(-b_f)).astype(BF16)
        qd[prow(n), :kw] = qd_f
        yield
        qd_b = (q * jnp.exp(b_b)).astype(BF16)
        ki_b = (k * jnp.exp(-b_b)).astype(BF16)
        qd[prow(n), kw:] = qd_b
        yield
        araw = []
        for p in range(n_pairs):
            cols = slice(p * LANES, (p + 1) * LANES)
            a_f = lax.dot_general(_split_heads_rows(qd_f[:, cols], lo_c), ki_f[:, cols], _CONTRACT_LAST,
                                  preferred_element_type=F32)
            yield
            a_b = lax.dot_general(_split_heads_rows(qd_b[:, cols], lo_c), ki_b[:, cols], _CONTRACT_LAST,
                                  preferred_element_type=F32)
            araw.append((a_f, a_b))
            yield
        ke = jnp.concatenate([k * jnp.exp(bl_f - b_f), k * jnp.exp(bl_b - b_b)], axis=1).astype(BF16)
        dec = jnp.concatenate([jnp.exp(bl_f), jnp.exp(bl_b)], axis=1)
        drow = _aligned(cidx(n) * SUBLANES, SUBLANES)
        decs[pl.ds(drow, SUBLANES), :] = jnp.broadcast_to(dec, (SUBLANES, 2 * kw))
        yield
        for p in range(n_pairs):
            amat = jnp.where(keep_f, araw[p][0], araw[p][1]).astype(BF16)
            kcat = jnp.concatenate([ke[:, p * LANES:(p + 1) * LANES],
                                    ke[:, kw + p * LANES:kw + (p + 1) * LANES]], axis=1)
            incr = []
            for hh in range(2):
                h = 2 * p + hh
                vh = rd(_GV + h * GLA_DV, _GV + (h + 1) * GLA_DV)
                yield
                oi[prow(n), h * GLA_DV:(h + 1) * GLA_DV] = jnp.dot(
                    amat[hh * c:(hh + 1) * c], vh, preferred_element_type=F32)
                yield
                incr.append(lax.dot_general(vh, kcat, _CONTRACT_FIRST, preferred_element_type=F32))
            yield
            dst[cidx(n), p] = jnp.where(lo_s, incr[0], incr[1])

    def pass_c_chunk(n):
        rd = reader(n)
        for p in range(n_pairs):
            qf = qd[prow(n), p * LANES:(p + 1) * LANES]
            qb = qd[prow(n), kw + p * LANES:kw + (p + 1) * LANES]
            q2 = jnp.concatenate([_split_heads_rows(qf, lo_c), _split_heads_rows(qb, lo_c)], axis=1)
            inter = lax.dot_general(q2, dst[cidx(n), p].astype(BF16), _CONTRACT_LAST,
                                    preferred_element_type=F32)
            yield
            for hh in range(2):
                h = 2 * p + hh
                hc = slice(h * GLA_DV, (h + 1) * GLA_DV)
                o = oi[prow(n), hc] + inter[hh * c:(hh + 1) * c]
                g = rd(_GG + h * GLA_DV, _GG + (h + 1) * GLA_DV).astype(F32)
                res = (_rms(o, gain) * (g * jax.nn.sigmoid(g))).astype(BF16)
                if n is None:
                    om_ref[0, :, hc] = res[GLA_PAD:]
                else:
                    ox_ref[0, pl.ds(_aligned((n - 1) * c, c), c), hc] = res
                yield

    def run_group(make_gen, chunks, skew):
        _interleave([make_gen(n) for n in chunks], skew)

    first_group = [None] + list(range(1, GLA_GROUP + 1))
    n_groups = (n_chunks - 1) // GLA_GROUP - 1

    def group_chunks(i):
        return [1 + GLA_GROUP * (i + 1) + u for u in range(GLA_GROUP)]

    run_group(pass_a_chunk, first_group, GLA_SKEW)

    def pass_a(i, carry):
        run_group(pass_a_chunk, group_chunks(i), GLA_SKEW)
        return carry

    lax.fori_loop(0, n_groups, pass_a, 0)

    st[...] = jnp.zeros(st.shape, F32)

    def scan(it, carry):
        for n, lanes, off in ((it, slice(0, LANES), 0), (n_chunks - 1 - it, slice(LANES, 2 * LANES), kw)):
            drow = _aligned(n * SUBLANES, SUBLANES)
            for p in range(n_pairs):
                inc = dst[n, p, :, lanes]
                s_in = st[p, :, lanes]
                dst[n, p, :, lanes] = s_in
                dec = decs[pl.ds(drow, 1), off + p * LANES:off + (p + 1) * LANES]
                st[p, :, lanes] = dec * s_in + inc
        return carry

    lax.fori_loop(0, n_chunks, scan, 0)

    run_group(pass_c_chunk, first_group, 1)

    def pass_c(i, carry):
        run_group(pass_c_chunk, group_chunks(i), 1)
        return carry

    lax.fori_loop(0, n_groups, pass_c, 0)


def _gla(gl_x, gl_m, up, bias, gain):
    b, t, _ = gl_x.shape
    assert (t // GLA_CHUNK) % GLA_GROUP == 0
    n_chunks = t // GLA_CHUNK + 1
    lp = n_chunks * GLA_CHUNK
    n_pairs = GLA_HEADS // 2
    return pl.pallas_call(
        _gla_body,
        grid=(b,),
        in_specs=[
            pl.BlockSpec((1, t, GLA_COLS), lambda bi: (bi, 0, 0)),
            _const_spec(gl_m.shape),
            _const_spec(up.shape),
            _const_spec(bias.shape),
            _const_spec(gain.shape),
        ],
        out_specs=[
            pl.BlockSpec((1, t, GLA_VW), lambda bi: (bi, 0, 0)),
            pl.BlockSpec((1, N_META, GLA_VW), lambda bi: (bi, 0, 0)),
        ],
        out_shape=[
            jax.ShapeDtypeStruct((b, t, GLA_VW), BF16),
            jax.ShapeDtypeStruct((b, N_META, GLA_VW), BF16),
        ],
        scratch_shapes=[
            pltpu.VMEM((GLA_CHUNK, GLA_COLS), BF16),
            pltpu.VMEM((lp, 2 * GLA_KW), BF16),
            pltpu.VMEM((lp, GLA_VW), F32),
            pltpu.VMEM((n_chunks, n_pairs, GLA_DV, 2 * LANES), F32),
            pltpu.VMEM((n_chunks * SUBLANES, 2 * GLA_KW), F32),
            pltpu.VMEM((n_pairs, GLA_DV, 2 * LANES), F32),
        ],
        compiler_params=pltpu.CompilerParams(
            dimension_semantics=("arbitrary",), vmem_limit_bytes=VMEM_LIMIT),
        name="gla",
    )(gl_x, gl_m, up, bias, gain)


def _ffn_body(xm_ref, xp_ref, xn_ref, nam_ref, nap_ref, nan_ref, glm_ref, glp_ref, gln_ref,
              xmeta_ref, nameta_ref, glmeta_ref, wo_hbm, g0_ref, g1_ref, win_hbm, cw_ref, wout_hbm, g2_ref,
              o_ref, y_ref, perm_ref, h_ref, wo_ref, win_ref, wout_ref, st_o, st_i, st_u, sems,
              *, tm, n_tiles):
    t = pl.program_id(1)
    sl = SUBLANES
    nv = tm // sl
    n_col = perm_ref.shape[0]
    hr = HALO_ROWS

    @pl.when(jnp.logical_and(pl.program_id(0) == 0, t == 0))
    def _():
        _stream_cast(wo_hbm, wo_ref, st_o, sems.at[0], WEIGHT_CAST_STEPS)
        _stream_cast(win_hbm, win_ref, st_i, sems.at[1], WEIGHT_CAST_STEPS)
        _stream_cast(wout_hbm, wout_ref, st_u, sems.at[2], WEIGHT_CAST_STEPS)

    first = t == 0
    na = jnp.concatenate([jnp.where(first, nameta_ref[...], nap_ref[0]), nam_ref[0], nan_ref[0]],
                         axis=0)
    gl = jnp.concatenate([jnp.where(first, glmeta_ref[0], glp_ref[0]), glm_ref[0], gln_ref[0]], axis=0)
    xe = jnp.concatenate([jnp.where(first, xmeta_ref[...], xp_ref[0]), xm_ref[0], xn_ref[0]],
                         axis=0)
    mixed = (jnp.dot(na, wo_ref[:NA_WIDTH], preferred_element_type=F32)
             + jnp.dot(gl, wo_ref[NA_WIDTH:], preferred_element_type=F32))
    h1e = xe + _rms(mixed, g0_ref[...])
    h_ref[...] = h1e[hr:hr + tm]
    h_prev = h1e[hr - sl:hr]
    h_next = h1e[hr + tm:hr + tm + sl]

    pitch = perm_ref.shape[1] // sl

    def restride(x, to_permuted):
        groups = []
        if to_permuted:
            for c in range(n_col):
                for s in range(sl):
                    perm_ref[c, s * pitch:s * pitch + nv] = x[s * nv:(s + 1) * nv, c * LANES:(c + 1) * LANES]
        else:
            for c in range(n_col):
                perm_ref[c, 0:tm] = x[:, c * LANES:(c + 1) * LANES]
        for k in range(nv):
            if to_permuted:
                start, stride = k, pitch
            else:
                start, stride = sl * ((sl * k) % nv) + (sl * k) // nv, sl
            groups.append(jnp.concatenate(
                [perm_ref[c, pl.ds(start, sl, stride=stride), :] for c in range(n_col)], axis=1))
        return jnp.concatenate(groups, axis=0)

    g1 = g1_ref[...]
    n2_main = restride(_rms(h_ref[...], g1), True)
    sub = lax.broadcasted_iota(jnp.int32, (sl, 1), 0)
    slab = jnp.where(sub == 0, pltpu.roll(h_prev, 1, 0),
                     jnp.where(sub == sl - 1, pltpu.roll(h_next, sl - 1, 0), 0.0))
    keep = jnp.logical_or(sub < sl - 1, t < n_tiles - 1)
    n2 = jnp.concatenate([n2_main, jnp.where(keep, _rms(slab, g1), 0.0)], axis=0).astype(BF16)

    nb = D_FF // FF_BLK
    sub_b = lax.broadcasted_iota(jnp.int32, (sl, FF_BLK), 0)

    def proj(cb):
        va = jnp.dot(n2, win_ref[:, cb * FF_BLK:(cb + 1) * FF_BLK], preferred_element_type=F32)
        ga = jnp.dot(n2, win_ref[:, D_FF + cb * FF_BLK:D_FF + (cb + 1) * FF_BLK],
                     preferred_element_type=F32)
        return va, ga

    def conv(a, taps):
        main = a[:tm]
        hal = a[tm:]
        first_prev = jnp.where(sub_b == 0, hal, pltpu.roll(main[tm - sl:], 1, 0))
        last_next = jnp.where(sub_b == sl - 1, hal, pltpu.roll(main[:sl], sl - 1, 0))
        a_prev = jnp.concatenate([first_prev, main[:tm - sl]], axis=0)
        a_next = jnp.concatenate([main[sl:], last_next], axis=0)
        return a_prev * taps[0:1] + main * taps[1:2] + a_next * taps[2:3] + taps[3:4]

    def act(cb, va, ga):
        val = conv(va, cw_ref[:, cb * FF_BLK:(cb + 1) * FF_BLK])
        gate = conv(ga, cw_ref[:, D_FF + cb * FF_BLK:D_FF + (cb + 1) * FF_BLK])
        y_ref[:, cb * FF_BLK:(cb + 1) * FF_BLK] = (jax.nn.gelu(gate, approximate=True) * val).astype(BF16)

    pending = proj(0)
    for cb in range(nb):
        nxt = proj(cb + 1) if cb + 1 < nb else None
        act(cb, *pending)
        pending = nxt
    r = _rms(jnp.dot(y_ref[...], wout_ref[...], preferred_element_type=F32), g2_ref[...])
    o_ref[0] = h_ref[...] + restride(r, False)


def _ffn(x, o_na, o_gl, x_meta, o_na_m, o_gl_m, wo, g0, g1, win, cw, wout, g2, tm):
    b, t, d = x.shape
    n_tiles = t // tm
    hb = tm // HALO_ROWS
    last = t // HALO_ROWS - 1
    steps = WEIGHT_CAST_STEPS
    assert all(w.shape[0] % (steps * 2 * SUBLANES) == 0 for w in (wo, win, wout))

    def with_halo(width):
        return [
            pl.BlockSpec((1, tm, width), lambda bi, ti: (bi, ti, 0)),
            pl.BlockSpec((1, HALO_ROWS, width), lambda bi, ti: (bi, jnp.maximum(ti * hb - 1, 0), 0)),
            pl.BlockSpec((1, HALO_ROWS, width), lambda bi, ti: (bi, jnp.minimum((ti + 1) * hb, last), 0)),
        ]

    return pl.pallas_call(
        functools.partial(_ffn_body, tm=tm, n_tiles=n_tiles),
        grid=(b, n_tiles),
        in_specs=with_halo(d) + with_halo(NA_WIDTH) + with_halo(GLA_VW) + [
            _const_spec(x_meta.shape),
            _const_spec(o_na_m.shape),
            pl.BlockSpec((1, N_META, GLA_VW), lambda bi, ti: (bi, 0, 0)),
            pl.BlockSpec(memory_space=pl.ANY),
            _const_spec(g0.shape),
            _const_spec(g1.shape),
            pl.BlockSpec(memory_space=pl.ANY),
            _const_spec(cw.shape),
            pl.BlockSpec(memory_space=pl.ANY),
            _const_spec(g2.shape),
        ],
        out_specs=pl.BlockSpec((1, tm, d), lambda bi, ti: (bi, ti, 0)),
        out_shape=jax.ShapeDtypeStruct((b, t, d), F32),
        scratch_shapes=[pltpu.VMEM((tm, D_FF), BF16),
                        pltpu.VMEM((d // LANES, tm + SUBLANES * PERM_PITCH_PAD, LANES), F32),
                        pltpu.VMEM((tm, d), F32),
                        pltpu.VMEM(wo.shape, BF16), pltpu.VMEM(win.shape, BF16), pltpu.VMEM(wout.shape, BF16),
                        pltpu.VMEM((2, wo.shape[0] // steps, wo.shape[1]), F32),
                        pltpu.VMEM((2, win.shape[0] // steps, win.shape[1]), F32),
                        pltpu.VMEM((2, wout.shape[0] // steps, wout.shape[1]), F32),
                        pltpu.SemaphoreType.DMA((3, 2))],
        compiler_params=pltpu.CompilerParams(
            dimension_semantics=("arbitrary", "arbitrary"), vmem_limit_bytes=VMEM_LIMIT),
        name="ffn",
    )(x, x, x, o_na, o_na, o_na, o_gl, o_gl, o_gl, x_meta, o_na_m, o_gl_m, wo, g0, g1, win, cw, wout, g2)


def _cast_body(*refs):
    n = len(refs) // 2
    for src, dst in zip(refs[:n], refs[n:]):
        dst[...] = src[...].astype(BF16)


def _cast_inproj_body(w_ref, o_ref):
    rows = w_ref.shape[0]
    col = pl.program_id(0) * rows + lax.broadcasted_iota(jnp.int32, (rows, 1), 0)
    o_ref[...] = (w_ref[...] * jnp.where(col < NA_WIDTH, NA_Q_SCALE, 1.0)).astype(BF16)


def _cast_weights(body, ws, steps, name):
    def rows_of(w):
        return pl.BlockSpec((w.shape[0] // steps, w.shape[1]), lambda i: (i, 0))

    assert all(w.shape[0] % (steps * 2 * SUBLANES) == 0 for w in ws)
    return pl.pallas_call(
        body,
        grid=(steps,),
        in_specs=[rows_of(w) for w in ws],
        out_specs=[rows_of(w) for w in ws],
        out_shape=[jax.ShapeDtypeStruct(w.shape, BF16) for w in ws],
        name=name,
    )(*ws)


def kernel(x, meta_tokens, norm_mix_pre, w_in, na_rel_bias, na_out_gain, gla_gate_up_fwd,
           gla_gate_bias_fwd, gla_gate_up_bwd, gla_gate_bias_bwd, gla_out_gain, w_o, norm_mix_post,
           norm_ffn_pre, w_ffn_in, ffn_conv_w, ffn_conv_b, w_ffn_out, norm_ffn_post):
    b, t, d = x.shape
    depth = w_in.shape[0]
    assert depth == 1, "meta rows are only carried as far as a single layer needs them"
    assert t % GRID_W == 0 and t // GRID_W >= NA_WIN_ROWS and N_META == 2 * SUBLANES
    l = 0
    row = lambda a: a[l].reshape(1, -1).astype(F32)

    (w_in_t,) = _cast_weights(_cast_inproj_body, (jnp.swapaxes(w_in[l], 0, 1),), 2, "cast_inproj")
    wo, win, wout = w_o[l], w_ffn_in[l], w_ffn_out[l]
    cw =jnp.concatenate([ffn_conv_w[l], ffn_conv_b[l][None]], axis=0).astype(F32)
    zpad = jnp.zeros((GLA_GATE_RANK, GLA_KW), BF16)
    gate_up = jnp.concatenate(
        [jnp.concatenate([gla_gate_up_fwd[l].astype(BF16), zpad], axis=1),
         jnp.concatenate([zpad, gla_gate_up_bwd[l].astype(BF16)], axis=1)], axis=0)
    gate_bias = jnp.concatenate([row(gla_gate_bias_fwd), row(gla_gate_bias_bwd)], axis=1)

    x2 = x.reshape(b * t, d)
    g_pre = row(norm_mix_pre)
    na_x, gl_x = _inproj(x2, g_pre, w_in_t, 1024, 2)
    na_m, gl_m = _inproj(meta_tokens.astype(F32), g_pre, w_in_t, N_META, 1)
    na_x = na_x.reshape(b, t, NA_COLS)
    gl_x = gl_x.reshape(b, t, GLA_COLS)

    na_gain = row(na_out_gain)
    o_na, o_na_m = _na(na_x, na_m, na_rel_bias[l], na_gain, 8)
    o_gl, o_gl_m = _gla(gl_x, gl_m, gate_up, gate_bias, row(gla_out_gain))

    g_post = row(norm_mix_post)
    assert N_META == HALO_ROWS
    return _ffn(x, o_na, o_gl, meta_tokens.astype(F32), o_na_m, o_gl_m, wo, g_post, row(norm_ffn_pre), win, cw, wout,
                row(norm_ffn_post), 512)
```

```python
import functools

import jax
import jax.numpy as jnp
from jax import lax
from jax.experimental import pallas as pl
from jax.experimental.pallas import tpu as pltpu

F32 = jnp.float32
BF16 = jnp.bfloat16

N_META = 16
GRID_W = 64
NA_WIN_ROWS = 8
NA_WIN_COLS = 16
NA_HEADS = 8
NA_HEAD_DIM = 64
NA_WIDTH = NA_HEADS * NA_HEAD_DIM
GLA_HEADS = 4
GLA_DK = 64
GLA_DV = 128
GLA_KW = GLA_HEADS * GLA_DK
GLA_VW = GLA_HEADS * GLA_DV
GLA_GATE_RANK = 16
GLA_GATE_TAU = 16.0
GLA_CHUNK = 64
GLA_PAD = (-N_META) % GLA_CHUNK
NA_COLS = 3 * NA_WIDTH
GLA_COLS = 2 * GLA_KW + 2 * GLA_VW + 2 * GLA_GATE_RANK
D_FF = 2816
FF_BLK = 256
PERM_PITCH_PAD = 8
CONV_W = 3
RMS_EPS = 1e-6
MASK_NEG = -1e30
LOG2E = 1.4426950408889634
NA_Q_SCALE = NA_HEAD_DIM ** -0.5 * LOG2E
WEIGHT_CAST_STEPS = 8
NA_SKEW = 1

LANES = 128
SUBLANES = 8
HALO_ROWS = 16
VMEM_LIMIT = 56 * 1024 * 1024

_CONTRACT_LAST = (((1,), (1,)), ((), ()))
_CONTRACT_FIRST = (((0,), (0,)), ((), ()))


def _rms(x, g):
    return x * lax.rsqrt(jnp.mean(x * x, axis=-1, keepdims=True) + RMS_EPS) * g


def _aligned(v, m):
    return v if isinstance(v, int) else pl.multiple_of(v, m)


def _interleave(gens, skew, newest_first=False):
    live = [True] * len(gens)
    tick = 0
    order = list(range(len(gens)))
    if newest_first:
        order.reverse()
    while any(live):
        for u in order:
            if live[u] and tick >= u * skew:
                try:
                    next(gens[u])
                except StopIteration:
                    live[u] = False
        tick += 1


def _stream_cast(src_hbm, dst, stage, sems, chunks, scale_rows=None):
    rows = src_hbm.shape[0] // chunks

    def copy(c):
        slot = c % 2
        return pltpu.make_async_copy(src_hbm.at[pl.ds(c * rows, rows)], stage.at[slot], sems.at[slot])

    copy(0).start()
    for c in range(chunks):
        if c + 1 < chunks:
            copy(c + 1).start()
        copy(c).wait()
        w = stage[c % 2]
        if scale_rows is not None:
            w = w * scale_rows(c * rows, rows)
        dst[c * rows:(c + 1) * rows] = w.astype(BF16)


def _const_spec(shape):
    nd = len(shape)
    return pl.BlockSpec(shape, lambda *_: (0,) * nd)


def _inproj_body(x_ref, g_ref, wt_ref, na_ref, gl_ref, *, parts):
    g = g_ref[...]
    pm = x_ref.shape[0] // parts
    u = _rms(x_ref[0:pm], g).astype(BF16)
    for p in range(parts):
        rows = slice(p * pm, (p + 1) * pm)
        na_ref[rows] = lax.dot_general(u, wt_ref[:NA_COLS], _CONTRACT_LAST,
                                       preferred_element_type=F32).astype(BF16)
        u_next = _rms(x_ref[(p + 1) * pm:(p + 2) * pm], g).astype(BF16) if p + 1 < parts else None
        gl_ref[rows] = lax.dot_general(u, wt_ref[NA_COLS:], _CONTRACT_LAST,
                                       preferred_element_type=F32).astype(BF16)
        u = u_next


def _inproj(x2, g, w_in_t, tm, parts):
    rows, d = x2.shape
    single = pl.Buffered(1)
    return pl.pallas_call(
        functools.partial(_inproj_body, parts=parts),
        grid=(rows // tm,),
        in_specs=[
            pl.BlockSpec((tm, d), lambda i: (i, 0)),
            _const_spec(g.shape),
            pl.BlockSpec(w_in_t.shape, lambda i: (0, 0), pipeline_mode=single),
        ],
        out_specs=[
            pl.BlockSpec((tm, NA_COLS), lambda i: (i, 0)),
            pl.BlockSpec((tm, GLA_COLS), lambda i: (i, 0)),
        ],
        out_shape=[
            jax.ShapeDtypeStruct((rows, NA_COLS), BF16),
            jax.ShapeDtypeStruct((rows, GLA_COLS), BF16),
        ],
        compiler_params=pltpu.CompilerParams(
            dimension_semantics=("arbitrary",), vmem_limit_bytes=VMEM_LIMIT),
        name="inproj",
    )(x2, g, w_in_t)


def _split_heads_rows(pair, lo):
    zero = jnp.zeros_like(pair)
    return jnp.concatenate([jnp.where(lo, pair, zero), jnp.where(lo, zero, pair)], axis=0)


def _na_meta(q_ref, k_ref, v_ref, gain_ref, o_ref):
    lane = lax.broadcasted_iota(jnp.int32, (N_META, NA_WIDTH), 1)
    q = q_ref[...]
    k = k_ref[...]
    v = v_ref[...]
    om = jnp.zeros((N_META, NA_WIDTH), F32)
    for h in range(NA_HEADS):
        in_head = (lane >= h * NA_HEAD_DIM) & (lane < (h + 1) * NA_HEAD_DIM)
        qh = jnp.where(in_head, q, jnp.zeros_like(q))
        s = lax.dot_general(qh, k, _CONTRACT_LAST, preferred_element_type=F32)
        m = jnp.max(s, axis=-1, keepdims=True)
        pw = jnp.exp2(s - m)
        pw = pw / jnp.sum(pw, axis=-1, keepdims=True)
        oh = jnp.dot(pw.astype(BF16), v, preferred_element_type=F32)
        om = jnp.where(in_head, oh, om)
    o_ref[...] = _rms(om, gain_ref[...]).astype(BF16)


def _na_bias_table(base_ref, o_ref):
    w, kw = GRID_W, NA_WIN_COLS
    cq = lax.broadcasted_iota(jnp.int32, (w, 2 * w), 0)
    kk = lax.broadcasted_iota(jnp.int32, (w, 2 * w), 1) % w
    cs = jnp.clip(cq - kw // 2, 0, w - kw)
    in_win = (kk >= cs) & (kk < cs + kw)
    for h in range(o_ref.shape[0]):
        for e in range(o_ref.shape[1]):
            rows = jnp.broadcast_to(base_ref[h, e:e + 1, :], (w, 2 * w))
            shifted = pltpu.roll(rows, 2 * w - (kw - 1), 1, stride=1, stride_axis=0)
            o_ref[h, e] = jnp.where(in_win, shifted * LOG2E, MASK_NEG)


def _na_body(q_ref, k_ref, v_ref, qm_ref, km_ref, vm_ref, base_ref, gain_ref, o_ref, om_ref, t2_ref,
             *, rq, n_rows):
    j = pl.program_id(1)

    @pl.when(jnp.logical_and(pl.program_id(0) == 0, j == 0))
    def _():
        _na_bias_table(base_ref, t2_ref)
        _na_meta(qm_ref, km_ref, vm_ref, gain_ref, om_ref)

    w = GRID_W
    kh = NA_WIN_ROWS
    lo = lax.broadcasted_iota(jnp.int32, (w, LANES), 1) < NA_HEAD_DIM
    n_pairs = NA_HEADS // 2
    units = [(i, p) for i in range(rq) for p in range(n_pairs)]

    outs = {}

    def unit(i, p):
        r = j * rq + i
        rs = jnp.clip(r - kh // 2, 0, n_rows - kh)
        e0 = rs - r + (NA_WIN_ROWS - 1)
        k0 = pl.multiple_of(rs * w, w)
        cols = slice(p * LANES, (p + 1) * LANES)
        qp = q_ref[0, i * w:(i + 1) * w, cols]
        q2 = _split_heads_rows(qp, lo)
        kw = k_ref[0, pl.ds(k0, kh * w), cols]
        s = lax.dot_general(q2, kw, _CONTRACT_LAST, preferred_element_type=F32)
        bias = jnp.concatenate(
            [jnp.concatenate([t2_ref[2 * p + hh, e0 + 2 * jj] for jj in range(kh // 2)], axis=1)
             for hh in range(2)], axis=0)
        s = s + bias
        sm = lax.dot_general(q2, km_ref[:, cols], _CONTRACT_LAST, preferred_element_type=F32)
        m = jnp.maximum(jnp.max(s, axis=-1, keepdims=True), jnp.max(sm, axis=-1, keepdims=True))
        yield
        pw = jnp.exp2(s - m)
        pm = jnp.exp2(sm - m)
        l = jnp.sum(pw, axis=-1, keepdims=True) + jnp.sum(pm, axis=-1, keepdims=True)
        pw = pw.astype(BF16)
        pm = pm.astype(BF16)
        yield
        o2 = (jnp.dot(pw, v_ref[0, pl.ds(k0, kh * w), cols], preferred_element_type=F32)
              + jnp.dot(pm, vm_ref[:, cols], preferred_element_type=F32))
        o2 = o2 / l
        outs[(i, p)] = jnp.where(lo, o2[:w], o2[w:])

    def finish_row(i):
        ssq = jnp.zeros((w, 1), F32)
        for p in range(n_pairs):
            ssq = ssq + jnp.sum(outs[(i, p)] * outs[(i, p)], axis=-1, keepdims=True)
        inv = lax.rsqrt(ssq * (1.0 / NA_WIDTH) + RMS_EPS)
        for p in range(n_pairs):
            cols = slice(p * LANES, (p + 1) * LANES)
            o_ref[0, i * w:(i + 1) * w, cols] = (outs.pop((i, p)) * inv * gain_ref[:, cols]).astype(BF16)

    _interleave([unit(i, p) for i, p in units], NA_SKEW, newest_first=True)
    for i in range(rq):
        finish_row(i)


def _na(na_x, na_m, rpb, gain, rq):
    b, t, _ = na_x.shape
    n_rows = t // GRID_W
    nw = NA_WIDTH
    h, nr, nc = rpb.shape
    w = GRID_W
    assert 2 * w == LANES and nc <= w
    padded = jnp.pad(rpb.astype(F32), ((0, 0), (0, 0), (0, w - nc)))
    base = jnp.concatenate([padded[:, :-1], padded[:, 1:]], axis=-1)
    return pl.pallas_call(
        functools.partial(_na_body, rq=rq, n_rows=n_rows),
        grid=(b, n_rows // rq),
        in_specs=[
            pl.BlockSpec((1, rq * GRID_W, nw), lambda bi, j: (bi, j, 0)),
            pl.BlockSpec((1, t, nw), lambda bi, j: (bi, 0, 1)),
            pl.BlockSpec((1, t, nw), lambda bi, j: (bi, 0, 2)),
            pl.BlockSpec((N_META, nw), lambda bi, j: (0, 0)),
            pl.BlockSpec((N_META, nw), lambda bi, j: (0, 1)),
            pl.BlockSpec((N_META, nw), lambda bi, j: (0, 2)),
            _const_spec(base.shape),
            _const_spec(gain.shape),
        ],
        out_specs=[
            pl.BlockSpec((1, rq * GRID_W, nw), lambda bi, j: (bi, j, 0)),
            pl.BlockSpec((N_META, nw), lambda bi, j: (0, 0)),
        ],
        out_shape=[
            jax.ShapeDtypeStruct((b, t, nw), BF16),
            jax.ShapeDtypeStruct((N_META, nw), BF16),
        ],
        scratch_shapes=[pltpu.VMEM((h, nr - 1, w, 2 * w), F32)],
        compiler_params=pltpu.CompilerParams(
            dimension_semantics=("arbitrary", "arbitrary"), vmem_limit_bytes=VMEM_LIMIT),
        name="na",
    )(na_x, na_x, na_x, na_m, na_m, na_m, base, gain)


_GQ, _GK, _GV, _GG, _GZ = 0, GLA_KW, 2 * GLA_KW, 2 * GLA_KW + GLA_VW, 2 * GLA_KW + 2 * GLA_VW


GLA_GROUP = 8
GLA_SKEW = 2


def _gla_body(x_ref, m_ref, up_ref, bias_ref, gain_ref, ox_ref, om_ref, c0, qd, oi, dst, decs, st):
    c = GLA_CHUNK
    kw = GLA_KW
    n_pairs = GLA_HEADS // 2
    t = x_ref.shape[1]
    n_chunks = t // c + 1
    c0[0:GLA_PAD, :] = jnp.zeros((GLA_PAD, GLA_COLS), BF16)
    c0[GLA_PAD:, :] = m_ref[...]

    ti = lax.broadcasted_iota(jnp.int32, (c, c), 0)
    si = lax.broadcasted_iota(jnp.int32, (c, c), 1)
    tri = jnp.where(si <= ti, 1.0, 0.0).astype(BF16)
    lo_c = lax.broadcasted_iota(jnp.int32, (c, LANES), 1) < GLA_DK
    lo_s = lax.broadcasted_iota(jnp.int32, (GLA_DV, 2 * LANES), 1) % LANES < GLA_DK
    t2 = lax.broadcasted_iota(jnp.int32, (2 * c, c), 0) % c
    s2 = lax.broadcasted_iota(jnp.int32, (2 * c, c), 1)
    keep_f = s2 <= t2
    up = up_ref[...]
    bias = bias_ref[...]
    gain = gain_ref[...]

    def reader(n):
        if n is None:
            return lambda a, b: c0[:, a:b]
        r0 = _aligned((n - 1) * c, c)
        return lambda a, b: x_ref[0, pl.ds(r0, c), a:b]

    def cidx(n):
        return 0 if n is None else n

    def prow(n):
        return pl.ds(_aligned(cidx(n) * c, c), c)

    def logsig_decay(gate, n):
        la = (jnp.minimum(gate, 0.0) - jnp.log1p(jnp.exp(-jnp.abs(gate)))) * (1.0 / GLA_GATE_TAU)
        if n is None:
            la = jnp.where(lax.broadcasted_iota(jnp.int32, (c, 1), 0) >= GLA_PAD, la, 0.0)
        hi = la.astype(BF16)
        return la, hi, (la - hi.astype(F32)).astype(BF16)

    def pass_a_chunk(n):
        rd = reader(n)
        gate = jnp.dot(rd(_GZ, _GZ + 2 * GLA_GATE_RANK), up, preferred_element_type=F32) + bias
        yield
        la_f, hi_f, low_f = logsig_decay(gate[:, :kw], n)
        yield
        la_b, hi_b, low_b = logsig_decay(gate[:, kw:], n)
        yield
        cs = jnp.dot(tri, jnp.concatenate([hi_f, hi_b, low_f, low_b], axis=1), preferred_element_type=F32)
        yield
        pre = cs[:, :2 * kw] + cs[:, 2 * kw:]
        b_f = pre[:, :kw]
        bl_f = b_f[c - 1:c]
        bl_b = pre[c - 1:c, kw:]
        b_b = bl_b - pre[:, kw:] + la_b
        q = rd(_GQ, _GQ + kw).astype(F32) * (GLA_DK ** -0.5)
        k = rd(_GK, _GK + kw).astype(F32)
        yield
        qd_f = (q * jnp.exp(b_f)).astype(BF16)
        ki_f = (k * jnp.exp(-b_f)).astype(BF16)
        qd[prow(n), :kw] = qd_f
        yield
        qd_b = (q * jnp.exp(b_b)).astype(BF16)
        ki_b = (k * jnp.exp(-b_b)).astype(BF16)
        qd[prow(n), kw:] = qd_b
        yield
        araw = []
        for p in range(n_pairs):
            cols = slice(p * LANES, (p + 1) * LANES)
            a_f = lax.dot_general(_split_heads_rows(qd_f[:, cols], lo_c), ki_f[:, cols], _CONTRACT_LAST,
                                  preferred_element_type=F32)
            yield
            a_b = lax.dot_general(_split_heads_rows(qd_b[:, cols], lo_c), ki_b[:, cols], _CONTRACT_LAST,
                                  preferred_element_type=F32)
            araw.append((a_f, a_b))
            yield
        ke = jnp.concatenate([k * jnp.exp(bl_f - b_f), k * jnp.exp(bl_b - b_b)], axis=1).astype(BF16)
        dec = jnp.concatenate([jnp.exp(bl_f), jnp.exp(bl_b)], axis=1)
        drow = _aligned(cidx(n) * SUBLANES, SUBLANES)
        decs[pl.ds(drow, SUBLANES), :] = jnp.broadcast_to(dec, (SUBLANES, 2 * kw))
        yield
        for p in range(n_pairs):
            amat = jnp.where(keep_f, araw[p][0], araw[p][1]).astype(BF16)
            kcat = jnp.concatenate([ke[:, p * LANES:(p + 1) * LANES],
                                    ke[:, kw + p * LANES:kw + (p + 1) * LANES]], axis=1)
            incr = []
            for hh in range(2):
                h = 2 * p + hh
                vh = rd(_GV + h * GLA_DV, _GV + (h + 1) * GLA_DV)
                yield
                oi[prow(n), h * GLA_DV:(h + 1) * GLA_DV] = jnp.dot(
                    amat[hh * c:(hh + 1) * c], vh, preferred_element_type=F32)
                yield
                incr.append(lax.dot_general(vh, kcat, _CONTRACT_FIRST, preferred_element_type=F32))
            yield
            dst[cidx(n), p] = jnp.where(lo_s, incr[0], incr[1])

    def pass_c_chunk(n):
        rd = reader(n)
        for p in range(n_pairs):
            qf = qd[prow(n), p * LANES:(p + 1) * LANES]
            qb = qd[prow(n), kw + p * LANES:kw + (p + 1) * LANES]
            q2 = jnp.concatenate([_split_heads_rows(qf, lo_c), _split_heads_rows(qb, lo_c)], axis=1)
            inter = lax.dot_general(q2, dst[cidx(n), p].astype(BF16), _CONTRACT_LAST,
                                    preferred_element_type=F32)
            yield
            for hh in range(2):
                h = 2 * p + hh
                hc = slice(h * GLA_DV, (h + 1) * GLA_DV)
                o = oi[prow(n), hc] + inter[hh * c:(hh + 1) * c]
                g = rd(_GG + h * GLA_DV, _GG + (h + 1) * GLA_DV).astype(F32)
                res = (_rms(o, gain) * (g * jax.nn.sigmoid(g))).astype(BF16)
                if n is None:
                    om_ref[0, :, hc] = res[GLA_PAD:]
                else:
                    ox_ref[0, pl.ds(_aligned((n - 1) * c, c), c), hc] = res
                yield

    def run_group(make_gen, chunks, skew):
        _interleave([make_gen(n) for n in chunks], skew)

    first_group = [None] + list(range(1, GLA_GROUP + 1))
    n_groups = (n_chunks - 1) // GLA_GROUP - 1

    def group_chunks(i):
        return [1 + GLA_GROUP * (i + 1) + u for u in range(GLA_GROUP)]

    run_group(pass_a_chunk, first_group, GLA_SKEW)

    def pass_a(i, carry):
        run_group(pass_a_chunk, group_chunks(i), GLA_SKEW)
        return carry

    lax.fori_loop(0, n_groups, pass_a, 0)

    st[...] = jnp.zeros(st.shape, F32)

    def scan(it, carry):
        for n, lanes, off in ((it, slice(0, LANES), 0), (n_chunks - 1 - it, slice(LANES, 2 * LANES), kw)):
            drow = _aligned(n * SUBLANES, SUBLANES)
            for p in range(n_pairs):
                inc = dst[n, p, :, lanes]
                s_in = st[p, :, lanes]
                dst[n, p, :, lanes] = s_in
                dec = decs[pl.ds(drow, 1), off + p * LANES:off + (p + 1) * LANES]
                st[p, :, lanes] = dec * s_in + inc
        return carry

    lax.fori_loop(0, n_chunks, scan, 0)

    run_group(pass_c_chunk, first_group, 1)

    def pass_c(i, carry):
        run_group(pass_c_chunk, group_chunks(i), 1)
        return carry

    lax.fori_loop(0, n_groups, pass_c, 0)


def _gla(gl_x, gl_m, up, bias, gain):
    b, t, _ = gl_x.shape
    assert (t // GLA_CHUNK) % GLA_GROUP == 0
    n_chunks = t // GLA_CHUNK + 1
    lp = n_chunks * GLA_CHUNK
    n_pairs = GLA_HEADS // 2
    return pl.pallas_call(
        _gla_body,
        grid=(b,),
        in_specs=[
            pl.BlockSpec((1, t, GLA_COLS), lambda bi: (bi, 0, 0)),
            _const_spec(gl_m.shape),
            _const_spec(up.shape),
            _const_spec(bias.shape),
            _const_spec(gain.shape),
        ],
        out_specs=[
            pl.BlockSpec((1, t, GLA_VW), lambda bi: (bi, 0, 0)),
            pl.BlockSpec((1, N_META, GLA_VW), lambda bi: (bi, 0, 0)),
        ],
        out_shape=[
            jax.ShapeDtypeStruct((b, t, GLA_VW), BF16),
            jax.ShapeDtypeStruct((b, N_META, GLA_VW), BF16),
        ],
        scratch_shapes=[
            pltpu.VMEM((GLA_CHUNK, GLA_COLS), BF16),
            pltpu.VMEM((lp, 2 * GLA_KW), BF16),
            pltpu.VMEM((lp, GLA_VW), F32),
            pltpu.VMEM((n_chunks, n_pairs, GLA_DV, 2 * LANES), F32),
            pltpu.VMEM((n_chunks * SUBLANES, 2 * GLA_KW), F32),
            pltpu.VMEM((n_pairs, GLA_DV, 2 * LANES), F32),
        ],
        compiler_params=pltpu.CompilerParams(
            dimension_semantics=("arbitrary",), vmem_limit_bytes=VMEM_LIMIT),
        name="gla",
    )(gl_x, gl_m, up, bias, gain)


def _ffn_body(xm_ref, xp_ref, xn_ref, nam_ref, nap_ref, nan_ref, glm_ref, glp_ref, gln_ref,
              xmeta_ref, nameta_ref, glmeta_ref, wo_hbm, g0_ref, g1_ref, win_hbm, cw_ref, wout_hbm, g2_ref,
              o_ref, y_ref, perm_ref, h_ref, wo_ref, win_ref, wout_ref, st_o, st_i, st_u, sem_o, sem_i, sem_u,
              *, tm, n_tiles):
    t = pl.program_id(1)
    sl = SUBLANES
    nv = tm // sl
    n_col = perm_ref.shape[0]
    hr = HALO_ROWS

    @pl.when(jnp.logical_and(pl.program_id(0) == 0, t == 0))
    def _():
        _stream_cast(wo_hbm, wo_ref, st_o, sem_o, WEIGHT_CAST_STEPS)
        _stream_cast(win_hbm, win_ref, st_i, sem_i, WEIGHT_CAST_STEPS)
        _stream_cast(wout_hbm, wout_ref, st_u, sem_u, WEIGHT_CAST_STEPS)

    first = t == 0
    na = jnp.concatenate([jnp.where(first, nameta_ref[...], nap_ref[0]), nam_ref[0], nan_ref[0]],
                         axis=0)
    gl = jnp.concatenate([jnp.where(first, glmeta_ref[0], glp_ref[0]), glm_ref[0], gln_ref[0]], axis=0)
    xe = jnp.concatenate([jnp.where(first, xmeta_ref[...], xp_ref[0]), xm_ref[0], xn_ref[0]],
                         axis=0)
    mixed = (jnp.dot(na, wo_ref[:NA_WIDTH], preferred_element_type=F32)
             + jnp.dot(gl, wo_ref[NA_WIDTH:], preferred_element_type=F32))
    h1e = xe + _rms(mixed, g0_ref[...])
    h_ref[...] = h1e[hr:hr + tm]
    h_prev = h1e[hr - sl:hr]
    h_next = h1e[hr + tm:hr + tm + sl]

    pitch = perm_ref.shape[1] // sl

    def restride(x, to_permuted):
        groups = []
        if to_permuted:
            for c in range(n_col):
                for s in range(sl):
                    perm_ref[c, s * pitch:s * pitch + nv] = x[s * nv:(s + 1) * nv, c * LANES:(c + 1) * LANES]
        else:
            for c in range(n_col):
                perm_ref[c, 0:tm] = x[:, c * LANES:(c + 1) * LANES]
        for k in range(nv):
            if to_permuted:
                start, stride = k, pitch
            else:
                start, stride = sl * ((sl * k) % nv) + (sl * k) // nv, sl
            groups.append(jnp.concatenate(
                [perm_ref[c, pl.ds(start, sl, stride=stride), :] for c in range(n_col)], axis=1))
        return jnp.concatenate(groups, axis=0)

    g1 = g1_ref[...]
    n2_main = restride(_rms(h_ref[...], g1), True)
    sub = lax.broadcasted_iota(jnp.int32, (sl, 1), 0)
    slab = jnp.where(sub == 0, pltpu.roll(h_prev, 1, 0),
                     jnp.where(sub == sl - 1, pltpu.roll(h_next, sl - 1, 0), 0.0))
    keep = jnp.logical_or(sub < sl - 1, t < n_tiles - 1)
    n2 = jnp.concatenate([n2_main, jnp.where(keep, _rms(slab, g1), 0.0)], axis=0).astype(BF16)

    nb = D_FF // FF_BLK
    sub_b = lax.broadcasted_iota(jnp.int32, (sl, FF_BLK), 0)

    def proj(cb):
        va = jnp.dot(n2, win_ref[:, cb * FF_BLK:(cb + 1) * FF_BLK], preferred_element_type=F32)
        ga = jnp.dot(n2, win_ref[:, D_FF + cb * FF_BLK:D_FF + (cb + 1) * FF_BLK],
                     preferred_element_type=F32)
        return va, ga

    def conv(a, taps):
        main = a[:tm]
        hal = a[tm:]
        first_prev = jnp.where(sub_b == 0, hal, pltpu.roll(main[tm - sl:], 1, 0))
        last_next = jnp.where(sub_b == sl - 1, hal, pltpu.roll(main[:sl], sl - 1, 0))
        a_prev = jnp.concatenate([first_prev, main[:tm - sl]], axis=0)
        a_next = jnp.concatenate([main[sl:], last_next], axis=0)
        return a_prev * taps[0:1] + main * taps[1:2] + a_next * taps[2:3] + taps[3:4]

    def act(cb, va, ga):
        val = conv(va, cw_ref[:, cb * FF_BLK:(cb + 1) * FF_BLK])
        gate = conv(ga, cw_ref[:, D_FF + cb * FF_BLK:D_FF + (cb + 1) * FF_BLK])
        y_ref[:, cb * FF_BLK:(cb + 1) * FF_BLK] = (jax.nn.gelu(gate, approximate=True) * val).astype(BF16)

    pending = proj(0)
    for cb in range(nb):
        nxt = proj(cb + 1) if cb + 1 < nb else None
        act(cb, *pending)
        pending = nxt
    r = _rms(jnp.dot(y_ref[...], wout_ref[...], preferred_element_type=F32), g2_ref[...])
    o_ref[0] = h_ref[...] + restride(r, False)


def _ffn(x, o_na, o_gl, x_meta, o_na_m, o_gl_m, wo, g0, g1, win, cw, wout, g2, tm):
    b, t, d = x.shape
    n_tiles = t // tm
    hb = tm // HALO_ROWS
    last = t // HALO_ROWS - 1
    steps = WEIGHT_CAST_STEPS
    assert all(w.shape[0] % (steps * 2 * SUBLANES) == 0 for w in (wo, win, wout))

    def with_halo(width):
        return [
            pl.BlockSpec((1, tm, width), lambda bi, ti: (bi, ti, 0)),
            pl.BlockSpec((1, HALO_ROWS, width), lambda bi, ti: (bi, jnp.maximum(ti * hb - 1, 0), 0)),
            pl.BlockSpec((1, HALO_ROWS, width), lambda bi, ti: (bi, jnp.minimum((ti + 1) * hb, last), 0)),
        ]

    return pl.pallas_call(
        functools.partial(_ffn_body, tm=tm, n_tiles=n_tiles),
        grid=(b, n_tiles),
        in_specs=with_halo(d) + with_halo(NA_WIDTH) + with_halo(GLA_VW) + [
            _const_spec(x_meta.shape),
            _const_spec(o_na_m.shape),
            pl.BlockSpec((1, N_META, GLA_VW), lambda bi, ti: (bi, 0, 0)),
            pl.BlockSpec(memory_space=pl.ANY),
            _const_spec(g0.shape),
            _const_spec(g1.shape),
            pl.BlockSpec(memory_space=pl.ANY),
            _const_spec(cw.shape),
            pl.BlockSpec(memory_space=pl.ANY),
            _const_spec(g2.shape),
        ],
        out_specs=pl.BlockSpec((1, tm, d), lambda bi, ti: (bi, ti, 0)),
        out_shape=jax.ShapeDtypeStruct((b, t, d), F32),
        scratch_shapes=[pltpu.VMEM((tm, D_FF), BF16),
                        pltpu.VMEM((d // LANES, tm + SUBLANES * PERM_PITCH_PAD, LANES), F32),
                        pltpu.VMEM((tm, d), F32),
                        pltpu.VMEM(wo.shape, BF16), pltpu.VMEM(win.shape, BF16), pltpu.VMEM(wout.shape, BF16),
                        pltpu.VMEM((2, wo.shape[0] // steps, wo.shape[1]), F32),
                        pltpu.VMEM((2, win.shape[0] // steps, win.shape[1]), F32),
                        pltpu.VMEM((2, wout.shape[0] // steps, wout.shape[1]), F32),
                        pltpu.SemaphoreType.DMA((2,)), pltpu.SemaphoreType.DMA((2,)),
                        pltpu.SemaphoreType.DMA((2,))],
        compiler_params=pltpu.CompilerParams(
            dimension_semantics=("arbitrary", "arbitrary"), vmem_limit_bytes=VMEM_LIMIT),
        name="ffn",
    )(x, x, x, o_na, o_na, o_na, o_gl, o_gl, o_gl, x_meta, o_na_m, o_gl_m, wo, g0, g1, win, cw, wout, g2)


def _cast_body(*refs):
    n = len(refs) // 2
    for src, dst in zip(refs[:n], refs[n:]):
        dst[...] = src[...].astype(BF16)


def _cast_inproj_body(w_ref, o_ref):
    rows = w_ref.shape[0]
    col = pl.program_id(0) * rows + lax.broadcasted_iota(jnp.int32, (rows, 1), 0)
    o_ref[...] = (w_ref[...] * jnp.where(col < NA_WIDTH, NA_Q_SCALE, 1.0)).astype(BF16)


def _cast_weights(body, ws, steps, name):
    def rows_of(w):
        return pl.BlockSpec((w.shape[0] // steps, w.shape[1]), lambda i: (i, 0))

    assert all(w.shape[0] % (steps * 2 * SUBLANES) == 0 for w in ws)
    return pl.pallas_call(
        body,
        grid=(steps,),
        in_specs=[rows_of(w) for w in ws],
        out_specs=[rows_of(w) for w in ws],
        out_shape=[jax.ShapeDtypeStruct(w.shape, BF16) for w in ws],
        name=name,
    )(*ws)


def kernel(x, meta_tokens, norm_mix_pre, w_in, na_rel_bias, na_out_gain, gla_gate_up_fwd,
           gla_gate_bias_fwd, gla_gate_up_bwd, gla_gate_bias_bwd, gla_out_gain, w_o, norm_mix_post,
           norm_ffn_pre, w_ffn_in, ffn_conv_w, ffn_conv_b, w_ffn_out, norm_ffn_post):
    b, t, d = x.shape
    depth = w_in.shape[0]
    assert depth == 1, "meta rows are only carried as far as a single layer needs them"
    assert t % GRID_W == 0 and t // GRID_W >= NA_WIN_ROWS and N_META == 2 * SUBLANES
    l = 0
    row = lambda a: a[l].reshape(1, -1).astype(F32)

    (w_in_t,) = _cast_weights(_cast_inproj_body, (jnp.swapaxes(w_in[l], 0, 1),), 2, "cast_inproj")
    wo, win, wout = w_o[l], w_ffn_in[l], w_ffn_out[l]
    cw =jnp.concatenate([ffn_conv_w[l], ffn_conv_b[l][None]], axis=0).astype(F32)
    zpad = jnp.zeros((GLA_GATE_RANK, GLA_KW), BF16)
    gate_up = jnp.concatenate(
        [jnp.concatenate([gla_gate_up_fwd[l].astype(BF16), zpad], axis=1),
         jnp.concatenate([zpad, gla_gate_up_bwd[l].astype(BF16)], axis=1)], axis=0)
    gate_bias = jnp.concatenate([row(gla_gate_bias_fwd), row(gla_gate_bias_bwd)], axis=1)

    x2 = x.reshape(b * t, d)
    g_pre = row(norm_mix_pre)
    na_x, gl_x = _inproj(x2, g_pre, w_in_t, 1024, 2)
    na_m, gl_m = _inproj(meta_tokens.astype(F32), g_pre, w_in_t, N_META, 1)
    na_x = na_x.reshape(b, t, NA_COLS)
    gl_x = gl_x.reshape(b, t, GLA_COLS)

    na_gain = row(na_out_gain)
    o_na, o_na_m = _na(na_x, na_m, na_rel_bias[l], na_gain, 8)
    o_gl, o_gl_m = _gla(gl_x, gl_m, gate_up, gate_bias, row(gla_out_gain))

    g_post = row(norm_mix_post)
    assert N_META == HALO_ROWS
    return _ffn(x, o_na, o_gl, meta_tokens.astype(F32), o_na_m, o_gl_m, wo, g_post, row(norm_ffn_pre), win, cw, wout,
                row(norm_ffn_post), 512)
```

```python
import functools

import jax
import jax.numpy as jnp
from jax import lax
from jax.experimental import pallas as pl
from jax.experimental.pallas import tpu as pltpu

F32 = jnp.float32
BF16 = jnp.bfloat16

N_META = 16
GRID_W = 64
NA_WIN_ROWS = 8
NA_WIN_COLS = 16
NA_HEADS = 8
NA_HEAD_DIM = 64
NA_WIDTH = NA_HEADS * NA_HEAD_DIM
GLA_HEADS = 4
GLA_DK = 64
GLA_DV = 128
GLA_KW = GLA_HEADS * GLA_DK
GLA_VW = GLA_HEADS * GLA_DV
GLA_GATE_RANK = 16
GLA_GATE_TAU = 16.0
GLA_CHUNK = 64
GLA_PAD = (-N_META) % GLA_CHUNK
NA_COLS = 3 * NA_WIDTH
GLA_COLS = 2 * GLA_KW + 2 * GLA_VW + 2 * GLA_GATE_RANK
D_FF = 2816
FF_BLK = 256
PERM_PITCH_PAD = 8
CONV_W = 3
RMS_EPS = 1e-6
MASK_NEG = -1e30
LOG2E = 1.4426950408889634
NA_Q_SCALE = NA_HEAD_DIM ** -0.5 * LOG2E
WEIGHT_CAST_STEPS = 8
NA_SKEW = 1

LANES = 128
SUBLANES = 8
HALO_ROWS = 16
VMEM_LIMIT = 56 * 1024 * 1024

_CONTRACT_LAST = (((1,), (1,)), ((), ()))
_CONTRACT_FIRST = (((0,), (0,)), ((), ()))


def _rms(x, g):
    return x * lax.rsqrt(jnp.mean(x * x, axis=-1, keepdims=True) + RMS_EPS) * g


def _aligned(v, m):
    return v if isinstance(v, int) else pl.multiple_of(v, m)


def _interleave(gens, skew, newest_first=False):
    live = [True] * len(gens)
    tick = 0
    order = list(range(len(gens)))
    if newest_first:
        order.reverse()
    while any(live):
        for u in order:
            if live[u] and tick >= u * skew:
                try:
                    next(gens[u])
                except StopIteration:
                    live[u] = False
        tick += 1


def _row_chunks(rows, max_rows):
    tile = 2 * SUBLANES
    n = -(-rows // max_rows)
    size = -(-rows // (n * tile)) * tile
    return [min(i * size, rows) for i in range(n + 1)]


def _stream_cast(src_hbm, dst, stage, sems, bounds, scale_rows=None):
    def copy(c):
        n = bounds[c + 1] - bounds[c]
        return pltpu.make_async_copy(src_hbm.at[pl.ds(bounds[c], n)], stage.at[c % 2, pl.ds(0, n)],
                                     sems.at[c % 2])

    chunks = len(bounds) - 1
    copy(0).start()
    for c in range(chunks):
        if c + 1 < chunks:
            copy(c + 1).start()
        copy(c).wait()
        n = bounds[c + 1] - bounds[c]
        w = stage[c % 2, 0:n]
        if scale_rows is not None:
            w = w * scale_rows(bounds[c], n)
        dst[bounds[c]:bounds[c + 1]] = w.astype(BF16)


def _const_spec(shape):
    nd = len(shape)
    return pl.BlockSpec(shape, lambda *_: (0,) * nd)


def _inproj_body(x_ref, xm_ref, g_ref, wt_hbm, na_ref, gl_ref, nam_ref, glm_ref, wt_ref, stage, sems, *, parts):
    @pl.when(pl.program_id(0) == 0)
    def _():
        def q_scale(r0, n):
            row = r0 + lax.broadcasted_iota(jnp.int32, (n, 1), 0)
            return jnp.where(row < NA_WIDTH, NA_Q_SCALE, 1.0)

        _stream_cast(wt_hbm, wt_ref, stage, sems, _row_chunks(wt_hbm.shape[0], stage.shape[1]), q_scale)

    g = g_ref[...]
    pm = x_ref.shape[0] // parts
    u = _rms(x_ref[0:pm], g).astype(BF16)
    for p in range(parts):
        rows = slice(p * pm, (p + 1) * pm)
        last = p + 1 == parts
        if last:
            u = jnp.concatenate([u, _rms(xm_ref[...], g).astype(BF16)], axis=0)
        na = lax.dot_general(u, wt_ref[:NA_COLS], _CONTRACT_LAST, preferred_element_type=F32).astype(BF16)
        u_next = None if last else _rms(x_ref[(p + 1) * pm:(p + 2) * pm], g).astype(BF16)
        gl = lax.dot_general(u, wt_ref[NA_COLS:], _CONTRACT_LAST, preferred_element_type=F32).astype(BF16)
        na_ref[rows] = na[:pm]
        gl_ref[rows] = gl[:pm]
        if last:
            nam_ref[...] = na[pm:]
            glm_ref[...] = gl[pm:]
        u = u_next


def _inproj(x2, x_meta, g, w_in_t, tm, parts):
    rows, d = x2.shape
    n_meta = x_meta.shape[0]
    stage_rows = _row_chunks(w_in_t.shape[0], w_in_t.shape[0] // 4)[1]
    return pl.pallas_call(
        functools.partial(_inproj_body, parts=parts),
        grid=(rows // tm,),
        in_specs=[
            pl.BlockSpec((tm, d), lambda i: (i, 0)),
            _const_spec(x_meta.shape),
            _const_spec(g.shape),
            pl.BlockSpec(memory_space=pl.ANY),
        ],
        out_specs=[
            pl.BlockSpec((tm, NA_COLS), lambda i: (i, 0)),
            pl.BlockSpec((tm, GLA_COLS), lambda i: (i, 0)),
            pl.BlockSpec((n_meta, NA_COLS), lambda i: (0, 0)),
            pl.BlockSpec((n_meta, GLA_COLS), lambda i: (0, 0)),
        ],
        out_shape=[
            jax.ShapeDtypeStruct((rows, NA_COLS), BF16),
            jax.ShapeDtypeStruct((rows, GLA_COLS), BF16),
            jax.ShapeDtypeStruct((n_meta, NA_COLS), BF16),
            jax.ShapeDtypeStruct((n_meta, GLA_COLS), BF16),
        ],
        scratch_shapes=[pltpu.VMEM(w_in_t.shape, BF16), pltpu.VMEM((2, stage_rows, d), F32),
                        pltpu.SemaphoreType.DMA((2,))],
        compiler_params=pltpu.CompilerParams(
            dimension_semantics=("arbitrary",), vmem_limit_bytes=VMEM_LIMIT),
        name="inproj",
    )(x2, x_meta, g, w_in_t)


def _split_heads_rows(pair, lo):
    zero = jnp.zeros_like(pair)
    return jnp.concatenate([jnp.where(lo, pair, zero), jnp.where(lo, zero, pair)], axis=0)


def _na_meta(q_ref, k_ref, v_ref, gain_ref, o_ref):
    lane = lax.broadcasted_iota(jnp.int32, (N_META, NA_WIDTH), 1)
    q = q_ref[...]
    k = k_ref[...]
    v = v_ref[...]
    om = jnp.zeros((N_META, NA_WIDTH), F32)
    for h in range(NA_HEADS):
        in_head = (lane >= h * NA_HEAD_DIM) & (lane < (h + 1) * NA_HEAD_DIM)
        qh = jnp.where(in_head, q, jnp.zeros_like(q))
        s = lax.dot_general(qh, k, _CONTRACT_LAST, preferred_element_type=F32)
        m = jnp.max(s, axis=-1, keepdims=True)
        pw = jnp.exp2(s - m)
        pw = pw / jnp.sum(pw, axis=-1, keepdims=True)
        oh = jnp.dot(pw.astype(BF16), v, preferred_element_type=F32)
        om = jnp.where(in_head, oh, om)
    o_ref[...] = _rms(om, gain_ref[...]).astype(BF16)


def _na_bias_table(base_ref, o_ref):
    w, kw = GRID_W, NA_WIN_COLS
    cq = lax.broadcasted_iota(jnp.int32, (w, 2 * w), 0)
    kk = lax.broadcasted_iota(jnp.int32, (w, 2 * w), 1) % w
    cs = jnp.clip(cq - kw // 2, 0, w - kw)
    in_win = (kk >= cs) & (kk < cs + kw)
    for h in range(o_ref.shape[0]):
        for e in range(o_ref.shape[1]):
            rows = jnp.broadcast_to(base_ref[h, e:e + 1, :], (w, 2 * w))
            shifted = pltpu.roll(rows, 2 * w - (kw - 1), 1, stride=1, stride_axis=0)
            o_ref[h, e] = jnp.where(in_win, shifted * LOG2E, MASK_NEG)


def _na_body(q_ref, k_ref, v_ref, qm_ref, km_ref, vm_ref, base_ref, gain_ref, o_ref, om_ref, t2_ref,
             *, rq, n_rows):
    j = pl.program_id(1)

    @pl.when(jnp.logical_and(pl.program_id(0) == 0, j == 0))
    def _():
        _na_bias_table(base_ref, t2_ref)
        _na_meta(qm_ref, km_ref, vm_ref, gain_ref, om_ref)

    w = GRID_W
    kh = NA_WIN_ROWS
    lo = lax.broadcasted_iota(jnp.int32, (w, LANES), 1) < NA_HEAD_DIM
    n_pairs = NA_HEADS // 2
    units = [(i, p) for i in range(rq) for p in range(n_pairs)]

    outs = {}

    def unit(i, p):
        r = j * rq + i
        rs = jnp.clip(r - kh // 2, 0, n_rows - kh)
        e0 = rs - r + (NA_WIN_ROWS - 1)
        k0 = pl.multiple_of(rs * w, w)
        cols = slice(p * LANES, (p + 1) * LANES)
        qp = q_ref[0, i * w:(i + 1) * w, cols]
        q2 = _split_heads_rows(qp, lo)
        kw = k_ref[0, pl.ds(k0, kh * w), cols]
        s = lax.dot_general(q2, kw, _CONTRACT_LAST, preferred_element_type=F32)
        bias = jnp.concatenate(
            [jnp.concatenate([t2_ref[2 * p + hh, e0 + 2 * jj] for jj in range(kh // 2)], axis=1)
             for hh in range(2)], axis=0)
        s = s + bias
        sm = lax.dot_general(q2, km_ref[:, cols], _CONTRACT_LAST, preferred_element_type=F32)
        m = jnp.maximum(jnp.max(s, axis=-1, keepdims=True), jnp.max(sm, axis=-1, keepdims=True))
        yield
        pw = jnp.exp2(s - m)
        pm = jnp.exp2(sm - m)
        l = jnp.sum(pw, axis=-1, keepdims=True) + jnp.sum(pm, axis=-1, keepdims=True)
        pw = pw.astype(BF16)
        pm = pm.astype(BF16)
        yield
        o2 = (jnp.dot(pw, v_ref[0, pl.ds(k0, kh * w), cols], preferred_element_type=F32)
              + jnp.dot(pm, vm_ref[:, cols], preferred_element_type=F32))
        o2 = o2 / l
        outs[(i, p)] = jnp.where(lo, o2[:w], o2[w:])

    def finish_row(i):
        ssq = jnp.zeros((w, 1), F32)
        for p in range(n_pairs):
            ssq = ssq + jnp.sum(outs[(i, p)] * outs[(i, p)], axis=-1, keepdims=True)
        inv = lax.rsqrt(ssq * (1.0 / NA_WIDTH) + RMS_EPS)
        for p in range(n_pairs):
            cols = slice(p * LANES, (p + 1) * LANES)
            o_ref[0, i * w:(i + 1) * w, cols] = (outs.pop((i, p)) * inv * gain_ref[:, cols]).astype(BF16)

    _interleave([unit(i, p) for i, p in units], NA_SKEW, newest_first=True)
    for i in range(rq):
        finish_row(i)


def _na(na_x, na_m, rpb, gain, rq):
    b, t, _ = na_x.shape
    n_rows = t // GRID_W
    nw = NA_WIDTH
    h, nr, nc = rpb.shape
    w = GRID_W
    assert 2 * w == LANES and nc <= w
    padded = jnp.pad(rpb.astype(F32), ((0, 0), (0, 0), (0, w - nc)))
    base = jnp.concatenate([padded[:, :-1], padded[:, 1:]], axis=-1)
    return pl.pallas_call(
        functools.partial(_na_body, rq=rq, n_rows=n_rows),
        grid=(b, n_rows // rq),
        in_specs=[
            pl.BlockSpec((1, rq * GRID_W, nw), lambda bi, j: (bi, j, 0)),
            pl.BlockSpec((1, t, nw), lambda bi, j: (bi, 0, 1)),
            pl.BlockSpec((1, t, nw), lambda bi, j: (bi, 0, 2)),
            pl.BlockSpec((N_META, nw), lambda bi, j: (0, 0)),
            pl.BlockSpec((N_META, nw), lambda bi, j: (0, 1)),
            pl.BlockSpec((N_META, nw), lambda bi, j: (0, 2)),
            _const_spec(base.shape),
            _const_spec(gain.shape),
        ],
        out_specs=[
            pl.BlockSpec((1, rq * GRID_W, nw), lambda bi, j: (bi, j, 0)),
            pl.BlockSpec((N_META, nw), lambda bi, j: (0, 0)),
        ],
        out_shape=[
            jax.ShapeDtypeStruct((b, t, nw), BF16),
            jax.ShapeDtypeStruct((N_META, nw), BF16),
        ],
        scratch_shapes=[pltpu.VMEM((h, nr - 1, w, 2 * w), F32)],
        compiler_params=pltpu.CompilerParams(
            dimension_semantics=("arbitrary", "arbitrary"), vmem_limit_bytes=VMEM_LIMIT),
        name="na",
    )(na_x, na_x, na_x, na_m, na_m, na_m, base, gain)


_GQ, _GK, _GV, _GG, _GZ = 0, GLA_KW, 2 * GLA_KW, 2 * GLA_KW + GLA_VW, 2 * GLA_KW + 2 * GLA_VW


GLA_GROUP = 8
GLA_SKEW = 2


def _gla_body(x_ref, m_ref, up_ref, bias_ref, gain_ref, ox_ref, om_ref, c0, qd, oi, dst, decs, st):
    c = GLA_CHUNK
    kw = GLA_KW
    n_pairs = GLA_HEADS // 2
    t = x_ref.shape[1]
    n_chunks = t // c + 1
    c0[0:GLA_PAD, :] = jnp.zeros((GLA_PAD, GLA_COLS), BF16)
    c0[GLA_PAD:, :] = m_ref[...]

    ti = lax.broadcasted_iota(jnp.int32, (c, c), 0)
    si = lax.broadcasted_iota(jnp.int32, (c, c), 1)
    tri = jnp.where(si <= ti, 1.0, 0.0).astype(BF16)
    lo_c = lax.broadcasted_iota(jnp.int32, (c, LANES), 1) < GLA_DK
    lo_s = lax.broadcasted_iota(jnp.int32, (GLA_DV, 2 * LANES), 1) % LANES < GLA_DK
    t2 = lax.broadcasted_iota(jnp.int32, (2 * c, c), 0) % c
    s2 = lax.broadcasted_iota(jnp.int32, (2 * c, c), 1)
    keep_f = s2 <= t2
    up = up_ref[...]
    bias = bias_ref[...]
    gain = gain_ref[...]

    def reader(n):
        if n is None:
            return lambda a, b: c0[:, a:b]
        r0 = _aligned((n - 1) * c, c)
        return lambda a, b: x_ref[0, pl.ds(r0, c), a:b]

    def cidx(n):
        return 0 if n is None else n

    def prow(n):
        return pl.ds(_aligned(cidx(n) * c, c), c)

    def logsig_decay(gate, n):
        la = (jnp.minimum(gate, 0.0) - jnp.log1p(jnp.exp(-jnp.abs(gate)))) * (1.0 / GLA_GATE_TAU)
        if n is None:
            la = jnp.where(lax.broadcasted_iota(jnp.int32, (c, 1), 0) >= GLA_PAD, la, 0.0)
        hi = la.astype(BF16)
        return la, hi, (la - hi.astype(F32)).astype(BF16)

    def pass_a_chunk(n):
        rd = reader(n)
        gate = jnp.dot(rd(_GZ, _GZ + 2 * GLA_GATE_RANK), up, preferred_element_type=F32) + bias
        yield
        la_f, hi_f, low_f = logsig_decay(gate[:, :kw], n)
        yield
        la_b, hi_b, low_b = logsig_decay(gate[:, kw:], n)
        yield
        cs = jnp.dot(tri, jnp.concatenate([hi_f, hi_b, low_f, low_b], axis=1), preferred_element_type=F32)
        yield
        pre = cs[:, :2 * kw] + cs[:, 2 * kw:]
        b_f = pre[:, :kw]
        bl_f = b_f[c - 1:c]
        bl_b = pre[c - 1:c, kw:]
        b_b = bl_b - pre[:, kw:] + la_b
        q = rd(_GQ, _GQ + kw).astype(F32) * (GLA_DK ** -0.5)
        k = rd(_GK, _GK + kw).astype(F32)
        yield
        qd_f = (q * jnp.exp(b_f)).astype(BF16)
        ki_f = (k * jnp.exp(-b_f)).astype(BF16)
        qd[prow(n), :kw] = qd_f
        yield
        qd_b = (q * jnp.exp(b_b)).astype(BF16)
        ki_b = (k * jnp.exp(-b_b)).astype(BF16)
        qd[prow(n), kw:] = qd_b
        yield
        araw = []
        for p in range(n_pairs):
            cols = slice(p * LANES, (p + 1) * LANES)
            a_f = lax.dot_general(_split_heads_rows(qd_f[:, cols], lo_c), ki_f[:, cols], _CONTRACT_LAST,
                                  preferred_element_type=F32)
            yield
            a_b = lax.dot_general(_split_heads_rows(qd_b[:, cols], lo_c), ki_b[:, cols], _CONTRACT_LAST,
                                  preferred_element_type=F32)
            araw.append((a_f, a_b))
            yield
        ke = jnp.concatenate([k * jnp.exp(bl_f - b_f), k * jnp.exp(bl_b - b_b)], axis=1).astype(BF16)
        dec = jnp.concatenate([jnp.exp(bl_f), jnp.exp(bl_b)], axis=1)
        drow = _aligned(cidx(n) * SUBLANES, SUBLANES)
        decs[pl.ds(drow, SUBLANES), :] = jnp.broadcast_to(dec, (SUBLANES, 2 * kw))
        yield
        for p in range(n_pairs):
            amat = jnp.where(keep_f, araw[p][0], araw[p][1]).astype(BF16)
            kcat = jnp.concatenate([ke[:, p * LANES:(p + 1) * LANES],
                                    ke[:, kw + p * LANES:kw + (p + 1) * LANES]], axis=1)
            incr = []
            for hh in range(2):
                h = 2 * p + hh
                vh = rd(_GV + h * GLA_DV, _GV + (h + 1) * GLA_DV)
                yield
                oi[prow(n), h * GLA_DV:(h + 1) * GLA_DV] = jnp.dot(
                    amat[hh * c:(hh + 1) * c], vh, preferred_element_type=F32)
                yield
                incr.append(lax.dot_general(vh, kcat, _CONTRACT_FIRST, preferred_element_type=F32))
            yield
            dst[cidx(n), p] = jnp.where(lo_s, incr[0], incr[1])

    def pass_c_chunk(n):
        rd = reader(n)
        for p in range(n_pairs):
            qf = qd[prow(n), p * LANES:(p + 1) * LANES]
            qb = qd[prow(n), kw + p * LANES:kw + (p + 1) * LANES]
            q2 = jnp.concatenate([_split_heads_rows(qf, lo_c), _split_heads_rows(qb, lo_c)], axis=1)
            inter = lax.dot_general(q2, dst[cidx(n), p].astype(BF16), _CONTRACT_LAST,
                                    preferred_element_type=F32)
            yield
            for hh in range(2):
                h = 2 * p + hh
                hc = slice(h * GLA_DV, (h + 1) * GLA_DV)
                o = oi[prow(n), hc] + inter[hh * c:(hh + 1) * c]
                g = rd(_GG + h * GLA_DV, _GG + (h + 1) * GLA_DV).astype(F32)
                res = (_rms(o, gain) * (g * jax.nn.sigmoid(g))).astype(BF16)
                if n is None:
                    om_ref[0, :, hc] = res[GLA_PAD:]
                else:
                    ox_ref[0, pl.ds(_aligned((n - 1) * c, c), c), hc] = res
                yield

    def run_group(make_gen, chunks, skew):
        _interleave([make_gen(n) for n in chunks], skew)

    first_group = [None] + list(range(1, GLA_GROUP + 1))
    n_groups = (n_chunks - 1) // GLA_GROUP - 1

    def group_chunks(i):
        return [1 + GLA_GROUP * (i + 1) + u for u in range(GLA_GROUP)]

    run_group(pass_a_chunk, first_group, GLA_SKEW)

    def pass_a(i, carry):
        run_group(pass_a_chunk, group_chunks(i), GLA_SKEW)
        return carry

    lax.fori_loop(0, n_groups, pass_a, 0)

    st[...] = jnp.zeros(st.shape, F32)

    def scan(it, carry):
        for n, lanes, off in ((it, slice(0, LANES), 0), (n_chunks - 1 - it, slice(LANES, 2 * LANES), kw)):
            drow = _aligned(n * SUBLANES, SUBLANES)
            for p in range(n_pairs):
                inc = dst[n, p, :, lanes]
                s_in = st[p, :, lanes]
                dst[n, p, :, lanes] = s_in
                dec = decs[pl.ds(drow, 1), off + p * LANES:off + (p + 1) * LANES]
                st[p, :, lanes] = dec * s_in + inc
        return carry

    lax.fori_loop(0, n_chunks, scan, 0)

    run_group(pass_c_chunk, first_group, 1)

    def pass_c(i, carry):
        run_group(pass_c_chunk, group_chunks(i), 1)
        return carry

    lax.fori_loop(0, n_groups, pass_c, 0)


def _gla(gl_x, gl_m, up, bias, gain):
    b, t, _ = gl_x.shape
    assert (t // GLA_CHUNK) % GLA_GROUP == 0
    n_chunks = t // GLA_CHUNK + 1
    lp = n_chunks * GLA_CHUNK
    n_pairs = GLA_HEADS // 2
    return pl.pallas_call(
        _gla_body,
        grid=(b,),
        in_specs=[
            pl.BlockSpec((1, t, GLA_COLS), lambda bi: (bi, 0, 0)),
            _const_spec(gl_m.shape),
            _const_spec(up.shape),
            _const_spec(bias.shape),
            _const_spec(gain.shape),
        ],
        out_specs=[
            pl.BlockSpec((1, t, GLA_VW), lambda bi: (bi, 0, 0)),
            pl.BlockSpec((1, N_META, GLA_VW), lambda bi: (bi, 0, 0)),
        ],
        out_shape=[
            jax.ShapeDtypeStruct((b, t, GLA_VW), BF16),
            jax.ShapeDtypeStruct((b, N_META, GLA_VW), BF16),
        ],
        scratch_shapes=[
            pltpu.VMEM((GLA_CHUNK, GLA_COLS), BF16),
            pltpu.VMEM((lp, 2 * GLA_KW), BF16),
            pltpu.VMEM((lp, GLA_VW), F32),
            pltpu.VMEM((n_chunks, n_pairs, GLA_DV, 2 * LANES), F32),
            pltpu.VMEM((n_chunks * SUBLANES, 2 * GLA_KW), F32),
            pltpu.VMEM((n_pairs, GLA_DV, 2 * LANES), F32),
        ],
        compiler_params=pltpu.CompilerParams(
            dimension_semantics=("arbitrary",), vmem_limit_bytes=VMEM_LIMIT),
        name="gla",
    )(gl_x, gl_m, up, bias, gain)


def _ffn_body(xm_ref, xp_ref, xn_ref, nam_ref, nap_ref, nan_ref, glm_ref, glp_ref, gln_ref,
              xmeta_ref, nameta_ref, glmeta_ref, wo_hbm, g0_ref, g1_ref, win_hbm, cw_ref, wout_hbm, g2_ref,
              o_ref, y_ref, perm_ref, h_ref, wo_ref, win_ref, wout_ref, st_o, st_i, st_u, sem_o, sem_i, sem_u,
              *, tm, n_tiles):
    t = pl.program_id(1)
    sl = SUBLANES
    nv = tm // sl
    n_col = perm_ref.shape[0]
    hr = HALO_ROWS

    @pl.when(jnp.logical_and(pl.program_id(0) == 0, t == 0))
    def _():
        _stream_cast(wo_hbm, wo_ref, st_o, sem_o, _row_chunks(wo_hbm.shape[0], st_o.shape[1]))
        _stream_cast(win_hbm, win_ref, st_i, sem_i, _row_chunks(win_hbm.shape[0], st_i.shape[1]))
        _stream_cast(wout_hbm, wout_ref, st_u, sem_u, _row_chunks(wout_hbm.shape[0], st_u.shape[1]))

    first = t == 0
    na = jnp.concatenate([jnp.where(first, nameta_ref[...], nap_ref[0]), nam_ref[0], nan_ref[0]],
                         axis=0)
    gl = jnp.concatenate([jnp.where(first, glmeta_ref[0], glp_ref[0]), glm_ref[0], gln_ref[0]], axis=0)
    xe = jnp.concatenate([jnp.where(first, xmeta_ref[...], xp_ref[0]), xm_ref[0], xn_ref[0]],
                         axis=0)
    mixed = (jnp.dot(na, wo_ref[:NA_WIDTH], preferred_element_type=F32)
             + jnp.dot(gl, wo_ref[NA_WIDTH:], preferred_element_type=F32))
    h1e = xe + _rms(mixed, g0_ref[...])
    h_ref[...] = h1e[hr:hr + tm]
    h_prev = h1e[hr - sl:hr]
    h_next = h1e[hr + tm:hr + tm + sl]

    pitch = perm_ref.shape[1] // sl

    def restride(x, to_permuted):
        groups = []
        if to_permuted:
            for c in range(n_col):
                for s in range(sl):
                    perm_ref[c, s * pitch:s * pitch + nv] = x[s * nv:(s + 1) * nv, c * LANES:(c + 1) * LANES]
        else:
            for c in range(n_col):
                perm_ref[c, 0:tm] = x[:, c * LANES:(c + 1) * LANES]
        for k in range(nv):
            if to_permuted:
                start, stride = k, pitch
            else:
                start, stride = sl * ((sl * k) % nv) + (sl * k) // nv, sl
            groups.append(jnp.concatenate(
                [perm_ref[c, pl.ds(start, sl, stride=stride), :] for c in range(n_col)], axis=1))
        return jnp.concatenate(groups, axis=0)

    g1 = g1_ref[...]
    n2_main = restride(_rms(h_ref[...], g1), True)
    sub = lax.broadcasted_iota(jnp.int32, (sl, 1), 0)
    slab = jnp.where(sub == 0, pltpu.roll(h_prev, 1, 0),
                     jnp.where(sub == sl - 1, pltpu.roll(h_next, sl - 1, 0), 0.0))
    keep = jnp.logical_or(sub < sl - 1, t < n_tiles - 1)
    n2 = jnp.concatenate([n2_main, jnp.where(keep, _rms(slab, g1), 0.0)], axis=0).astype(BF16)

    nb = D_FF // FF_BLK
    sub_b = lax.broadcasted_iota(jnp.int32, (sl, FF_BLK), 0)

    def proj(cb):
        va = jnp.dot(n2, win_ref[:, cb * FF_BLK:(cb + 1) * FF_BLK], preferred_element_type=F32)
        ga = jnp.dot(n2, win_ref[:, D_FF + cb * FF_BLK:D_FF + (cb + 1) * FF_BLK],
                     preferred_element_type=F32)
        return va, ga

    def conv(a, taps):
        main = a[:tm]
        hal = a[tm:]
        first_prev = jnp.where(sub_b == 0, hal, pltpu.roll(main[tm - sl:], 1, 0))
        last_next = jnp.where(sub_b == sl - 1, hal, pltpu.roll(main[:sl], sl - 1, 0))
        a_prev = jnp.concatenate([first_prev, main[:tm - sl]], axis=0)
        a_next = jnp.concatenate([main[sl:], last_next], axis=0)
        return a_prev * taps[0:1] + main * taps[1:2] + a_next * taps[2:3] + taps[3:4]

    def act(cb, va, ga):
        val = conv(va, cw_ref[:, cb * FF_BLK:(cb + 1) * FF_BLK])
        gate = conv(ga, cw_ref[:, D_FF + cb * FF_BLK:D_FF + (cb + 1) * FF_BLK])
        y_ref[:, cb * FF_BLK:(cb + 1) * FF_BLK] = (jax.nn.gelu(gate, approximate=True) * val).astype(BF16)

    pending = proj(0)
    for cb in range(nb):
        nxt = proj(cb + 1) if cb + 1 < nb else None
        act(cb, *pending)
        pending = nxt
    r = _rms(jnp.dot(y_ref[...], wout_ref[...], preferred_element_type=F32), g2_ref[...])
    o_ref[0] = h_ref[...] + restride(r, False)


def _ffn(x, o_na, o_gl, x_meta, o_na_m, o_gl_m, wo, g0, g1, win, cw, wout, g2, tm):
    b, t, d = x.shape
    n_tiles = t // tm
    hb = tm // HALO_ROWS
    last = t // HALO_ROWS - 1
    steps = WEIGHT_CAST_STEPS
    assert all(w.shape[0] % (steps * 2 * SUBLANES) == 0 for w in (wo, win, wout))

    def with_halo(width):
        return [
            pl.BlockSpec((1, tm, width), lambda bi, ti: (bi, ti, 0)),
            pl.BlockSpec((1, HALO_ROWS, width), lambda bi, ti: (bi, jnp.maximum(ti * hb - 1, 0), 0)),
            pl.BlockSpec((1, HALO_ROWS, width), lambda bi, ti: (bi, jnp.minimum((ti + 1) * hb, last), 0)),
        ]

    return pl.pallas_call(
        functools.partial(_ffn_body, tm=tm, n_tiles=n_tiles),
        grid=(b, n_tiles),
        in_specs=with_halo(d) + with_halo(NA_WIDTH) + with_halo(GLA_VW) + [
            _const_spec(x_meta.shape),
            _const_spec(o_na_m.shape),
            pl.BlockSpec((1, N_META, GLA_VW), lambda bi, ti: (bi, 0, 0)),
            pl.BlockSpec(memory_space=pl.ANY),
            _const_spec(g0.shape),
            _const_spec(g1.shape),
            pl.BlockSpec(memory_space=pl.ANY),
            _const_spec(cw.shape),
            pl.BlockSpec(memory_space=pl.ANY),
            _const_spec(g2.shape),
        ],
        out_specs=pl.BlockSpec((1, tm, d), lambda bi, ti: (bi, ti, 0)),
        out_shape=jax.ShapeDtypeStruct((b, t, d), F32),
        scratch_shapes=[pltpu.VMEM((tm, D_FF), BF16),
                        pltpu.VMEM((d // LANES, tm + SUBLANES * PERM_PITCH_PAD, LANES), F32),
                        pltpu.VMEM((tm, d), F32),
                        pltpu.VMEM(wo.shape, BF16), pltpu.VMEM(win.shape, BF16), pltpu.VMEM(wout.shape, BF16),
                        pltpu.VMEM((2, wo.shape[0] // steps, wo.shape[1]), F32),
                        pltpu.VMEM((2, win.shape[0] // steps, win.shape[1]), F32),
                        pltpu.VMEM((2, wout.shape[0] // steps, wout.shape[1]), F32),
                        pltpu.SemaphoreType.DMA((2,)), pltpu.SemaphoreType.DMA((2,)),
                        pltpu.SemaphoreType.DMA((2,))],
        compiler_params=pltpu.CompilerParams(
            dimension_semantics=("arbitrary", "arbitrary"), vmem_limit_bytes=VMEM_LIMIT),
        name="ffn",
    )(x, x, x, o_na, o_na, o_na, o_gl, o_gl, o_gl, x_meta, o_na_m, o_gl_m, wo, g0, g1, win, cw, wout, g2)


def kernel(x, meta_tokens, norm_mix_pre, w_in, na_rel_bias, na_out_gain, gla_gate_up_fwd,
           gla_gate_bias_fwd, gla_gate_up_bwd, gla_gate_bias_bwd, gla_out_gain, w_o, norm_mix_post,
           norm_ffn_pre, w_ffn_in, ffn_conv_w, ffn_conv_b, w_ffn_out, norm_ffn_post):
    b, t, d = x.shape
    depth = w_in.shape[0]
    assert depth == 1, "meta rows are only carried as far as a single layer needs them"
    assert t % GRID_W == 0 and t // GRID_W >= NA_WIN_ROWS and N_META == 2 * SUBLANES
    l = 0
    row = lambda a: a[l].reshape(1, -1).astype(F32)

    w_in_t = jnp.swapaxes(w_in[l], 0, 1)
    wo, win, wout = w_o[l], w_ffn_in[l], w_ffn_out[l]
    cw =jnp.concatenate([ffn_conv_w[l], ffn_conv_b[l][None]], axis=0).astype(F32)
    zpad = jnp.zeros((GLA_GATE_RANK, GLA_KW), BF16)
    gate_up = jnp.concatenate(
        [jnp.concatenate([gla_gate_up_fwd[l].astype(BF16), zpad], axis=1),
         jnp.concatenate([zpad, gla_gate_up_bwd[l].astype(BF16)], axis=1)], axis=0)
    gate_bias = jnp.concatenate([row(gla_gate_bias_fwd), row(gla_gate_bias_bwd)], axis=1)

    x2 = x.reshape(b * t, d)
    g_pre = row(norm_mix_pre)
    na_x, gl_x, na_m, gl_m = _inproj(x2, meta_tokens.astype(F32), g_pre, w_in_t, 1024, 2)
    na_x = na_x.reshape(b, t, NA_COLS)
    gl_x = gl_x.reshape(b, t, GLA_COLS)

    na_gain = row(na_out_gain)
    o_na, o_na_m = _na(na_x, na_m, na_rel_bias[l], na_gain, 8)
    o_gl, o_gl_m = _gla(gl_x, gl_m, gate_up, gate_bias, row(gla_out_gain))

    g_post = row(norm_mix_post)
    assert N_META == HALO_ROWS
    return _ffn(x, o_na, o_gl, meta_tokens.astype(F32), o_na_m, o_gl_m, wo, g_post, row(norm_ffn_pre), win, cw, wout,
                row(norm_ffn_post), 512)
```

```python
import functools

import jax
import jax.numpy as jnp
from jax import lax
from jax.experimental import pallas as pl
from jax.experimental.pallas import tpu as pltpu

F32 = jnp.float32
BF16 = jnp.bfloat16

N_META = 16
GRID_W = 64
NA_WIN_ROWS = 8
NA_WIN_COLS = 16
NA_HEADS = 8
NA_HEAD_DIM = 64
NA_WIDTH = NA_HEADS * NA_HEAD_DIM
GLA_HEADS = 4
GLA_DK = 64
GLA_DV = 128
GLA_KW = GLA_HEADS * GLA_DK
GLA_VW = GLA_HEADS * GLA_DV
GLA_GATE_RANK = 16
GLA_GATE_TAU = 16.0
GLA_CHUNK = 64
GLA_PAD = (-N_META) % GLA_CHUNK
NA_COLS = 3 * NA_WIDTH
GLA_COLS = 2 * GLA_KW + 2 * GLA_VW + 2 * GLA_GATE_RANK
D_FF = 2816
FF_BLK = 256
PERM_PITCH_PAD = 8
CONV_W = 3
RMS_EPS = 1e-6
MASK_NEG = -1e30
LOG2E = 1.4426950408889634
NA_Q_SCALE = NA_HEAD_DIM ** -0.5 * LOG2E
WEIGHT_CAST_STEPS = 8
NA_SKEW = 1

LANES = 128
SUBLANES = 8
HALO_ROWS = 16
VMEM_LIMIT = 56 * 1024 * 1024

_CONTRACT_LAST = (((1,), (1,)), ((), ()))
_CONTRACT_FIRST = (((0,), (0,)), ((), ()))


def _rms(x, g):
    return x * lax.rsqrt(jnp.mean(x * x, axis=-1, keepdims=True) + RMS_EPS) * g


def _aligned(v, m):
    return v if isinstance(v, int) else pl.multiple_of(v, m)


def _interleave(gens, skew, newest_first=False):
    live = [True] * len(gens)
    tick = 0
    order = list(range(len(gens)))
    if newest_first:
        order.reverse()
    while any(live):
        for u in order:
            if live[u] and tick >= u * skew:
                try:
                    next(gens[u])
                except StopIteration:
                    live[u] = False
        tick += 1


def _row_chunks(rows, max_rows):
    tile = 2 * SUBLANES
    n = -(-rows // max_rows)
    size = -(-rows // (n * tile)) * tile
    return [min(i * size, rows) for i in range(n + 1)]


def _stream_cast(src_hbm, dst, stage, sems, bounds, scale_rows=None):
    def copy(c):
        n = bounds[c + 1] - bounds[c]
        return pltpu.make_async_copy(src_hbm.at[pl.ds(bounds[c], n)], stage.at[c % 2, pl.ds(0, n)],
                                     sems.at[c % 2])

    chunks = len(bounds) - 1
    copy(0).start()
    for c in range(chunks):
        if c + 1 < chunks:
            copy(c + 1).start()
        copy(c).wait()
        n = bounds[c + 1] - bounds[c]
        w = stage[c % 2, 0:n]
        if scale_rows is not None:
            w = w * scale_rows(bounds[c], n)
        dst[bounds[c]:bounds[c + 1]] = w.astype(BF16)


def _const_spec(shape):
    nd = len(shape)
    return pl.BlockSpec(shape, lambda *_: (0,) * nd)


def _inproj_body(x_ref, xm_ref, g_ref, wt_hbm, na_ref, gl_ref, nam_ref, glm_ref, wt_ref, stage, sems, *, parts):
    @pl.when(pl.program_id(0) == 0)
    def _():
        def q_scale(r0, n):
            row = r0 + lax.broadcasted_iota(jnp.int32, (n, 1), 0)
            return jnp.where(row < NA_WIDTH, NA_Q_SCALE, 1.0)

        _stream_cast(wt_hbm, wt_ref, stage, sems, _row_chunks(wt_hbm.shape[0], stage.shape[1]), q_scale)

    g = g_ref[...]
    pm = x_ref.shape[0] // parts
    u = _rms(x_ref[0:pm], g).astype(BF16)
    for p in range(parts):
        rows = slice(p * pm, (p + 1) * pm)
        last = p + 1 == parts
        if last:
            u = jnp.concatenate([u, _rms(xm_ref[...], g).astype(BF16)], axis=0)
        na = lax.dot_general(u, wt_ref[:NA_COLS], _CONTRACT_LAST, preferred_element_type=F32).astype(BF16)
        u_next = None if last else _rms(x_ref[(p + 1) * pm:(p + 2) * pm], g).astype(BF16)
        gl = lax.dot_general(u, wt_ref[NA_COLS:], _CONTRACT_LAST, preferred_element_type=F32).astype(BF16)
        na_ref[rows] = na[:pm]
        gl_ref[rows] = gl[:pm]
        if last:
            nam_ref[...] = na[pm:]
            glm_ref[...] = gl[pm:]
        u = u_next


def _inproj(x2, x_meta, g, w_in_t, tm, parts):
    rows, d = x2.shape
    n_meta = x_meta.shape[0]
    stage_rows = _row_chunks(w_in_t.shape[0], w_in_t.shape[0] // 4)[1]
    return pl.pallas_call(
        functools.partial(_inproj_body, parts=parts),
        grid=(rows // tm,),
        in_specs=[
            pl.BlockSpec((tm, d), lambda i: (i, 0)),
            _const_spec(x_meta.shape),
            _const_spec(g.shape),
            pl.BlockSpec(memory_space=pl.ANY),
        ],
        out_specs=[
            pl.BlockSpec((tm, NA_COLS), lambda i: (i, 0)),
            pl.BlockSpec((tm, GLA_COLS), lambda i: (i, 0)),
            pl.BlockSpec((n_meta, NA_COLS), lambda i: (0, 0)),
            pl.BlockSpec((n_meta, GLA_COLS), lambda i: (0, 0)),
        ],
        out_shape=[
            jax.ShapeDtypeStruct((rows, NA_COLS), BF16),
            jax.ShapeDtypeStruct((rows, GLA_COLS), BF16),
            jax.ShapeDtypeStruct((n_meta, NA_COLS), BF16),
            jax.ShapeDtypeStruct((n_meta, GLA_COLS), BF16),
        ],
        scratch_shapes=[pltpu.VMEM(w_in_t.shape, BF16), pltpu.VMEM((2, stage_rows, d), F32),
                        pltpu.SemaphoreType.DMA((2,))],
        compiler_params=pltpu.CompilerParams(
            dimension_semantics=("arbitrary",), vmem_limit_bytes=VMEM_LIMIT),
        name="inproj",
    )(x2, x_meta, g, w_in_t)


def _split_heads_rows(pair, lo):
    zero = jnp.zeros_like(pair)
    return jnp.concatenate([jnp.where(lo, pair, zero), jnp.where(lo, zero, pair)], axis=0)


def _na_meta(q_ref, k_ref, v_ref, gain_ref, o_ref):
    lane = lax.broadcasted_iota(jnp.int32, (N_META, NA_WIDTH), 1)
    q = q_ref[...]
    k = k_ref[...]
    v = v_ref[...]
    om = jnp.zeros((N_META, NA_WIDTH), F32)
    for h in range(NA_HEADS):
        in_head = (lane >= h * NA_HEAD_DIM) & (lane < (h + 1) * NA_HEAD_DIM)
        qh = jnp.where(in_head, q, jnp.zeros_like(q))
        s = lax.dot_general(qh, k, _CONTRACT_LAST, preferred_element_type=F32)
        m = jnp.max(s, axis=-1, keepdims=True)
        pw = jnp.exp2(s - m)
        pw = pw / jnp.sum(pw, axis=-1, keepdims=True)
        oh = jnp.dot(pw.astype(BF16), v, preferred_element_type=F32)
        om = jnp.where(in_head, oh, om)
    o_ref[...] = _rms(om, gain_ref[...]).astype(BF16)


def _na_bias_table(base_ref, o_ref):
    w, kw = GRID_W, NA_WIN_COLS
    cq = lax.broadcasted_iota(jnp.int32, (w, 2 * w), 0)
    kk = lax.broadcasted_iota(jnp.int32, (w, 2 * w), 1) % w
    cs = jnp.clip(cq - kw // 2, 0, w - kw)
    in_win = (kk >= cs) & (kk < cs + kw)
    for h in range(o_ref.shape[0]):
        for e in range(o_ref.shape[1]):
            rows = jnp.broadcast_to(base_ref[h, e:e + 1, :], (w, 2 * w))
            shifted = pltpu.roll(rows, 2 * w - (kw - 1), 1, stride=1, stride_axis=0)
            o_ref[h, e] = jnp.where(in_win, shifted * LOG2E, MASK_NEG)


def _na_body(q_ref, k_ref, v_ref, qm_ref, km_ref, vm_ref, base_ref, gain_ref, o_ref, om_ref, t2_ref,
             *, rq, n_rows):
    j = pl.program_id(1)

    @pl.when(jnp.logical_and(pl.program_id(0) == 0, j == 0))
    def _():
        _na_bias_table(base_ref, t2_ref)
        _na_meta(qm_ref, km_ref, vm_ref, gain_ref, om_ref)

    w = GRID_W
    kh = NA_WIN_ROWS
    lo = lax.broadcasted_iota(jnp.int32, (w, LANES), 1) < NA_HEAD_DIM
    n_pairs = NA_HEADS // 2
    units = [(i, p) for i in range(rq) for p in range(n_pairs)]

    outs = {}

    def unit(i, p):
        r = j * rq + i
        rs = jnp.clip(r - kh // 2, 0, n_rows - kh)
        e0 = rs - r + (NA_WIN_ROWS - 1)
        k0 = pl.multiple_of(rs * w, w)
        cols = slice(p * LANES, (p + 1) * LANES)
        qp = q_ref[0, i * w:(i + 1) * w, cols]
        q2 = _split_heads_rows(qp, lo)
        kw = k_ref[0, pl.ds(k0, kh * w), cols]
        s = lax.dot_general(q2, kw, _CONTRACT_LAST, preferred_element_type=F32)
        bias = jnp.concatenate(
            [jnp.concatenate([t2_ref[2 * p + hh, e0 + 2 * jj] for jj in range(kh // 2)], axis=1)
             for hh in range(2)], axis=0)
        s = s + bias
        sm = lax.dot_general(q2, km_ref[:, cols], _CONTRACT_LAST, preferred_element_type=F32)
        m = jnp.maximum(jnp.max(s, axis=-1, keepdims=True), jnp.max(sm, axis=-1, keepdims=True))
        yield
        pw = jnp.exp2(s - m)
        pm = jnp.exp2(sm - m)
        l = jnp.sum(pw, axis=-1, keepdims=True) + jnp.sum(pm, axis=-1, keepdims=True)
        pw = pw.astype(BF16)
        pm = pm.astype(BF16)
        yield
        o2 = (jnp.dot(pw, v_ref[0, pl.ds(k0, kh * w), cols], preferred_element_type=F32)
              + jnp.dot(pm, vm_ref[:, cols], preferred_element_type=F32))
        o2 = o2 / l
        outs[(i, p)] = jnp.where(lo, o2[:w], o2[w:])

    def finish_row(i):
        ssq = jnp.zeros((w, 1), F32)
        for p in range(n_pairs):
            ssq = ssq + jnp.sum(outs[(i, p)] * outs[(i, p)], axis=-1, keepdims=True)
        inv = lax.rsqrt(ssq * (1.0 / NA_WIDTH) + RMS_EPS)
        for p in range(n_pairs):
            cols = slice(p * LANES, (p + 1) * LANES)
            o_ref[0, i * w:(i + 1) * w, cols] = (outs.pop((i, p)) * inv * gain_ref[:, cols]).astype(BF16)

    _interleave([unit(i, p) for i, p in units], NA_SKEW, newest_first=True)
    for i in range(rq):
        finish_row(i)


def _na(na_x, na_m, rpb, gain, rq):
    b, t, _ = na_x.shape
    n_rows = t // GRID_W
    nw = NA_WIDTH
    h, nr, nc = rpb.shape
    w = GRID_W
    assert 2 * w == LANES and nc <= w
    padded = jnp.pad(rpb.astype(F32), ((0, 0), (0, 0), (0, w - nc)))
    base = jnp.concatenate([padded[:, :-1], padded[:, 1:]], axis=-1)
    return pl.pallas_call(
        functools.partial(_na_body, rq=rq, n_rows=n_rows),
        grid=(b, n_rows // rq),
        in_specs=[
            pl.BlockSpec((1, rq * GRID_W, nw), lambda bi, j: (bi, j, 0)),
            pl.BlockSpec((1, t, nw), lambda bi, j: (bi, 0, 1)),
            pl.BlockSpec((1, t, nw), lambda bi, j: (bi, 0, 2)),
            pl.BlockSpec((N_META, nw), lambda bi, j: (0, 0)),
            pl.BlockSpec((N_META, nw), lambda bi, j: (0, 1)),
            pl.BlockSpec((N_META, nw), lambda bi, j: (0, 2)),
            _const_spec(base.shape),
            _const_spec(gain.shape),
        ],
        out_specs=[
            pl.BlockSpec((1, rq * GRID_W, nw), lambda bi, j: (bi, j, 0)),
            pl.BlockSpec((N_META, nw), lambda bi, j: (0, 0)),
        ],
        out_shape=[
            jax.ShapeDtypeStruct((b, t, nw), BF16),
            jax.ShapeDtypeStruct((N_META, nw), BF16),
        ],
        scratch_shapes=[pltpu.VMEM((h, nr - 1, w, 2 * w), F32)],
        compiler_params=pltpu.CompilerParams(
            dimension_semantics=("arbitrary", "arbitrary"), vmem_limit_bytes=VMEM_LIMIT),
        name="na",
    )(na_x, na_x, na_x, na_m, na_m, na_m, base, gain)


_GQ, _GK, _GV, _GG, _GZ = 0, GLA_KW, 2 * GLA_KW, 2 * GLA_KW + GLA_VW, 2 * GLA_KW + 2 * GLA_VW


GLA_GROUP = 16
GLA_SKEW = 2


def _gla_body(x_ref, m_ref, up_ref, bias_ref, gain_ref, ox_ref, om_ref, c0, qd, oi, dst, decs, st):
    c = GLA_CHUNK
    kw = GLA_KW
    n_pairs = GLA_HEADS // 2
    t = x_ref.shape[1]
    n_chunks = t // c + 1
    c0[0:GLA_PAD, :] = jnp.zeros((GLA_PAD, GLA_COLS), BF16)
    c0[GLA_PAD:, :] = m_ref[...]

    ti = lax.broadcasted_iota(jnp.int32, (c, c), 0)
    si = lax.broadcasted_iota(jnp.int32, (c, c), 1)
    tri = jnp.where(si <= ti, 1.0, 0.0).astype(BF16)
    lo_c = lax.broadcasted_iota(jnp.int32, (c, LANES), 1) < GLA_DK
    lo_s = lax.broadcasted_iota(jnp.int32, (GLA_DV, 2 * LANES), 1) % LANES < GLA_DK
    t2 = lax.broadcasted_iota(jnp.int32, (2 * c, c), 0) % c
    s2 = lax.broadcasted_iota(jnp.int32, (2 * c, c), 1)
    keep_f = s2 <= t2
    up = up_ref[...]
    bias = bias_ref[...]
    gain = gain_ref[...]

    def reader(n):
        if n is None:
            return lambda a, b: c0[:, a:b]
        r0 = _aligned((n - 1) * c, c)
        return lambda a, b: x_ref[0, pl.ds(r0, c), a:b]

    def cidx(n):
        return 0 if n is None else n

    def prow(n):
        return pl.ds(_aligned(cidx(n) * c, c), c)

    def logsig_decay(gate, n):
        la = (jnp.minimum(gate, 0.0) - jnp.log1p(jnp.exp(-jnp.abs(gate)))) * (1.0 / GLA_GATE_TAU)
        if n is None:
            la = jnp.where(lax.broadcasted_iota(jnp.int32, (c, 1), 0) >= GLA_PAD, la, 0.0)
        hi = la.astype(BF16)
        return la, hi, (la - hi.astype(F32)).astype(BF16)

    def pass_a_chunk(n):
        rd = reader(n)
        gate = jnp.dot(rd(_GZ, _GZ + 2 * GLA_GATE_RANK), up, preferred_element_type=F32) + bias
        yield
        la_f, hi_f, low_f = logsig_decay(gate[:, :kw], n)
        yield
        la_b, hi_b, low_b = logsig_decay(gate[:, kw:], n)
        yield
        cs = jnp.dot(tri, jnp.concatenate([hi_f, hi_b, low_f, low_b], axis=1), preferred_element_type=F32)
        yield
        pre = cs[:, :2 * kw] + cs[:, 2 * kw:]
        b_f = pre[:, :kw]
        bl_f = b_f[c - 1:c]
        bl_b = pre[c - 1:c, kw:]
        b_b = bl_b - pre[:, kw:] + la_b
        q = rd(_GQ, _GQ + kw).astype(F32) * (GLA_DK ** -0.5)
        k = rd(_GK, _GK + kw).astype(F32)
        yield
        qd_f = (q * jnp.exp(b_f)).astype(BF16)
        ki_f = (k * jnp.exp(-b_f)).astype(BF16)
        qd[prow(n), :kw] = qd_f
        yield
        qd_b = (q * jnp.exp(b_b)).astype(BF16)
        ki_b = (k * jnp.exp(-b_b)).astype(BF16)
        qd[prow(n), kw:] = qd_b
        yield
        araw = []
        for p in range(n_pairs):
            cols = slice(p * LANES, (p + 1) * LANES)
            a_f = lax.dot_general(_split_heads_rows(qd_f[:, cols], lo_c), ki_f[:, cols], _CONTRACT_LAST,
                                  preferred_element_type=F32)
            yield
            a_b = lax.dot_general(_split_heads_rows(qd_b[:, cols], lo_c), ki_b[:, cols], _CONTRACT_LAST,
                                  preferred_element_type=F32)
            araw.append((a_f, a_b))
            yield
        ke = jnp.concatenate([k * jnp.exp(bl_f - b_f), k * jnp.exp(bl_b - b_b)], axis=1).astype(BF16)
        dec = jnp.concatenate([jnp.exp(bl_f), jnp.exp(bl_b)], axis=1)
        drow = _aligned(cidx(n) * SUBLANES, SUBLANES)
        decs[pl.ds(drow, SUBLANES), :] = jnp.broadcast_to(dec, (SUBLANES, 2 * kw))
        yield
        for p in range(n_pairs):
            amat = jnp.where(keep_f, araw[p][0], araw[p][1]).astype(BF16)
            kcat = jnp.concatenate([ke[:, p * LANES:(p + 1) * LANES],
                                    ke[:, kw + p * LANES:kw + (p + 1) * LANES]], axis=1)
            incr = []
            for hh in range(2):
                h = 2 * p + hh
                vh = rd(_GV + h * GLA_DV, _GV + (h + 1) * GLA_DV)
                yield
                oi[prow(n), h * GLA_DV:(h + 1) * GLA_DV] = jnp.dot(
                    amat[hh * c:(hh + 1) * c], vh, preferred_element_type=F32)
                yield
                incr.append(lax.dot_general(vh, kcat, _CONTRACT_FIRST, preferred_element_type=F32))
            yield
            dst[cidx(n), p] = jnp.where(lo_s, incr[0], incr[1])

    def pass_c_chunk(n):
        rd = reader(n)
        for p in range(n_pairs):
            qf = qd[prow(n), p * LANES:(p + 1) * LANES]
            qb = qd[prow(n), kw + p * LANES:kw + (p + 1) * LANES]
            q2 = jnp.concatenate([_split_heads_rows(qf, lo_c), _split_heads_rows(qb, lo_c)], axis=1)
            inter = lax.dot_general(q2, dst[cidx(n), p].astype(BF16), _CONTRACT_LAST,
                                    preferred_element_type=F32)
            yield
            for hh in range(2):
                h = 2 * p + hh
                hc = slice(h * GLA_DV, (h + 1) * GLA_DV)
                o = oi[prow(n), hc] + inter[hh * c:(hh + 1) * c]
                g = rd(_GG + h * GLA_DV, _GG + (h + 1) * GLA_DV).astype(F32)
                res = (_rms(o, gain) * (g * jax.nn.sigmoid(g))).astype(BF16)
                if n is None:
                    om_ref[0, :, hc] = res[GLA_PAD:]
                else:
                    ox_ref[0, pl.ds(_aligned((n - 1) * c, c), c), hc] = res
                yield

    def run_group(make_gen, chunks, skew):
        _interleave([make_gen(n) for n in chunks], skew)

    first_group = [None] + list(range(1, GLA_GROUP + 1))
    n_groups = (n_chunks - 1) // GLA_GROUP - 1

    def group_chunks(i):
        return [1 + GLA_GROUP * (i + 1) + u for u in range(GLA_GROUP)]

    run_group(pass_a_chunk, first_group, GLA_SKEW)

    def pass_a(i, carry):
        run_group(pass_a_chunk, group_chunks(i), GLA_SKEW)
        return carry

    lax.fori_loop(0, n_groups, pass_a, 0)

    st[...] = jnp.zeros(st.shape, F32)

    def scan(it, carry):
        for n, lanes, off in ((it, slice(0, LANES), 0), (n_chunks - 1 - it, slice(LANES, 2 * LANES), kw)):
            drow = _aligned(n * SUBLANES, SUBLANES)
            for p in range(n_pairs):
                inc = dst[n, p, :, lanes]
                s_in = st[p, :, lanes]
                dst[n, p, :, lanes] = s_in
                dec = decs[pl.ds(drow, 1), off + p * LANES:off + (p + 1) * LANES]
                st[p, :, lanes] = dec * s_in + inc
        return carry

    lax.fori_loop(0, n_chunks, scan, 0)

    run_group(pass_c_chunk, first_group, 1)

    def pass_c(i, carry):
        run_group(pass_c_chunk, group_chunks(i), 1)
        return carry

    lax.fori_loop(0, n_groups, pass_c, 0)


def _gla(gl_x, gl_m, up, bias, gain):
    b, t, _ = gl_x.shape
    assert (t // GLA_CHUNK) % GLA_GROUP == 0
    n_chunks = t // GLA_CHUNK + 1
    lp = n_chunks * GLA_CHUNK
    n_pairs = GLA_HEADS // 2
    return pl.pallas_call(
        _gla_body,
        grid=(b,),
        in_specs=[
            pl.BlockSpec((1, t, GLA_COLS), lambda bi: (bi, 0, 0)),
            _const_spec(gl_m.shape),
            _const_spec(up.shape),
            _const_spec(bias.shape),
            _const_spec(gain.shape),
        ],
        out_specs=[
            pl.BlockSpec((1, t, GLA_VW), lambda bi: (bi, 0, 0)),
            pl.BlockSpec((1, N_META, GLA_VW), lambda bi: (bi, 0, 0)),
        ],
        out_shape=[
            jax.ShapeDtypeStruct((b, t, GLA_VW), BF16),
            jax.ShapeDtypeStruct((b, N_META, GLA_VW), BF16),
        ],
        scratch_shapes=[
            pltpu.VMEM((GLA_CHUNK, GLA_COLS), BF16),
            pltpu.VMEM((lp, 2 * GLA_KW), BF16),
            pltpu.VMEM((lp, GLA_VW), F32),
            pltpu.VMEM((n_chunks, n_pairs, GLA_DV, 2 * LANES), F32),
            pltpu.VMEM((n_chunks * SUBLANES, 2 * GLA_KW), F32),
            pltpu.VMEM((n_pairs, GLA_DV, 2 * LANES), F32),
        ],
        compiler_params=pltpu.CompilerParams(
            dimension_semantics=("arbitrary",), vmem_limit_bytes=VMEM_LIMIT),
        name="gla",
    )(gl_x, gl_m, up, bias, gain)


def _ffn_body(xm_ref, xp_ref, xn_ref, nam_ref, nap_ref, nan_ref, glm_ref, glp_ref, gln_ref,
              xmeta_ref, nameta_ref, glmeta_ref, wo_hbm, g0_ref, g1_ref, win_hbm, cw_ref, wout_hbm, g2_ref,
              o_ref, y_ref, perm_ref, h_ref, wo_ref, win_ref, wout_ref, st_o, st_i, st_u, sem_o, sem_i, sem_u,
              *, tm, n_tiles):
    t = pl.program_id(1)
    sl = SUBLANES
    nv = tm // sl
    n_col = perm_ref.shape[0]
    hr = HALO_ROWS

    @pl.when(jnp.logical_and(pl.program_id(0) == 0, t == 0))
    def _():
        _stream_cast(wo_hbm, wo_ref, st_o, sem_o, _row_chunks(wo_hbm.shape[0], st_o.shape[1]))
        _stream_cast(win_hbm, win_ref, st_i, sem_i, _row_chunks(win_hbm.shape[0], st_i.shape[1]))
        _stream_cast(wout_hbm, wout_ref, st_u, sem_u, _row_chunks(wout_hbm.shape[0], st_u.shape[1]))

    first = t == 0
    na = jnp.concatenate([jnp.where(first, nameta_ref[...], nap_ref[0]), nam_ref[0], nan_ref[0]],
                         axis=0)
    gl = jnp.concatenate([jnp.where(first, glmeta_ref[0], glp_ref[0]), glm_ref[0], gln_ref[0]], axis=0)
    xe = jnp.concatenate([jnp.where(first, xmeta_ref[...], xp_ref[0]), xm_ref[0], xn_ref[0]],
                         axis=0)
    mixed = (jnp.dot(na, wo_ref[:NA_WIDTH], preferred_element_type=F32)
             + jnp.dot(gl, wo_ref[NA_WIDTH:], preferred_element_type=F32))
    h1e = xe + _rms(mixed, g0_ref[...])
    h_ref[...] = h1e[hr:hr + tm]
    h_prev = h1e[hr - sl:hr]
    h_next = h1e[hr + tm:hr + tm + sl]

    pitch = perm_ref.shape[1] // sl

    def restride(x, to_permuted):
        groups = []
        if to_permuted:
            for c in range(n_col):
                for s in range(sl):
                    perm_ref[c, s * pitch:s * pitch + nv] = x[s * nv:(s + 1) * nv, c * LANES:(c + 1) * LANES]
        else:
            for c in range(n_col):
                perm_ref[c, 0:tm] = x[:, c * LANES:(c + 1) * LANES]
        for k in range(nv):
            if to_permuted:
                start, stride = k, pitch
            else:
                start, stride = sl * ((sl * k) % nv) + (sl * k) // nv, sl
            groups.append(jnp.concatenate(
                [perm_ref[c, pl.ds(start, sl, stride=stride), :] for c in range(n_col)], axis=1))
        return jnp.concatenate(groups, axis=0)

    g1 = g1_ref[...]
    n2_main = restride(_rms(h_ref[...], g1), True)
    sub = lax.broadcasted_iota(jnp.int32, (sl, 1), 0)
    slab = jnp.where(sub == 0, pltpu.roll(h_prev, 1, 0),
                     jnp.where(sub == sl - 1, pltpu.roll(h_next, sl - 1, 0), 0.0))
    keep = jnp.logical_or(sub < sl - 1, t < n_tiles - 1)
    n2 = jnp.concatenate([n2_main, jnp.where(keep, _rms(slab, g1), 0.0)], axis=0).astype(BF16)

    nb = D_FF // FF_BLK
    sub_b = lax.broadcasted_iota(jnp.int32, (sl, FF_BLK), 0)

    def proj(cb):
        va = jnp.dot(n2, win_ref[:, cb * FF_BLK:(cb + 1) * FF_BLK], preferred_element_type=F32)
        ga = jnp.dot(n2, win_ref[:, D_FF + cb * FF_BLK:D_FF + (cb + 1) * FF_BLK],
                     preferred_element_type=F32)
        return va, ga

    def conv(a, taps):
        main = a[:tm]
        hal = a[tm:]
        first_prev = jnp.where(sub_b == 0, hal, pltpu.roll(main[tm - sl:], 1, 0))
        last_next = jnp.where(sub_b == sl - 1, hal, pltpu.roll(main[:sl], sl - 1, 0))
        a_prev = jnp.concatenate([first_prev, main[:tm - sl]], axis=0)
        a_next = jnp.concatenate([main[sl:], last_next], axis=0)
        return a_prev * taps[0:1] + main * taps[1:2] + a_next * taps[2:3] + taps[3:4]

    def act(cb, va, ga):
        val = conv(va, cw_ref[:, cb * FF_BLK:(cb + 1) * FF_BLK])
        gate = conv(ga, cw_ref[:, D_FF + cb * FF_BLK:D_FF + (cb + 1) * FF_BLK])
        y_ref[:, cb * FF_BLK:(cb + 1) * FF_BLK] = (jax.nn.gelu(gate, approximate=True) * val).astype(BF16)

    pending = proj(0)
    for cb in range(nb):
        nxt = proj(cb + 1) if cb + 1 < nb else None
        act(cb, *pending)
        pending = nxt
    r = _rms(jnp.dot(y_ref[...], wout_ref[...], preferred_element_type=F32), g2_ref[...])
    o_ref[0] = h_ref[...] + restride(r, False)


def _ffn(x, o_na, o_gl, x_meta, o_na_m, o_gl_m, wo, g0, g1, win, cw, wout, g2, tm):
    b, t, d = x.shape
    n_tiles = t // tm
    hb = tm // HALO_ROWS
    last = t // HALO_ROWS - 1
    steps = WEIGHT_CAST_STEPS
    assert all(w.shape[0] % (steps * 2 * SUBLANES) == 0 for w in (wo, win, wout))

    def with_halo(width):
        return [
            pl.BlockSpec((1, tm, width), lambda bi, ti: (bi, ti, 0)),
            pl.BlockSpec((1, HALO_ROWS, width), lambda bi, ti: (bi, jnp.maximum(ti * hb - 1, 0), 0)),
            pl.BlockSpec((1, HALO_ROWS, width), lambda bi, ti: (bi, jnp.minimum((ti + 1) * hb, last), 0)),
        ]

    return pl.pallas_call(
        functools.partial(_ffn_body, tm=tm, n_tiles=n_tiles),
        grid=(b, n_tiles),
        in_specs=with_halo(d) + with_halo(NA_WIDTH) + with_halo(GLA_VW) + [
            _const_spec(x_meta.shape),
            _const_spec(o_na_m.shape),
            pl.BlockSpec((1, N_META, GLA_VW), lambda bi, ti: (bi, 0, 0)),
            pl.BlockSpec(memory_space=pl.ANY),
            _const_spec(g0.shape),
            _const_spec(g1.shape),
            pl.BlockSpec(memory_space=pl.ANY),
            _const_spec(cw.shape),
            pl.BlockSpec(memory_space=pl.ANY),
            _const_spec(g2.shape),
        ],
        out_specs=pl.BlockSpec((1, tm, d), lambda bi, ti: (bi, ti, 0)),
        out_shape=jax.ShapeDtypeStruct((b, t, d), F32),
        scratch_shapes=[pltpu.VMEM((tm, D_FF), BF16),
                        pltpu.VMEM((d // LANES, tm + SUBLANES * PERM_PITCH_PAD, LANES), F32),
                        pltpu.VMEM((tm, d), F32),
                        pltpu.VMEM(wo.shape, BF16), pltpu.VMEM(win.shape, BF16), pltpu.VMEM(wout.shape, BF16),
                        pltpu.VMEM((2, wo.shape[0] // steps, wo.shape[1]), F32),
                        pltpu.VMEM((2, win.shape[0] // steps, win.shape[1]), F32),
                        pltpu.VMEM((2, wout.shape[0] // steps, wout.shape[1]), F32),
                        pltpu.SemaphoreType.DMA((2,)), pltpu.SemaphoreType.DMA((2,)),
                        pltpu.SemaphoreType.DMA((2,))],
        compiler_params=pltpu.CompilerParams(
            dimension_semantics=("arbitrary", "arbitrary"), vmem_limit_bytes=VMEM_LIMIT),
        name="ffn",
    )(x, x, x, o_na, o_na, o_na, o_gl, o_gl, o_gl, x_meta, o_na_m, o_gl_m, wo, g0, g1, win, cw, wout, g2)


def kernel(x, meta_tokens, norm_mix_pre, w_in, na_rel_bias, na_out_gain, gla_gate_up_fwd,
           gla_gate_bias_fwd, gla_gate_up_bwd, gla_gate_bias_bwd, gla_out_gain, w_o, norm_mix_post,
           norm_ffn_pre, w_ffn_in, ffn_conv_w, ffn_conv_b, w_ffn_out, norm_ffn_post):
    b, t, d = x.shape
    depth = w_in.shape[0]
    assert depth == 1, "meta rows are only carried as far as a single layer needs them"
    assert t % GRID_W == 0 and t // GRID_W >= NA_WIN_ROWS and N_META == 2 * SUBLANES
    l = 0
    row = lambda a: a[l].reshape(1, -1).astype(F32)

    w_in_t = jnp.swapaxes(w_in[l], 0, 1)
    wo, win, wout = w_o[l], w_ffn_in[l], w_ffn_out[l]
    cw =jnp.concatenate([ffn_conv_w[l], ffn_conv_b[l][None]], axis=0).astype(F32)
    zpad = jnp.zeros((GLA_GATE_RANK, GLA_KW), BF16)
    gate_up = jnp.concatenate(
        [jnp.concatenate([gla_gate_up_fwd[l].astype(BF16), zpad], axis=1),
         jnp.concatenate([zpad, gla_gate_up_bwd[l].astype(BF16)], axis=1)], axis=0)
    gate_bias = jnp.concatenate([row(gla_gate_bias_fwd), row(gla_gate_bias_bwd)], axis=1)

    x2 = x.reshape(b * t, d)
    g_pre = row(norm_mix_pre)
    na_x, gl_x, na_m, gl_m = _inproj(x2, meta_tokens.astype(F32), g_pre, w_in_t, 1024, 2)
    na_x = na_x.reshape(b, t, NA_COLS)
    gl_x = gl_x.reshape(b, t, GLA_COLS)

    na_gain = row(na_out_gain)
    o_na, o_na_m = _na(na_x, na_m, na_rel_bias[l], na_gain, 16)
    o_gl, o_gl_m = _gla(gl_x, gl_m, gate_up, gate_bias, row(gla_out_gain))

    g_post = row(norm_mix_post)
    assert N_META == HALO_ROWS
    return _ffn(x, o_na, o_gl, meta_tokens.astype(F32), o_na_m, o_gl_m, wo, g_post, row(norm_ffn_pre), win, cw, wout,
                row(norm_ffn_post), 512)
```

```python
import functools

import jax
import jax.numpy as jnp
from jax import lax
from jax.experimental import pallas as pl
from jax.experimental.pallas import tpu as pltpu

F32 = jnp.float32
BF16 = jnp.bfloat16

N_META = 16
GRID_W = 64
NA_WIN_ROWS = 8
NA_WIN_COLS = 16
NA_HEADS = 8
NA_HEAD_DIM = 64
NA_WIDTH = NA_HEADS * NA_HEAD_DIM
GLA_HEADS = 4
GLA_DK = 64
GLA_DV = 128
GLA_KW = GLA_HEADS * GLA_DK
GLA_VW = GLA_HEADS * GLA_DV
GLA_GATE_RANK = 16
GLA_GATE_TAU = 16.0
GLA_CHUNK = 64
GLA_PAD = (-N_META) % GLA_CHUNK
NA_COLS = 3 * NA_WIDTH
GLA_COLS = 2 * GLA_KW + 2 * GLA_VW + 2 * GLA_GATE_RANK
D_FF = 2816
FF_BLK = 256
PERM_PITCH_PAD = 8
CONV_W = 3
RMS_EPS = 1e-6
MASK_NEG = -1e30
LOG2E = 1.4426950408889634
NA_Q_SCALE = NA_HEAD_DIM ** -0.5 * LOG2E
WEIGHT_CAST_STEPS = 8
NA_SKEW = 1

LANES = 128
SUBLANES = 8
HALO_ROWS = 16
VMEM_LIMIT = 56 * 1024 * 1024

_CONTRACT_LAST = (((1,), (1,)), ((), ()))
_CONTRACT_FIRST = (((0,), (0,)), ((), ()))


def _rms(x, g):
    return x * lax.rsqrt(jnp.mean(x * x, axis=-1, keepdims=True) + RMS_EPS) * g


def _aligned(v, m):
    return v if isinstance(v, int) else pl.multiple_of(v, m)


def _interleave(gens, skew, newest_first=False):
    live = [True] * len(gens)
    tick = 0
    order = list(range(len(gens)))
    if newest_first:
        order.reverse()
    while any(live):
        for u in order:
            if live[u] and tick >= u * skew:
                try:
                    next(gens[u])
                except StopIteration:
                    live[u] = False
        tick += 1


def _row_chunks(rows, max_rows):
    tile = 2 * SUBLANES
    n = -(-rows // max_rows)
    size = -(-rows // (n * tile)) * tile
    return [min(i * size, rows) for i in range(n + 1)]


def _stream_cast(src_hbm, dst, stage, sems, bounds, scale_rows=None):
    def copy(c):
        n = bounds[c + 1] - bounds[c]
        return pltpu.make_async_copy(src_hbm.at[pl.ds(bounds[c], n)], stage.at[c % 2, pl.ds(0, n)],
                                     sems.at[c % 2])

    chunks = len(bounds) - 1
    copy(0).start()
    for c in range(chunks):
        if c + 1 < chunks:
            copy(c + 1).start()
        copy(c).wait()
        n = bounds[c + 1] - bounds[c]
        w = stage[c % 2, 0:n]
        if scale_rows is not None:
            w = w * scale_rows(bounds[c], n)
        dst[bounds[c]:bounds[c + 1]] = w.astype(BF16)


def _const_spec(shape):
    nd = len(shape)
    return pl.BlockSpec(shape, lambda *_: (0,) * nd)


def _inproj_body(x_ref, xm_ref, g_ref, wt_hbm, na_ref, gl_ref, nam_ref, glm_ref, wt_ref, stage, sems, *, parts):
    @pl.when(pl.program_id(0) == 0)
    def _():
        def q_scale(r0, n):
            row = r0 + lax.broadcasted_iota(jnp.int32, (n, 1), 0)
            gla_q = jnp.logical_and(row >= NA_COLS, row < NA_COLS + GLA_KW)
            return jnp.where(row < NA_WIDTH, NA_Q_SCALE, jnp.where(gla_q, GLA_DK ** -0.5, 1.0))

        _stream_cast(wt_hbm, wt_ref, stage, sems, _row_chunks(wt_hbm.shape[0], stage.shape[1]), q_scale)

    g = g_ref[...]
    pm = x_ref.shape[0] // parts
    u = _rms(x_ref[0:pm], g).astype(BF16)
    for p in range(parts):
        rows = slice(p * pm, (p + 1) * pm)
        last = p + 1 == parts
        if last:
            u = jnp.concatenate([u, _rms(xm_ref[...], g).astype(BF16)], axis=0)
        na = lax.dot_general(u, wt_ref[:NA_COLS], _CONTRACT_LAST, preferred_element_type=F32).astype(BF16)
        u_next = None if last else _rms(x_ref[(p + 1) * pm:(p + 2) * pm], g).astype(BF16)
        gl = lax.dot_general(u, wt_ref[NA_COLS:], _CONTRACT_LAST, preferred_element_type=F32).astype(BF16)
        na_ref[rows] = na[:pm]
        gl_ref[rows] = gl[:pm]
        if last:
            nam_ref[...] = na[pm:]
            glm_ref[...] = gl[pm:]
        u = u_next


def _inproj(x2, x_meta, g, w_in_t, tm, parts):
    rows, d = x2.shape
    n_meta = x_meta.shape[0]
    stage_rows = _row_chunks(w_in_t.shape[0], w_in_t.shape[0] // 4)[1]
    return pl.pallas_call(
        functools.partial(_inproj_body, parts=parts),
        grid=(rows // tm,),
        in_specs=[
            pl.BlockSpec((tm, d), lambda i: (i, 0)),
            _const_spec(x_meta.shape),
            _const_spec(g.shape),
            pl.BlockSpec(memory_space=pl.ANY),
        ],
        out_specs=[
            pl.BlockSpec((tm, NA_COLS), lambda i: (i, 0)),
            pl.BlockSpec((tm, GLA_COLS), lambda i: (i, 0)),
            pl.BlockSpec((n_meta, NA_COLS), lambda i: (0, 0)),
            pl.BlockSpec((n_meta, GLA_COLS), lambda i: (0, 0)),
        ],
        out_shape=[
            jax.ShapeDtypeStruct((rows, NA_COLS), BF16),
            jax.ShapeDtypeStruct((rows, GLA_COLS), BF16),
            jax.ShapeDtypeStruct((n_meta, NA_COLS), BF16),
            jax.ShapeDtypeStruct((n_meta, GLA_COLS), BF16),
        ],
        scratch_shapes=[pltpu.VMEM(w_in_t.shape, BF16), pltpu.VMEM((2, stage_rows, d), F32),
                        pltpu.SemaphoreType.DMA((2,))],
        compiler_params=pltpu.CompilerParams(
            dimension_semantics=("arbitrary",), vmem_limit_bytes=VMEM_LIMIT),
        name="inproj",
    )(x2, x_meta, g, w_in_t)


def _split_heads_rows(pair, lo):
    zero = jnp.zeros_like(pair)
    return jnp.concatenate([jnp.where(lo, pair, zero), jnp.where(lo, zero, pair)], axis=0)


def _na_meta(q_ref, k_ref, v_ref, gain_ref, o_ref):
    lane = lax.broadcasted_iota(jnp.int32, (N_META, NA_WIDTH), 1)
    q = q_ref[...]
    k = k_ref[...]
    v = v_ref[...]
    om = jnp.zeros((N_META, NA_WIDTH), F32)
    for h in range(NA_HEADS):
        in_head = (lane >= h * NA_HEAD_DIM) & (lane < (h + 1) * NA_HEAD_DIM)
        qh = jnp.where(in_head, q, jnp.zeros_like(q))
        s = lax.dot_general(qh, k, _CONTRACT_LAST, preferred_element_type=F32)
        m = jnp.max(s, axis=-1, keepdims=True)
        pw = jnp.exp2(s - m)
        pw = pw / jnp.sum(pw, axis=-1, keepdims=True)
        oh = jnp.dot(pw.astype(BF16), v, preferred_element_type=F32)
        om = jnp.where(in_head, oh, om)
    o_ref[...] = _rms(om, gain_ref[...]).astype(BF16)


def _na_bias_table(base_ref, o_ref):
    w, kw = GRID_W, NA_WIN_COLS
    cq = lax.broadcasted_iota(jnp.int32, (w, 2 * w), 0)
    kk = lax.broadcasted_iota(jnp.int32, (w, 2 * w), 1) % w
    cs = jnp.clip(cq - kw // 2, 0, w - kw)
    in_win = (kk >= cs) & (kk < cs + kw)
    for h in range(o_ref.shape[0]):
        for e in range(o_ref.shape[1]):
            rows = jnp.broadcast_to(base_ref[h, e:e + 1, :], (w, 2 * w))
            shifted = pltpu.roll(rows, 2 * w - (kw - 1), 1, stride=1, stride_axis=0)
            o_ref[h, e] = jnp.where(in_win, shifted * LOG2E, MASK_NEG)


def _na_body(q_ref, k_ref, v_ref, qm_ref, km_ref, vm_ref, base_ref, gain_ref, o_ref, om_ref, t2_ref,
             *, rq, n_rows):
    j = pl.program_id(1)

    @pl.when(jnp.logical_and(pl.program_id(0) == 0, j == 0))
    def _():
        _na_bias_table(base_ref, t2_ref)
        _na_meta(qm_ref, km_ref, vm_ref, gain_ref, om_ref)

    w = GRID_W
    kh = NA_WIN_ROWS
    lo = lax.broadcasted_iota(jnp.int32, (w, LANES), 1) < NA_HEAD_DIM
    n_pairs = NA_HEADS // 2
    units = [(i, p) for i in range(rq) for p in range(n_pairs)]

    outs = {}

    def unit(i, p):
        r = j * rq + i
        rs = jnp.clip(r - kh // 2, 0, n_rows - kh)
        e0 = rs - r + (NA_WIN_ROWS - 1)
        k0 = pl.multiple_of(rs * w, w)
        cols = slice(p * LANES, (p + 1) * LANES)
        qp = q_ref[0, i * w:(i + 1) * w, cols]
        q2 = _split_heads_rows(qp, lo)
        kw = k_ref[0, pl.ds(k0, kh * w), cols]
        s = lax.dot_general(q2, kw, _CONTRACT_LAST, preferred_element_type=F32)
        bias = jnp.concatenate(
            [jnp.concatenate([t2_ref[2 * p + hh, e0 + 2 * jj] for jj in range(kh // 2)], axis=1)
             for hh in range(2)], axis=0)
        s = s + bias
        sm = lax.dot_general(q2, km_ref[:, cols], _CONTRACT_LAST, preferred_element_type=F32)
        m = jnp.maximum(jnp.max(s, axis=-1, keepdims=True), jnp.max(sm, axis=-1, keepdims=True))
        yield
        pw = jnp.exp2(s - m)
        pm = jnp.exp2(sm - m)
        l = jnp.sum(pw, axis=-1, keepdims=True) + jnp.sum(pm, axis=-1, keepdims=True)
        pw = pw.astype(BF16)
        pm = pm.astype(BF16)
        yield
        o2 = (jnp.dot(pw, v_ref[0, pl.ds(k0, kh * w), cols], preferred_element_type=F32)
              + jnp.dot(pm, vm_ref[:, cols], preferred_element_type=F32))
        o2 = o2 / l
        outs[(i, p)] = jnp.where(lo, o2[:w], o2[w:])

    def finish_row(i):
        ssq = jnp.zeros((w, 1), F32)
        for p in range(n_pairs):
            ssq = ssq + jnp.sum(outs[(i, p)] * outs[(i, p)], axis=-1, keepdims=True)
        inv = lax.rsqrt(ssq * (1.0 / NA_WIDTH) + RMS_EPS)
        for p in range(n_pairs):
            cols = slice(p * LANES, (p + 1) * LANES)
            o_ref[0, i * w:(i + 1) * w, cols] = (outs.pop((i, p)) * inv * gain_ref[:, cols]).astype(BF16)

    def unit_then_row(i, p):
        yield from unit(i, p)
        if p == n_pairs - 1:
            finish_row(i)

    _interleave([unit_then_row(i, p) for i, p in units], NA_SKEW, newest_first=True)


def _na(na_x, na_m, rpb, gain, rq):
    b, t, _ = na_x.shape
    n_rows = t // GRID_W
    nw = NA_WIDTH
    h, nr, nc = rpb.shape
    w = GRID_W
    assert 2 * w == LANES and nc <= w
    padded = jnp.pad(rpb.astype(F32), ((0, 0), (0, 0), (0, w - nc)))
    base = jnp.concatenate([padded[:, :-1], padded[:, 1:]], axis=-1)
    return pl.pallas_call(
        functools.partial(_na_body, rq=rq, n_rows=n_rows),
        grid=(b, n_rows // rq),
        in_specs=[
            pl.BlockSpec((1, rq * GRID_W, nw), lambda bi, j: (bi, j, 0)),
            pl.BlockSpec((1, t, nw), lambda bi, j: (bi, 0, 1)),
            pl.BlockSpec((1, t, nw), lambda bi, j: (bi, 0, 2)),
            pl.BlockSpec((N_META, nw), lambda bi, j: (0, 0)),
            pl.BlockSpec((N_META, nw), lambda bi, j: (0, 1)),
            pl.BlockSpec((N_META, nw), lambda bi, j: (0, 2)),
            _const_spec(base.shape),
            _const_spec(gain.shape),
        ],
        out_specs=[
            pl.BlockSpec((1, rq * GRID_W, nw), lambda bi, j: (bi, j, 0)),
            pl.BlockSpec((N_META, nw), lambda bi, j: (0, 0)),
        ],
        out_shape=[
            jax.ShapeDtypeStruct((b, t, nw), BF16),
            jax.ShapeDtypeStruct((N_META, nw), BF16),
        ],
        scratch_shapes=[pltpu.VMEM((h, nr - 1, w, 2 * w), F32)],
        compiler_params=pltpu.CompilerParams(
            dimension_semantics=("arbitrary", "arbitrary"), vmem_limit_bytes=VMEM_LIMIT),
        name="na",
    )(na_x, na_x, na_x, na_m, na_m, na_m, base, gain)


_GQ, _GK, _GV, _GG, _GZ = 0, GLA_KW, 2 * GLA_KW, 2 * GLA_KW + GLA_VW, 2 * GLA_KW + 2 * GLA_VW


GLA_GROUP = 16
GLA_SKEW = 2


def _gla_body(x_ref, m_ref, up_ref, bias_ref, gain_ref, ox_ref, om_ref, c0, qd, oi, dst, decs, st):
    c = GLA_CHUNK
    kw = GLA_KW
    n_pairs = GLA_HEADS // 2
    t = x_ref.shape[1]
    n_chunks = t // c + 1
    c0[0:GLA_PAD, :] = jnp.zeros((GLA_PAD, GLA_COLS), BF16)
    c0[GLA_PAD:, :] = m_ref[...]

    ti = lax.broadcasted_iota(jnp.int32, (c, c), 0)
    si = lax.broadcasted_iota(jnp.int32, (c, c), 1)
    tri = jnp.where(si <= ti, 1.0, 0.0).astype(BF16)
    lo_c = lax.broadcasted_iota(jnp.int32, (c, LANES), 1) < GLA_DK
    lo_s = lax.broadcasted_iota(jnp.int32, (GLA_DV, 2 * LANES), 1) % LANES < GLA_DK
    t2 = lax.broadcasted_iota(jnp.int32, (2 * c, c), 0) % c
    s2 = lax.broadcasted_iota(jnp.int32, (2 * c, c), 1)
    keep_f = s2 <= t2
    up = up_ref[...]
    bias = bias_ref[...]
    gain = gain_ref[...]

    def reader(n):
        if n is None:
            return lambda a, b: c0[:, a:b]
        r0 = _aligned((n - 1) * c, c)
        return lambda a, b: x_ref[0, pl.ds(r0, c), a:b]

    def cidx(n):
        return 0 if n is None else n

    def prow(n):
        return pl.ds(_aligned(cidx(n) * c, c), c)

    def logsig_decay(gate, n):
        la = (jnp.minimum(gate, 0.0) - jnp.log1p(jnp.exp(-jnp.abs(gate)))) * (LOG2E / GLA_GATE_TAU)
        if n is None:
            la = jnp.where(lax.broadcasted_iota(jnp.int32, (c, 1), 0) >= GLA_PAD, la, 0.0)
        hi = la.astype(BF16)
        return la, hi, (la - hi.astype(F32)).astype(BF16)

    def pass_a_chunk(n):
        rd = reader(n)
        gate = jnp.dot(rd(_GZ, _GZ + 2 * GLA_GATE_RANK), up, preferred_element_type=F32) + bias
        yield
        la_f, hi_f, low_f = logsig_decay(gate[:, :kw], n)
        yield
        la_b, hi_b, low_b = logsig_decay(gate[:, kw:], n)
        yield
        cs = jnp.dot(tri, jnp.concatenate([hi_f, hi_b, low_f, low_b], axis=1), preferred_element_type=F32)
        yield
        pre = cs[:, :2 * kw] + cs[:, 2 * kw:]
        b_f = pre[:, :kw]
        bl_f = b_f[c - 1:c]
        bl_b = pre[c - 1:c, kw:]
        b_b = bl_b - pre[:, kw:] + la_b
        q = rd(_GQ, _GQ + kw).astype(F32)
        k = rd(_GK, _GK + kw).astype(F32)
        yield
        qd_f = (q * jnp.exp2(b_f)).astype(BF16)
        ki_f = (k * jnp.exp2(-b_f)).astype(BF16)
        qd[prow(n), :kw] = qd_f
        yield
        qd_b = (q * jnp.exp2(b_b)).astype(BF16)
        ki_b = (k * jnp.exp2(-b_b)).astype(BF16)
        qd[prow(n), kw:] = qd_b
        yield
        araw = []
        for p in range(n_pairs):
            cols = slice(p * LANES, (p + 1) * LANES)
            a_f = lax.dot_general(_split_heads_rows(qd_f[:, cols], lo_c), ki_f[:, cols], _CONTRACT_LAST,
                                  preferred_element_type=F32)
            yield
            a_b = lax.dot_general(_split_heads_rows(qd_b[:, cols], lo_c), ki_b[:, cols], _CONTRACT_LAST,
                                  preferred_element_type=F32)
            araw.append((a_f, a_b))
            yield
        ke = jnp.concatenate([k * jnp.exp2(bl_f - b_f), k * jnp.exp2(bl_b - b_b)], axis=1).astype(BF16)
        dec = jnp.concatenate([jnp.exp2(bl_f), jnp.exp2(bl_b)], axis=1)
        drow = _aligned(cidx(n) * SUBLANES, SUBLANES)
        decs[pl.ds(drow, SUBLANES), :] = jnp.broadcast_to(dec, (SUBLANES, 2 * kw))
        yield
        for p in range(n_pairs):
            amat = jnp.where(keep_f, araw[p][0], araw[p][1]).astype(BF16)
            kcat = jnp.concatenate([ke[:, p * LANES:(p + 1) * LANES],
                                    ke[:, kw + p * LANES:kw + (p + 1) * LANES]], axis=1)
            incr = []
            for hh in range(2):
                h = 2 * p + hh
                vh = rd(_GV + h * GLA_DV, _GV + (h + 1) * GLA_DV)
                yield
                oi[prow(n), h * GLA_DV:(h + 1) * GLA_DV] = jnp.dot(
                    amat[hh * c:(hh + 1) * c], vh, preferred_element_type=F32)
                yield
                incr.append(lax.dot_general(vh, kcat, _CONTRACT_FIRST, preferred_element_type=F32))
            yield
            dst[cidx(n), p] = jnp.where(lo_s, incr[0], incr[1])

    def pass_c_chunk(n):
        rd = reader(n)
        for p in range(n_pairs):
            qf = qd[prow(n), p * LANES:(p + 1) * LANES]
            qb = qd[prow(n), kw + p * LANES:kw + (p + 1) * LANES]
            q2 = jnp.concatenate([_split_heads_rows(qf, lo_c), _split_heads_rows(qb, lo_c)], axis=1)
            inter = lax.dot_general(q2, dst[cidx(n), p].astype(BF16), _CONTRACT_LAST,
                                    preferred_element_type=F32)
            yield
            for hh in range(2):
                h = 2 * p + hh
                hc = slice(h * GLA_DV, (h + 1) * GLA_DV)
                o = oi[prow(n), hc] + inter[hh * c:(hh + 1) * c]
                g = rd(_GG + h * GLA_DV, _GG + (h + 1) * GLA_DV).astype(F32)
                res = (_rms(o, gain) * (g * jax.nn.sigmoid(g))).astype(BF16)
                if n is None:
                    om_ref[0, :, hc] = res[GLA_PAD:]
                else:
                    ox_ref[0, pl.ds(_aligned((n - 1) * c, c), c), hc] = res
                yield

    def run_group(make_gen, chunks, skew):
        _interleave([make_gen(n) for n in chunks], skew)

    first_group = [None] + list(range(1, GLA_GROUP + 1))
    n_groups = (n_chunks - 1) // GLA_GROUP - 1

    def group_chunks(i):
        return [1 + GLA_GROUP * (i + 1) + u for u in range(GLA_GROUP)]

    run_group(pass_a_chunk, first_group, GLA_SKEW)

    def pass_a(i, carry):
        run_group(pass_a_chunk, group_chunks(i), GLA_SKEW)
        return carry

    lax.fori_loop(0, n_groups, pass_a, 0)

    st[...] = jnp.zeros(st.shape, F32)

    def scan(it, carry):
        for n, lanes, off in ((it, slice(0, LANES), 0), (n_chunks - 1 - it, slice(LANES, 2 * LANES), kw)):
            drow = _aligned(n * SUBLANES, SUBLANES)
            for p in range(n_pairs):
                inc = dst[n, p, :, lanes]
                s_in = st[p, :, lanes]
                dst[n, p, :, lanes] = s_in
                dec = decs[pl.ds(drow, 1), off + p * LANES:off + (p + 1) * LANES]
                st[p, :, lanes] = dec * s_in + inc
        return carry

    lax.fori_loop(0, n_chunks, scan, 0)

    run_group(pass_c_chunk, first_group, 1)

    def pass_c(i, carry):
        run_group(pass_c_chunk, group_chunks(i), 1)
        return carry

    lax.fori_loop(0, n_groups, pass_c, 0)


def _gla(gl_x, gl_m, up, bias, gain):
    b, t, _ = gl_x.shape
    assert (t // GLA_CHUNK) % GLA_GROUP == 0
    n_chunks = t // GLA_CHUNK + 1
    lp = n_chunks * GLA_CHUNK
    n_pairs = GLA_HEADS // 2
    return pl.pallas_call(
        _gla_body,
        grid=(b,),
        in_specs=[
            pl.BlockSpec((1, t, GLA_COLS), lambda bi: (bi, 0, 0)),
            _const_spec(gl_m.shape),
            _const_spec(up.shape),
            _const_spec(bias.shape),
            _const_spec(gain.shape),
        ],
        out_specs=[
            pl.BlockSpec((1, t, GLA_VW), lambda bi: (bi, 0, 0)),
            pl.BlockSpec((1, N_META, GLA_VW), lambda bi: (bi, 0, 0)),
        ],
        out_shape=[
            jax.ShapeDtypeStruct((b, t, GLA_VW), BF16),
            jax.ShapeDtypeStruct((b, N_META, GLA_VW), BF16),
        ],
        scratch_shapes=[
            pltpu.VMEM((GLA_CHUNK, GLA_COLS), BF16),
            pltpu.VMEM((lp, 2 * GLA_KW), BF16),
            pltpu.VMEM((lp, GLA_VW), F32),
            pltpu.VMEM((n_chunks, n_pairs, GLA_DV, 2 * LANES), F32),
            pltpu.VMEM((n_chunks * SUBLANES, 2 * GLA_KW), F32),
            pltpu.VMEM((n_pairs, GLA_DV, 2 * LANES), F32),
        ],
        compiler_params=pltpu.CompilerParams(
            dimension_semantics=("arbitrary",), vmem_limit_bytes=VMEM_LIMIT),
        name="gla",
    )(gl_x, gl_m, up, bias, gain)


def _ffn_body(xm_ref, xp_ref, xn_ref, nam_ref, nap_ref, nan_ref, glm_ref, glp_ref, gln_ref,
              xmeta_ref, nameta_ref, glmeta_ref, wo_hbm, g0_ref, g1_ref, win_hbm, cw_ref, wout_hbm, g2_ref,
              o_ref, y_ref, perm_ref, h_ref, wo_ref, win_ref, wout_ref, st_o, st_i, st_u, sem_o, sem_i, sem_u,
              *, tm, n_tiles):
    t = pl.program_id(1)
    sl = SUBLANES
    nv = tm // sl
    n_col = perm_ref.shape[0]
    hr = HALO_ROWS

    @pl.when(jnp.logical_and(pl.program_id(0) == 0, t == 0))
    def _():
        _stream_cast(wo_hbm, wo_ref, st_o, sem_o, _row_chunks(wo_hbm.shape[0], st_o.shape[1]))
        _stream_cast(win_hbm, win_ref, st_i, sem_i, _row_chunks(win_hbm.shape[0], st_i.shape[1]))
        _stream_cast(wout_hbm, wout_ref, st_u, sem_u, _row_chunks(wout_hbm.shape[0], st_u.shape[1]))

    first = t == 0
    na = jnp.concatenate([jnp.where(first, nameta_ref[...], nap_ref[0]), nam_ref[0], nan_ref[0]],
                         axis=0)
    gl = jnp.concatenate([jnp.where(first, glmeta_ref[0], glp_ref[0]), glm_ref[0], gln_ref[0]], axis=0)
    xe = jnp.concatenate([jnp.where(first, xmeta_ref[...], xp_ref[0]), xm_ref[0], xn_ref[0]],
                         axis=0)
    mixed = (jnp.dot(na, wo_ref[:NA_WIDTH], preferred_element_type=F32)
             + jnp.dot(gl, wo_ref[NA_WIDTH:], preferred_element_type=F32))
    h1e = xe + _rms(mixed, g0_ref[...])
    h_ref[...] = h1e[hr:hr + tm]
    h_prev = h1e[hr - sl:hr]
    h_next = h1e[hr + tm:hr + tm + sl]

    pitch = perm_ref.shape[1] // sl

    def restride(x, to_permuted):
        groups = []
        if to_permuted:
            for c in range(n_col):
                for s in range(sl):
                    perm_ref[c, s * pitch:s * pitch + nv] = x[s * nv:(s + 1) * nv, c * LANES:(c + 1) * LANES]
        else:
            for c in range(n_col):
                perm_ref[c, 0:tm] = x[:, c * LANES:(c + 1) * LANES]
        for k in range(nv):
            if to_permuted:
                start, stride = k, pitch
            else:
                start, stride = sl * ((sl * k) % nv) + (sl * k) // nv, sl
            groups.append(jnp.concatenate(
                [perm_ref[c, pl.ds(start, sl, stride=stride), :] for c in range(n_col)], axis=1))
        return jnp.concatenate(groups, axis=0)

    g1 = g1_ref[...]
    n2_main = restride(_rms(h_ref[...], g1), True)
    sub = lax.broadcasted_iota(jnp.int32, (sl, 1), 0)
    slab = jnp.where(sub == 0, pltpu.roll(h_prev, 1, 0),
                     jnp.where(sub == sl - 1, pltpu.roll(h_next, sl - 1, 0), 0.0))
    keep = jnp.logical_or(sub < sl - 1, t < n_tiles - 1)
    n2 = jnp.concatenate([n2_main, jnp.where(keep, _rms(slab, g1), 0.0)], axis=0).astype(BF16)

    nb = D_FF // FF_BLK
    sub_b = lax.broadcasted_iota(jnp.int32, (sl, FF_BLK), 0)

    def proj(cb):
        va = jnp.dot(n2, win_ref[:, cb * FF_BLK:(cb + 1) * FF_BLK], preferred_element_type=F32)
        ga = jnp.dot(n2, win_ref[:, D_FF + cb * FF_BLK:D_FF + (cb + 1) * FF_BLK],
                     preferred_element_type=F32)
        return va, ga

    def conv(a, taps):
        main = a[:tm]
        hal = a[tm:]
        first_prev = jnp.where(sub_b == 0, hal, pltpu.roll(main[tm - sl:], 1, 0))
        last_next = jnp.where(sub_b == sl - 1, hal, pltpu.roll(main[:sl], sl - 1, 0))
        a_prev = jnp.concatenate([first_prev, main[:tm - sl]], axis=0)
        a_next = jnp.concatenate([main[sl:], last_next], axis=0)
        return a_prev * taps[0:1] + main * taps[1:2] + a_next * taps[2:3] + taps[3:4]

    def act(cb, va, ga):
        val = conv(va, cw_ref[:, cb * FF_BLK:(cb + 1) * FF_BLK])
        gate = conv(ga, cw_ref[:, D_FF + cb * FF_BLK:D_FF + (cb + 1) * FF_BLK])
        y_ref[:, cb * FF_BLK:(cb + 1) * FF_BLK] = (jax.nn.gelu(gate, approximate=True) * val).astype(BF16)

    pending = proj(0)
    for cb in range(nb):
        nxt = proj(cb + 1) if cb + 1 < nb else None
        act(cb, *pending)
        pending = nxt
    r = _rms(jnp.dot(y_ref[...], wout_ref[...], preferred_element_type=F32), g2_ref[...])
    o_ref[0] = h_ref[...] + restride(r, False)


def _ffn(x, o_na, o_gl, x_meta, o_na_m, o_gl_m, wo, g0, g1, win, cw, wout, g2, tm):
    b, t, d = x.shape
    n_tiles = t // tm
    hb = tm // HALO_ROWS
    last = t // HALO_ROWS - 1
    steps = WEIGHT_CAST_STEPS
    assert all(w.shape[0] % (steps * 2 * SUBLANES) == 0 for w in (wo, win, wout))

    def with_halo(width):
        return [
            pl.BlockSpec((1, tm, width), lambda bi, ti: (bi, ti, 0)),
            pl.BlockSpec((1, HALO_ROWS, width), lambda bi, ti: (bi, jnp.maximum(ti * hb - 1, 0), 0)),
            pl.BlockSpec((1, HALO_ROWS, width), lambda bi, ti: (bi, jnp.minimum((ti + 1) * hb, last), 0)),
        ]

    return pl.pallas_call(
        functools.partial(_ffn_body, tm=tm, n_tiles=n_tiles),
        grid=(b, n_tiles),
        in_specs=with_halo(d) + with_halo(NA_WIDTH) + with_halo(GLA_VW) + [
            _const_spec(x_meta.shape),
            _const_spec(o_na_m.shape),
            pl.BlockSpec((1, N_META, GLA_VW), lambda bi, ti: (bi, 0, 0)),
            pl.BlockSpec(memory_space=pl.ANY),
            _const_spec(g0.shape),
            _const_spec(g1.shape),
            pl.BlockSpec(memory_space=pl.ANY),
            _const_spec(cw.shape),
            pl.BlockSpec(memory_space=pl.ANY),
            _const_spec(g2.shape),
        ],
        out_specs=pl.BlockSpec((1, tm, d), lambda bi, ti: (bi, ti, 0)),
        out_shape=jax.ShapeDtypeStruct((b, t, d), F32),
        scratch_shapes=[pltpu.VMEM((tm, D_FF), BF16),
                        pltpu.VMEM((d // LANES, tm + SUBLANES * PERM_PITCH_PAD, LANES), F32),
                        pltpu.VMEM((tm, d), F32),
                        pltpu.VMEM(wo.shape, BF16), pltpu.VMEM(win.shape, BF16), pltpu.VMEM(wout.shape, BF16),
                        pltpu.VMEM((2, wo.shape[0] // steps, wo.shape[1]), F32),
                        pltpu.VMEM((2, win.shape[0] // steps, win.shape[1]), F32),
                        pltpu.VMEM((2, wout.shape[0] // steps, wout.shape[1]), F32),
                        pltpu.SemaphoreType.DMA((2,)), pltpu.SemaphoreType.DMA((2,)),
                        pltpu.SemaphoreType.DMA((2,))],
        compiler_params=pltpu.CompilerParams(
            dimension_semantics=("arbitrary", "arbitrary"), vmem_limit_bytes=VMEM_LIMIT),
        name="ffn",
    )(x, x, x, o_na, o_na, o_na, o_gl, o_gl, o_gl, x_meta, o_na_m, o_gl_m, wo, g0, g1, win, cw, wout, g2)


def kernel(x, meta_tokens, norm_mix_pre, w_in, na_rel_bias, na_out_gain, gla_gate_up_fwd,
           gla_gate_bias_fwd, gla_gate_up_bwd, gla_gate_bias_bwd, gla_out_gain, w_o, norm_mix_post,
           norm_ffn_pre, w_ffn_in, ffn_conv_w, ffn_conv_b, w_ffn_out, norm_ffn_post):
    b, t, d = x.shape
    depth = w_in.shape[0]
    assert depth == 1, "meta rows are only carried as far as a single layer needs them"
    assert t % GRID_W == 0 and t // GRID_W >= NA_WIN_ROWS and N_META == 2 * SUBLANES
    l = 0
    row = lambda a: a[l].reshape(1, -1).astype(F32)

    w_in_t = jnp.swapaxes(w_in[l], 0, 1)
    wo, win, wout = w_o[l], w_ffn_in[l], w_ffn_out[l]
    cw =jnp.concatenate([ffn_conv_w[l], ffn_conv_b[l][None]], axis=0).astype(F32)
    zpad = jnp.zeros((GLA_GATE_RANK, GLA_KW), BF16)
    gate_up = jnp.concatenate(
        [jnp.concatenate([gla_gate_up_fwd[l].astype(BF16), zpad], axis=1),
         jnp.concatenate([zpad, gla_gate_up_bwd[l].astype(BF16)], axis=1)], axis=0)
    gate_bias = jnp.concatenate([row(gla_gate_bias_fwd), row(gla_gate_bias_bwd)], axis=1)

    x2 = x.reshape(b * t, d)
    g_pre = row(norm_mix_pre)
    na_x, gl_x, na_m, gl_m = _inproj(x2, meta_tokens.astype(F32), g_pre, w_in_t, 1024, 2)
    na_x = na_x.reshape(b, t, NA_COLS)
    gl_x = gl_x.reshape(b, t, GLA_COLS)

    na_gain = row(na_out_gain)
    o_na, o_na_m = _na(na_x, na_m, na_rel_bias[l], na_gain, 16)
    o_gl, o_gl_m = _gla(gl_x, gl_m, gate_up, gate_bias, row(gla_out_gain))

    g_post = row(norm_mix_post)
    assert N_META == HALO_ROWS
    return _ffn(x, o_na, o_gl, meta_tokens.astype(F32), o_na_m, o_gl_m, wo, g_post, row(norm_ffn_pre), win, cw, wout,
                row(norm_ffn_post), 512)
```

```python
import functools

import jax
import jax.numpy as jnp
from jax import lax
from jax.experimental import pallas as pl
from jax.experimental.pallas import tpu as pltpu

F32 = jnp.float32
BF16 = jnp.bfloat16

N_META = 16
GRID_W = 64
NA_WIN_ROWS = 8
NA_WIN_COLS = 16
NA_HEADS = 8
NA_HEAD_DIM = 64
NA_WIDTH = NA_HEADS * NA_HEAD_DIM
GLA_HEADS = 4
GLA_DK = 64
GLA_DV = 128
GLA_KW = GLA_HEADS * GLA_DK
GLA_VW = GLA_HEADS * GLA_DV
GLA_GATE_RANK = 16
GLA_GATE_TAU = 16.0
GLA_CHUNK = 64
GLA_PAD = (-N_META) % GLA_CHUNK
NA_COLS = 3 * NA_WIDTH
GLA_COLS = 2 * GLA_KW + 2 * GLA_VW + 2 * GLA_GATE_RANK
D_FF = 2816
FF_BLK = 256
PERM_PITCH_PAD = 8
CONV_W = 3
RMS_EPS = 1e-6
MASK_NEG = -1e30
LOG2E = 1.4426950408889634
NA_Q_SCALE = NA_HEAD_DIM ** -0.5 * LOG2E
WEIGHT_CAST_STEPS = 8
NA_SKEW = 1

LANES = 128
SUBLANES = 8
HALO_ROWS = 16
VMEM_LIMIT = 56 * 1024 * 1024

_CONTRACT_LAST = (((1,), (1,)), ((), ()))
_CONTRACT_FIRST = (((0,), (0,)), ((), ()))


def _rms(x, g):
    return x * lax.rsqrt(jnp.mean(x * x, axis=-1, keepdims=True) + RMS_EPS) * g


def _aligned(v, m):
    return v if isinstance(v, int) else pl.multiple_of(v, m)


def _interleave(gens, skew, newest_first=False):
    live = [True] * len(gens)
    tick = 0
    order = list(range(len(gens)))
    if newest_first:
        order.reverse()
    while any(live):
        for u in order:
            if live[u] and tick >= u * skew:
                try:
                    next(gens[u])
                except StopIteration:
                    live[u] = False
        tick += 1


def _row_chunks(rows, max_rows):
    tile = 2 * SUBLANES
    n = -(-rows // max_rows)
    size = -(-rows // (n * tile)) * tile
    return [min(i * size, rows) for i in range(n + 1)]


def _stream_cast(src_hbm, dst, stage, sems, bounds, scale_rows=None):
    def copy(c):
        n = bounds[c + 1] - bounds[c]
        return pltpu.make_async_copy(src_hbm.at[pl.ds(bounds[c], n)], stage.at[c % 2, pl.ds(0, n)],
                                     sems.at[c % 2])

    chunks = len(bounds) - 1
    copy(0).start()
    for c in range(chunks):
        if c + 1 < chunks:
            copy(c + 1).start()
        copy(c).wait()
        n = bounds[c + 1] - bounds[c]
        w = stage[c % 2, 0:n]
        if scale_rows is not None:
            w = w * scale_rows(bounds[c], n)
        dst[bounds[c]:bounds[c + 1]] = w.astype(BF16)


def _const_spec(shape):
    nd = len(shape)
    return pl.BlockSpec(shape, lambda *_: (0,) * nd)


def _inproj_body(x_ref, xm_ref, g_ref, wt_hbm, na_ref, gl_ref, nam_ref, glm_ref, wt_ref, stage, sems, *, parts):
    @pl.when(pl.program_id(0) == 0)
    def _():
        def q_scale(r0, n):
            row = r0 + lax.broadcasted_iota(jnp.int32, (n, 1), 0)
            gla_q = jnp.logical_and(row >= NA_COLS, row < NA_COLS + GLA_KW)
            return jnp.where(row < NA_WIDTH, NA_Q_SCALE, jnp.where(gla_q, GLA_DK ** -0.5, 1.0))

        _stream_cast(wt_hbm, wt_ref, stage, sems, _row_chunks(wt_hbm.shape[0], stage.shape[1]), q_scale)

    g = g_ref[...]
    pm = x_ref.shape[0] // parts
    u = _rms(x_ref[0:pm], g).astype(BF16)
    for p in range(parts):
        rows = slice(p * pm, (p + 1) * pm)
        last = p + 1 == parts
        if last:
            u = jnp.concatenate([u, _rms(xm_ref[...], g).astype(BF16)], axis=0)
        na = lax.dot_general(u, wt_ref[:NA_COLS], _CONTRACT_LAST, preferred_element_type=F32).astype(BF16)
        u_next = None if last else _rms(x_ref[(p + 1) * pm:(p + 2) * pm], g).astype(BF16)
        gl = lax.dot_general(u, wt_ref[NA_COLS:], _CONTRACT_LAST, preferred_element_type=F32).astype(BF16)
        na_ref[rows] = na[:pm]
        gl_ref[rows] = gl[:pm]
        if last:
            nam_ref[...] = na[pm:]
            glm_ref[...] = gl[pm:]
        u = u_next


def _inproj(x2, x_meta, g, w_in_t, tm, parts):
    rows, d = x2.shape
    n_meta = x_meta.shape[0]
    stage_rows = _row_chunks(w_in_t.shape[0], w_in_t.shape[0] // 4)[1]
    return pl.pallas_call(
        functools.partial(_inproj_body, parts=parts),
        grid=(rows // tm,),
        in_specs=[
            pl.BlockSpec((tm, d), lambda i: (i, 0)),
            _const_spec(x_meta.shape),
            _const_spec(g.shape),
            pl.BlockSpec(memory_space=pl.ANY),
        ],
        out_specs=[
            pl.BlockSpec((tm, NA_COLS), lambda i: (i, 0)),
            pl.BlockSpec((tm, GLA_COLS), lambda i: (i, 0)),
            pl.BlockSpec((n_meta, NA_COLS), lambda i: (0, 0)),
            pl.BlockSpec((n_meta, GLA_COLS), lambda i: (0, 0)),
        ],
        out_shape=[
            jax.ShapeDtypeStruct((rows, NA_COLS), BF16),
            jax.ShapeDtypeStruct((rows, GLA_COLS), BF16),
            jax.ShapeDtypeStruct((n_meta, NA_COLS), BF16),
            jax.ShapeDtypeStruct((n_meta, GLA_COLS), BF16),
        ],
        scratch_shapes=[pltpu.VMEM(w_in_t.shape, BF16), pltpu.VMEM((2, stage_rows, d), F32),
                        pltpu.SemaphoreType.DMA((2,))],
        compiler_params=pltpu.CompilerParams(
            dimension_semantics=("arbitrary",), vmem_limit_bytes=VMEM_LIMIT),
        name="inproj",
    )(x2, x_meta, g, w_in_t)


def _split_heads_rows(pair, lo):
    zero = jnp.zeros_like(pair)
    return jnp.concatenate([jnp.where(lo, pair, zero), jnp.where(lo, zero, pair)], axis=0)


def _na_meta(q_ref, k_ref, v_ref, gain_ref, o_ref):
    lane = lax.broadcasted_iota(jnp.int32, (N_META, NA_WIDTH), 1)
    q = q_ref[...]
    k = k_ref[...]
    v = v_ref[...]
    om = jnp.zeros((N_META, NA_WIDTH), F32)
    for h in range(NA_HEADS):
        in_head = (lane >= h * NA_HEAD_DIM) & (lane < (h + 1) * NA_HEAD_DIM)
        qh = jnp.where(in_head, q, jnp.zeros_like(q))
        s = lax.dot_general(qh, k, _CONTRACT_LAST, preferred_element_type=F32)
        m = jnp.max(s, axis=-1, keepdims=True)
        pw = jnp.exp2(s - m)
        pw = pw / jnp.sum(pw, axis=-1, keepdims=True)
        oh = jnp.dot(pw.astype(BF16), v, preferred_element_type=F32)
        om = jnp.where(in_head, oh, om)
    o_ref[...] = _rms(om, gain_ref[...]).astype(BF16)


def _na_bias_table(base_ref, o_ref):
    w, kw = GRID_W, NA_WIN_COLS
    cq = lax.broadcasted_iota(jnp.int32, (w, 2 * w), 0)
    kk = lax.broadcasted_iota(jnp.int32, (w, 2 * w), 1) % w
    cs = jnp.clip(cq - kw // 2, 0, w - kw)
    in_win = (kk >= cs) & (kk < cs + kw)
    for h in range(o_ref.shape[0]):
        for e in range(o_ref.shape[1]):
            rows = jnp.broadcast_to(base_ref[h, e:e + 1, :], (w, 2 * w))
            shifted = pltpu.roll(rows, 2 * w - (kw - 1), 1, stride=1, stride_axis=0)
            o_ref[h, e] = jnp.where(in_win, shifted * LOG2E, MASK_NEG)


def _na_body(q_ref, k_ref, v_ref, qm_ref, km_ref, vm_ref, base_ref, gain_ref, o_ref, om_ref, t2_ref,
             *, rq, n_rows):
    j = pl.program_id(1)

    @pl.when(jnp.logical_and(pl.program_id(0) == 0, j == 0))
    def _():
        _na_bias_table(base_ref, t2_ref)
        _na_meta(qm_ref, km_ref, vm_ref, gain_ref, om_ref)

    w = GRID_W
    kh = NA_WIN_ROWS
    lo = lax.broadcasted_iota(jnp.int32, (w, LANES), 1) < NA_HEAD_DIM
    n_pairs = NA_HEADS // 2
    units = [(i, p) for i in range(rq) for p in range(n_pairs)]

    outs = {}

    def unit(i, p):
        r = j * rq + i
        rs = jnp.clip(r - kh // 2, 0, n_rows - kh)
        e0 = rs - r + (NA_WIN_ROWS - 1)
        k0 = pl.multiple_of(rs * w, w)
        cols = slice(p * LANES, (p + 1) * LANES)
        qp = q_ref[0, i * w:(i + 1) * w, cols]
        q2 = _split_heads_rows(qp, lo)
        kw = k_ref[0, pl.ds(k0, kh * w), cols]
        s = lax.dot_general(q2, kw, _CONTRACT_LAST, preferred_element_type=F32)
        bias = jnp.concatenate(
            [jnp.concatenate([t2_ref[2 * p + hh, e0 + 2 * jj] for jj in range(kh // 2)], axis=1)
             for hh in range(2)], axis=0)
        s = s + bias
        sm = lax.dot_general(q2, km_ref[:, cols], _CONTRACT_LAST, preferred_element_type=F32)
        m = jnp.maximum(jnp.max(s, axis=-1, keepdims=True), jnp.max(sm, axis=-1, keepdims=True))
        yield
        pw = jnp.exp2(s - m)
        pm = jnp.exp2(sm - m)
        l = jnp.sum(pw, axis=-1, keepdims=True) + jnp.sum(pm, axis=-1, keepdims=True)
        pw = pw.astype(BF16)
        pm = pm.astype(BF16)
        yield
        o2 = (jnp.dot(pw, v_ref[0, pl.ds(k0, kh * w), cols], preferred_element_type=F32)
              + jnp.dot(pm, vm_ref[:, cols], preferred_element_type=F32))
        o2 = o2 / l
        outs[(i, p)] = jnp.where(lo, o2[:w], o2[w:])

    def finish_row(i):
        ssq = jnp.zeros((w, 1), F32)
        for p in range(n_pairs):
            ssq = ssq + jnp.sum(outs[(i, p)] * outs[(i, p)], axis=-1, keepdims=True)
        inv = lax.rsqrt(ssq * (1.0 / NA_WIDTH) + RMS_EPS)
        for p in range(n_pairs):
            cols = slice(p * LANES, (p + 1) * LANES)
            o_ref[0, i * w:(i + 1) * w, cols] = (outs.pop((i, p)) * inv * gain_ref[:, cols]).astype(BF16)

    def unit_then_row(i, p):
        yield from unit(i, p)
        if p == n_pairs - 1:
            finish_row(i)

    _interleave([unit_then_row(i, p) for i, p in units], NA_SKEW, newest_first=True)


def _na(na_x, na_m, rpb, gain, rq):
    b, t, _ = na_x.shape
    n_rows = t // GRID_W
    nw = NA_WIDTH
    h, nr, nc = rpb.shape
    w = GRID_W
    assert 2 * w == LANES and nc <= w
    padded = jnp.pad(rpb.astype(F32), ((0, 0), (0, 0), (0, w - nc)))
    base = jnp.concatenate([padded[:, :-1], padded[:, 1:]], axis=-1)
    return pl.pallas_call(
        functools.partial(_na_body, rq=rq, n_rows=n_rows),
        grid=(b, n_rows // rq),
        in_specs=[
            pl.BlockSpec((1, rq * GRID_W, nw), lambda bi, j: (bi, j, 0)),
            pl.BlockSpec((1, t, nw), lambda bi, j: (bi, 0, 1)),
            pl.BlockSpec((1, t, nw), lambda bi, j: (bi, 0, 2)),
            pl.BlockSpec((N_META, nw), lambda bi, j: (0, 0)),
            pl.BlockSpec((N_META, nw), lambda bi, j: (0, 1)),
            pl.BlockSpec((N_META, nw), lambda bi, j: (0, 2)),
            _const_spec(base.shape),
            _const_spec(gain.shape),
        ],
        out_specs=[
            pl.BlockSpec((1, rq * GRID_W, nw), lambda bi, j: (bi, j, 0)),
            pl.BlockSpec((N_META, nw), lambda bi, j: (0, 0)),
        ],
        out_shape=[
            jax.ShapeDtypeStruct((b, t, nw), BF16),
            jax.ShapeDtypeStruct((N_META, nw), BF16),
        ],
        scratch_shapes=[pltpu.VMEM((h, nr - 1, w, 2 * w), F32)],
        compiler_params=pltpu.CompilerParams(
            dimension_semantics=("arbitrary", "arbitrary"), vmem_limit_bytes=VMEM_LIMIT),
        name="na",
    )(na_x, na_x, na_x, na_m, na_m, na_m, base, gain)


_GQ, _GK, _GV, _GG, _GZ = 0, GLA_KW, 2 * GLA_KW, 2 * GLA_KW + GLA_VW, 2 * GLA_KW + 2 * GLA_VW


GLA_GROUP = 16
GLA_SKEW = 2


def _gla_body(x_ref, m_ref, upf_ref, upb_ref, bf_ref, bb_ref, gain_ref, ox_ref, om_ref, c0, qd, oi, dst, decs, st):
    c = GLA_CHUNK
    kw = GLA_KW
    n_pairs = GLA_HEADS // 2
    t = x_ref.shape[1]
    n_chunks = t // c + 1
    c0[0:GLA_PAD, :] = jnp.zeros((GLA_PAD, GLA_COLS), BF16)
    c0[GLA_PAD:, :] = m_ref[...]

    ti = lax.broadcasted_iota(jnp.int32, (c, c), 0)
    si = lax.broadcasted_iota(jnp.int32, (c, c), 1)
    tri = jnp.where(si <= ti, 1.0, 0.0).astype(BF16)
    lo_c = lax.broadcasted_iota(jnp.int32, (c, LANES), 1) < GLA_DK
    lo_s = lax.broadcasted_iota(jnp.int32, (GLA_DV, 2 * LANES), 1) % LANES < GLA_DK
    t2 = lax.broadcasted_iota(jnp.int32, (2 * c, c), 0) % c
    s2 = lax.broadcasted_iota(jnp.int32, (2 * c, c), 1)
    keep_f = s2 <= t2
    zeros = jnp.zeros((GLA_GATE_RANK, kw), BF16)
    up = jnp.concatenate([jnp.concatenate([upf_ref[...].astype(BF16), zeros], axis=1),
                          jnp.concatenate([zeros, upb_ref[...].astype(BF16)], axis=1)], axis=0)
    bias = jnp.concatenate([bf_ref[...], bb_ref[...]], axis=1)
    gain = gain_ref[...]

    def reader(n):
        if n is None:
            return lambda a, b: c0[:, a:b]
        r0 = _aligned((n - 1) * c, c)
        return lambda a, b: x_ref[0, pl.ds(r0, c), a:b]

    def cidx(n):
        return 0 if n is None else n

    def prow(n):
        return pl.ds(_aligned(cidx(n) * c, c), c)

    def logsig_decay(gate, n):
        la = (jnp.minimum(gate, 0.0) - jnp.log1p(jnp.exp(-jnp.abs(gate)))) * (LOG2E / GLA_GATE_TAU)
        if n is None:
            la = jnp.where(lax.broadcasted_iota(jnp.int32, (c, 1), 0) >= GLA_PAD, la, 0.0)
        hi = la.astype(BF16)
        return la, hi, (la - hi.astype(F32)).astype(BF16)

    def pass_a_chunk(n):
        rd = reader(n)
        gate = jnp.dot(rd(_GZ, _GZ + 2 * GLA_GATE_RANK), up, preferred_element_type=F32) + bias
        yield
        la_f, hi_f, low_f = logsig_decay(gate[:, :kw], n)
        yield
        la_b, hi_b, low_b = logsig_decay(gate[:, kw:], n)
        yield
        cs = jnp.dot(tri, jnp.concatenate([hi_f, hi_b, low_f, low_b], axis=1), preferred_element_type=F32)
        yield
        pre = cs[:, :2 * kw] + cs[:, 2 * kw:]
        b_f = pre[:, :kw]
        bl_f = b_f[c - 1:c]
        bl_b = pre[c - 1:c, kw:]
        b_b = bl_b - pre[:, kw:] + la_b
        q = rd(_GQ, _GQ + kw).astype(F32)
        k = rd(_GK, _GK + kw).astype(F32)
        yield
        qd_f = (q * jnp.exp2(b_f)).astype(BF16)
        ki_f = (k * jnp.exp2(-b_f)).astype(BF16)
        qd[prow(n), :kw] = qd_f
        yield
        qd_b = (q * jnp.exp2(b_b)).astype(BF16)
        ki_b = (k * jnp.exp2(-b_b)).astype(BF16)
        qd[prow(n), kw:] = qd_b
        yield
        araw = []
        for p in range(n_pairs):
            cols = slice(p * LANES, (p + 1) * LANES)
            a_f = lax.dot_general(_split_heads_rows(qd_f[:, cols], lo_c), ki_f[:, cols], _CONTRACT_LAST,
                                  preferred_element_type=F32)
            yield
            a_b = lax.dot_general(_split_heads_rows(qd_b[:, cols], lo_c), ki_b[:, cols], _CONTRACT_LAST,
                                  preferred_element_type=F32)
            araw.append((a_f, a_b))
            yield
        ke = jnp.concatenate([k * jnp.exp2(bl_f - b_f), k * jnp.exp2(bl_b - b_b)], axis=1).astype(BF16)
        dec = jnp.concatenate([jnp.exp2(bl_f), jnp.exp2(bl_b)], axis=1)
        drow = _aligned(cidx(n) * SUBLANES, SUBLANES)
        decs[pl.ds(drow, SUBLANES), :] = jnp.broadcast_to(dec, (SUBLANES, 2 * kw))
        yield
        for p in range(n_pairs):
            amat = jnp.where(keep_f, araw[p][0], araw[p][1]).astype(BF16)
            kcat = jnp.concatenate([ke[:, p * LANES:(p + 1) * LANES],
                                    ke[:, kw + p * LANES:kw + (p + 1) * LANES]], axis=1)
            incr = []
            for hh in range(2):
                h = 2 * p + hh
                vh = rd(_GV + h * GLA_DV, _GV + (h + 1) * GLA_DV)
                yield
                oi[prow(n), h * GLA_DV:(h + 1) * GLA_DV] = jnp.dot(
                    amat[hh * c:(hh + 1) * c], vh, preferred_element_type=F32)
                yield
                incr.append(lax.dot_general(vh, kcat, _CONTRACT_FIRST, preferred_element_type=F32))
            yield
            dst[cidx(n), p] = jnp.where(lo_s, incr[0], incr[1])

    def pass_c_chunk(n):
        rd = reader(n)
        for p in range(n_pairs):
            qf = qd[prow(n), p * LANES:(p + 1) * LANES]
            qb = qd[prow(n), kw + p * LANES:kw + (p + 1) * LANES]
            q2 = jnp.concatenate([_split_heads_rows(qf, lo_c), _split_heads_rows(qb, lo_c)], axis=1)
            inter = lax.dot_general(q2, dst[cidx(n), p].astype(BF16), _CONTRACT_LAST,
                                    preferred_element_type=F32)
            yield
            for hh in range(2):
                h = 2 * p + hh
                hc = slice(h * GLA_DV, (h + 1) * GLA_DV)
                o = oi[prow(n), hc] + inter[hh * c:(hh + 1) * c]
                g = rd(_GG + h * GLA_DV, _GG + (h + 1) * GLA_DV).astype(F32)
                res = (_rms(o, gain) * (g * jax.nn.sigmoid(g))).astype(BF16)
                if n is None:
                    om_ref[0, :, hc] = res[GLA_PAD:]
                else:
                    ox_ref[0, pl.ds(_aligned((n - 1) * c, c), c), hc] = res
                yield

    def run_group(make_gen, chunks, skew):
        _interleave([make_gen(n) for n in chunks], skew)

    first_group = [None] + list(range(1, GLA_GROUP + 1))
    n_groups = (n_chunks - 1) // GLA_GROUP - 1

    def group_chunks(i):
        return [1 + GLA_GROUP * (i + 1) + u for u in range(GLA_GROUP)]

    run_group(pass_a_chunk, first_group, GLA_SKEW)

    def pass_a(i, carry):
        run_group(pass_a_chunk, group_chunks(i), GLA_SKEW)
        return carry

    lax.fori_loop(0, n_groups, pass_a, 0)

    st[...] = jnp.zeros(st.shape, F32)

    def scan(it, carry):
        for n, lanes, off in ((it, slice(0, LANES), 0), (n_chunks - 1 - it, slice(LANES, 2 * LANES), kw)):
            drow = _aligned(n * SUBLANES, SUBLANES)
            for p in range(n_pairs):
                inc = dst[n, p, :, lanes]
                s_in = st[p, :, lanes]
                dst[n, p, :, lanes] = s_in
                dec = decs[pl.ds(drow, 1), off + p * LANES:off + (p + 1) * LANES]
                st[p, :, lanes] = dec * s_in + inc
        return carry

    lax.fori_loop(0, n_chunks, scan, 0)

    run_group(pass_c_chunk, first_group, 1)

    def pass_c(i, carry):
        run_group(pass_c_chunk, group_chunks(i), 1)
        return carry

    lax.fori_loop(0, n_groups, pass_c, 0)


def _gla(gl_x, gl_m, up_f, up_b, bias_f, bias_b, gain):
    b, t, _ = gl_x.shape
    assert (t // GLA_CHUNK) % GLA_GROUP == 0
    n_chunks = t // GLA_CHUNK + 1
    lp = n_chunks * GLA_CHUNK
    n_pairs = GLA_HEADS // 2
    return pl.pallas_call(
        _gla_body,
        grid=(b,),
        in_specs=[
            pl.BlockSpec((1, t, GLA_COLS), lambda bi: (bi, 0, 0)),
            _const_spec(gl_m.shape),
            _const_spec(up_f.shape),
            _const_spec(up_b.shape),
            _const_spec(bias_f.shape),
            _const_spec(bias_b.shape),
            _const_spec(gain.shape),
        ],
        out_specs=[
            pl.BlockSpec((1, t, GLA_VW), lambda bi: (bi, 0, 0)),
            pl.BlockSpec((1, N_META, GLA_VW), lambda bi: (bi, 0, 0)),
        ],
        out_shape=[
            jax.ShapeDtypeStruct((b, t, GLA_VW), BF16),
            jax.ShapeDtypeStruct((b, N_META, GLA_VW), BF16),
        ],
        scratch_shapes=[
            pltpu.VMEM((GLA_CHUNK, GLA_COLS), BF16),
            pltpu.VMEM((lp, 2 * GLA_KW), BF16),
            pltpu.VMEM((lp, GLA_VW), F32),
            pltpu.VMEM((n_chunks, n_pairs, GLA_DV, 2 * LANES), F32),
            pltpu.VMEM((n_chunks * SUBLANES, 2 * GLA_KW), F32),
            pltpu.VMEM((n_pairs, GLA_DV, 2 * LANES), F32),
        ],
        compiler_params=pltpu.CompilerParams(
            dimension_semantics=("arbitrary",), vmem_limit_bytes=VMEM_LIMIT),
        name="gla",
    )(gl_x, gl_m, up_f, up_b, bias_f, bias_b, gain)


def _ffn_body(xm_ref, xp_ref, xn_ref, nam_ref, nap_ref, nan_ref, glm_ref, glp_ref, gln_ref,
              xmeta_ref, nameta_ref, glmeta_ref, wo_hbm, g0_ref, g1_ref, win_hbm, cw_ref, cb_ref, wout_hbm, g2_ref,
              o_ref, y_ref, perm_ref, h_ref, wo_ref, win_ref, wout_ref, st_o, st_i, st_u, sem_o, sem_i, sem_u,
              *, tm, n_tiles):
    t = pl.program_id(1)
    sl = SUBLANES
    nv = tm // sl
    n_col = perm_ref.shape[0]
    hr = HALO_ROWS

    @pl.when(jnp.logical_and(pl.program_id(0) == 0, t == 0))
    def _():
        _stream_cast(wo_hbm, wo_ref, st_o, sem_o, _row_chunks(wo_hbm.shape[0], st_o.shape[1]))
        _stream_cast(win_hbm, win_ref, st_i, sem_i, _row_chunks(win_hbm.shape[0], st_i.shape[1]))
        _stream_cast(wout_hbm, wout_ref, st_u, sem_u, _row_chunks(wout_hbm.shape[0], st_u.shape[1]))

    first = t == 0
    na = jnp.concatenate([jnp.where(first, nameta_ref[...], nap_ref[0]), nam_ref[0], nan_ref[0]],
                         axis=0)
    gl = jnp.concatenate([jnp.where(first, glmeta_ref[0], glp_ref[0]), glm_ref[0], gln_ref[0]], axis=0)
    xe = jnp.concatenate([jnp.where(first, xmeta_ref[...], xp_ref[0]), xm_ref[0], xn_ref[0]],
                         axis=0)
    mixed = (jnp.dot(na, wo_ref[:NA_WIDTH], preferred_element_type=F32)
             + jnp.dot(gl, wo_ref[NA_WIDTH:], preferred_element_type=F32))
    h1e = xe + _rms(mixed, g0_ref[...])
    h_ref[...] = h1e[hr:hr + tm]
    h_prev = h1e[hr - sl:hr]
    h_next = h1e[hr + tm:hr + tm + sl]

    pitch = perm_ref.shape[1] // sl

    def restride(x, to_permuted):
        groups = []
        if to_permuted:
            for c in range(n_col):
                for s in range(sl):
                    perm_ref[c, s * pitch:s * pitch + nv] = x[s * nv:(s + 1) * nv, c * LANES:(c + 1) * LANES]
        else:
            for c in range(n_col):
                perm_ref[c, 0:tm] = x[:, c * LANES:(c + 1) * LANES]
        for k in range(nv):
            if to_permuted:
                start, stride = k, pitch
            else:
                start, stride = sl * ((sl * k) % nv) + (sl * k) // nv, sl
            groups.append(jnp.concatenate(
                [perm_ref[c, pl.ds(start, sl, stride=stride), :] for c in range(n_col)], axis=1))
        return jnp.concatenate(groups, axis=0)

    g1 = g1_ref[...]
    n2_main = restride(_rms(h_ref[...], g1), True)
    sub = lax.broadcasted_iota(jnp.int32, (sl, 1), 0)
    slab = jnp.where(sub == 0, pltpu.roll(h_prev, 1, 0),
                     jnp.where(sub == sl - 1, pltpu.roll(h_next, sl - 1, 0), 0.0))
    keep = jnp.logical_or(sub < sl - 1, t < n_tiles - 1)
    n2 = jnp.concatenate([n2_main, jnp.where(keep, _rms(slab, g1), 0.0)], axis=0).astype(BF16)

    nb = D_FF // FF_BLK
    sub_b = lax.broadcasted_iota(jnp.int32, (sl, FF_BLK), 0)

    def proj(cb):
        va = jnp.dot(n2, win_ref[:, cb * FF_BLK:(cb + 1) * FF_BLK], preferred_element_type=F32)
        ga = jnp.dot(n2, win_ref[:, D_FF + cb * FF_BLK:D_FF + (cb + 1) * FF_BLK],
                     preferred_element_type=F32)
        return va, ga

    def conv(a, taps, tap_bias):
        main = a[:tm]
        hal = a[tm:]
        first_prev = jnp.where(sub_b == 0, hal, pltpu.roll(main[tm - sl:], 1, 0))
        last_next = jnp.where(sub_b == sl - 1, hal, pltpu.roll(main[:sl], sl - 1, 0))
        a_prev = jnp.concatenate([first_prev, main[:tm - sl]], axis=0)
        a_next = jnp.concatenate([main[sl:], last_next], axis=0)
        return a_prev * taps[0:1] + main * taps[1:2] + a_next * taps[2:3] + tap_bias

    def act(cb, va, ga):
        vcols = slice(cb * FF_BLK, (cb + 1) * FF_BLK)
        gcols = slice(D_FF + cb * FF_BLK, D_FF + (cb + 1) * FF_BLK)
        val = conv(va, cw_ref[:, vcols], cb_ref[:, vcols])
        gate = conv(ga, cw_ref[:, gcols], cb_ref[:, gcols])
        y_ref[:, cb * FF_BLK:(cb + 1) * FF_BLK] = (jax.nn.gelu(gate, approximate=True) * val).astype(BF16)

    pending = proj(0)
    for cb in range(nb):
        nxt = proj(cb + 1) if cb + 1 < nb else None
        act(cb, *pending)
        pending = nxt
    r = _rms(jnp.dot(y_ref[...], wout_ref[...], preferred_element_type=F32), g2_ref[...])
    o_ref[0] = h_ref[...] + restride(r, False)


def _ffn(x, o_na, o_gl, x_meta, o_na_m, o_gl_m, wo, g0, g1, win, cw, cb, wout, g2, tm):
    b, t, d = x.shape
    n_tiles = t // tm
    hb = tm // HALO_ROWS
    last = t // HALO_ROWS - 1
    steps = WEIGHT_CAST_STEPS
    assert all(w.shape[0] % (steps * 2 * SUBLANES) == 0 for w in (wo, win, wout))

    def with_halo(width):
        return [
            pl.BlockSpec((1, tm, width), lambda bi, ti: (bi, ti, 0)),
            pl.BlockSpec((1, HALO_ROWS, width), lambda bi, ti: (bi, jnp.maximum(ti * hb - 1, 0), 0)),
            pl.BlockSpec((1, HALO_ROWS, width), lambda bi, ti: (bi, jnp.minimum((ti + 1) * hb, last), 0)),
        ]

    return pl.pallas_call(
        functools.partial(_ffn_body, tm=tm, n_tiles=n_tiles),
        grid=(b, n_tiles),
        in_specs=with_halo(d) + with_halo(NA_WIDTH) + with_halo(GLA_VW) + [
            _const_spec(x_meta.shape),
            _const_spec(o_na_m.shape),
            pl.BlockSpec((1, N_META, GLA_VW), lambda bi, ti: (bi, 0, 0)),
            pl.BlockSpec(memory_space=pl.ANY),
            _const_spec(g0.shape),
            _const_spec(g1.shape),
            pl.BlockSpec(memory_space=pl.ANY),
            _const_spec(cw.shape),
            _const_spec(cb.shape),
            pl.BlockSpec(memory_space=pl.ANY),
            _const_spec(g2.shape),
        ],
        out_specs=pl.BlockSpec((1, tm, d), lambda bi, ti: (bi, ti, 0)),
        out_shape=jax.ShapeDtypeStruct((b, t, d), F32),
        scratch_shapes=[pltpu.VMEM((tm, D_FF), BF16),
                        pltpu.VMEM((d // LANES, tm + SUBLANES * PERM_PITCH_PAD, LANES), F32),
                        pltpu.VMEM((tm, d), F32),
                        pltpu.VMEM(wo.shape, BF16), pltpu.VMEM(win.shape, BF16), pltpu.VMEM(wout.shape, BF16),
                        pltpu.VMEM((2, wo.shape[0] // steps, wo.shape[1]), F32),
                        pltpu.VMEM((2, win.shape[0] // steps, win.shape[1]), F32),
                        pltpu.VMEM((2, wout.shape[0] // steps, wout.shape[1]), F32),
                        pltpu.SemaphoreType.DMA((2,)), pltpu.SemaphoreType.DMA((2,)),
                        pltpu.SemaphoreType.DMA((2,))],
        compiler_params=pltpu.CompilerParams(
            dimension_semantics=("arbitrary", "arbitrary"), vmem_limit_bytes=VMEM_LIMIT),
        name="ffn",
    )(x, x, x, o_na, o_na, o_na, o_gl, o_gl, o_gl, x_meta, o_na_m, o_gl_m, wo, g0, g1, win, cw, cb, wout, g2)


def kernel(x, meta_tokens, norm_mix_pre, w_in, na_rel_bias, na_out_gain, gla_gate_up_fwd,
           gla_gate_bias_fwd, gla_gate_up_bwd, gla_gate_bias_bwd, gla_out_gain, w_o, norm_mix_post,
           norm_ffn_pre, w_ffn_in, ffn_conv_w, ffn_conv_b, w_ffn_out, norm_ffn_post):
    b, t, d = x.shape
    depth = w_in.shape[0]
    assert depth == 1, "meta rows are only carried as far as a single layer needs them"
    assert t % GRID_W == 0 and t // GRID_W >= NA_WIN_ROWS and N_META == 2 * SUBLANES
    l = 0
    row = lambda a: a[l].reshape(1, -1).astype(F32)

    w_in_t = jnp.swapaxes(w_in[l], 0, 1)
    wo, win, wout = w_o[l], w_ffn_in[l], w_ffn_out[l]

    x2 = x.reshape(b * t, d)
    g_pre = row(norm_mix_pre)
    na_x, gl_x, na_m, gl_m = _inproj(x2, meta_tokens.astype(F32), g_pre, w_in_t, 1024, 2)
    na_x = na_x.reshape(b, t, NA_COLS)
    gl_x = gl_x.reshape(b, t, GLA_COLS)

    na_gain = row(na_out_gain)
    o_na, o_na_m = _na(na_x, na_m, na_rel_bias[l], na_gain, 16)
    o_gl, o_gl_m = _gla(gl_x, gl_m, gla_gate_up_fwd[l].astype(F32), gla_gate_up_bwd[l].astype(F32),
                        row(gla_gate_bias_fwd), row(gla_gate_bias_bwd), row(gla_out_gain))

    g_post = row(norm_mix_post)
    assert N_META == HALO_ROWS and ffn_conv_w.shape[1] == CONV_W
    return _ffn(x, o_na, o_gl, meta_tokens.astype(F32), o_na_m, o_gl_m, wo, g_post, row(norm_ffn_pre), win,
                ffn_conv_w[l].astype(F32), row(ffn_conv_b), wout,
                row(norm_ffn_post), 512)
```

```python
import functools

import jax
import jax.numpy as jnp
from jax import lax
from jax.experimental import pallas as pl
from jax.experimental.pallas import tpu as pltpu

F32 = jnp.float32
BF16 = jnp.bfloat16

N_META = 16
GRID_W = 64
NA_WIN_ROWS = 8
NA_WIN_COLS = 16
NA_HEADS = 8
NA_HEAD_DIM = 64
NA_WIDTH = NA_HEADS * NA_HEAD_DIM
GLA_HEADS = 4
GLA_DK = 64
GLA_DV = 128
GLA_KW = GLA_HEADS * GLA_DK
GLA_VW = GLA_HEADS * GLA_DV
GLA_GATE_RANK = 16
GLA_GATE_TAU = 16.0
GLA_CHUNK = 64
GLA_PAD = (-N_META) % GLA_CHUNK
NA_COLS = 3 * NA_WIDTH
GLA_COLS = 2 * GLA_KW + 2 * GLA_VW + 2 * GLA_GATE_RANK
D_FF = 2816
FF_BLK = 256
PERM_PITCH_PAD = 8
CONV_W = 3
RMS_EPS = 1e-6
MASK_NEG = -1e30
LOG2E = 1.4426950408889634
NA_Q_SCALE = NA_HEAD_DIM ** -0.5 * LOG2E
WEIGHT_CAST_STEPS = 8
NA_SKEW = 1

LANES = 128
SUBLANES = 8
HALO_ROWS = 16
VMEM_LIMIT = 56 * 1024 * 1024

_CONTRACT_LAST = (((1,), (1,)), ((), ()))
_CONTRACT_FIRST = (((0,), (0,)), ((), ()))


def _rms(x, g):
    return x * lax.rsqrt(jnp.mean(x * x, axis=-1, keepdims=True) + RMS_EPS) * g


def _aligned(v, m):
    return v if isinstance(v, int) else pl.multiple_of(v, m)


def _interleave(gens, skew, newest_first=False):
    live = [True] * len(gens)
    tick = 0
    order = list(range(len(gens)))
    if newest_first:
        order.reverse()
    while any(live):
        for u in order:
            if live[u] and tick >= u * skew:
                try:
                    next(gens[u])
                except StopIteration:
                    live[u] = False
        tick += 1


def _row_chunks(rows, max_rows):
    tile = 2 * SUBLANES
    n = -(-rows // max_rows)
    size = -(-rows // (n * tile)) * tile
    return [min(i * size, rows) for i in range(n + 1)]


def _stream_cast(src_hbm, dst, stage, sems, bounds, scale_rows=None):
    def copy(c):
        n = bounds[c + 1] - bounds[c]
        return pltpu.make_async_copy(src_hbm.at[pl.ds(bounds[c], n)], stage.at[c % 2, pl.ds(0, n)],
                                     sems.at[c % 2])

    chunks = len(bounds) - 1
    copy(0).start()
    for c in range(chunks):
        if c + 1 < chunks:
            copy(c + 1).start()
        copy(c).wait()
        n = bounds[c + 1] - bounds[c]
        w = stage[c % 2, 0:n]
        if scale_rows is not None:
            w = w * scale_rows(bounds[c], n)
        dst[bounds[c]:bounds[c + 1]] = w.astype(BF16)


def _const_spec(shape):
    nd = len(shape)
    return pl.BlockSpec(shape, lambda *_: (0,) * nd)


def _inproj_body(x_ref, xm_ref, g_ref, wt_hbm, na_ref, gl_ref, nam_ref, glm_ref, wt_ref, stage, sems, *, parts):
    @pl.when(pl.program_id(0) == 0)
    def _():
        def q_scale(r0, n):
            row = r0 + lax.broadcasted_iota(jnp.int32, (n, 1), 0)
            gla_q = jnp.logical_and(row >= NA_COLS, row < NA_COLS + GLA_KW)
            return jnp.where(row < NA_WIDTH, NA_Q_SCALE, jnp.where(gla_q, GLA_DK ** -0.5, 1.0))

        _stream_cast(wt_hbm, wt_ref, stage, sems, _row_chunks(wt_hbm.shape[0], stage.shape[1]), q_scale)

    g = g_ref[...]
    pm = x_ref.shape[0] // parts
    u = _rms(x_ref[0:pm], g).astype(BF16)
    for p in range(parts):
        rows = slice(p * pm, (p + 1) * pm)
        last = p + 1 == parts
        if last:
            u = jnp.concatenate([u, _rms(xm_ref[...], g).astype(BF16)], axis=0)
        na = lax.dot_general(u, wt_ref[:NA_COLS], _CONTRACT_LAST, preferred_element_type=F32).astype(BF16)
        u_next = None if last else _rms(x_ref[(p + 1) * pm:(p + 2) * pm], g).astype(BF16)
        gl = lax.dot_general(u, wt_ref[NA_COLS:], _CONTRACT_LAST, preferred_element_type=F32).astype(BF16)
        na_ref[rows] = na[:pm]
        gl_ref[rows] = gl[:pm]
        if last:
            nam_ref[...] = na[pm:]
            glm_ref[...] = gl[pm:]
        u = u_next


def _inproj(x2, x_meta, g, w_in_t, tm, parts):
    rows, d = x2.shape
    n_meta = x_meta.shape[0]
    stage_rows = _row_chunks(w_in_t.shape[0], w_in_t.shape[0] // 4)[1]
    return pl.pallas_call(
        functools.partial(_inproj_body, parts=parts),
        grid=(rows // tm,),
        in_specs=[
            pl.BlockSpec((tm, d), lambda i: (i, 0)),
            _const_spec(x_meta.shape),
            _const_spec(g.shape),
            pl.BlockSpec(memory_space=pl.ANY),
        ],
        out_specs=[
            pl.BlockSpec((tm, NA_COLS), lambda i: (i, 0)),
            pl.BlockSpec((tm, GLA_COLS), lambda i: (i, 0)),
            pl.BlockSpec((n_meta, NA_COLS), lambda i: (0, 0)),
            pl.BlockSpec((n_meta, GLA_COLS), lambda i: (0, 0)),
        ],
        out_shape=[
            jax.ShapeDtypeStruct((rows, NA_COLS), BF16),
            jax.ShapeDtypeStruct((rows, GLA_COLS), BF16),
            jax.ShapeDtypeStruct((n_meta, NA_COLS), BF16),
            jax.ShapeDtypeStruct((n_meta, GLA_COLS), BF16),
        ],
        scratch_shapes=[pltpu.VMEM(w_in_t.shape, BF16), pltpu.VMEM((2, stage_rows, d), F32),
                        pltpu.SemaphoreType.DMA((2,))],
        compiler_params=pltpu.CompilerParams(
            dimension_semantics=("arbitrary",), vmem_limit_bytes=VMEM_LIMIT),
        name="inproj",
    )(x2, x_meta, g, w_in_t)


def _split_heads_rows(pair, lo):
    zero = jnp.zeros_like(pair)
    return jnp.concatenate([jnp.where(lo, pair, zero), jnp.where(lo, zero, pair)], axis=0)


def _na_meta(q_ref, k_ref, v_ref, gain_ref, o_ref):
    lane = lax.broadcasted_iota(jnp.int32, (N_META, NA_WIDTH), 1)
    q = q_ref[...]
    k = k_ref[...]
    v = v_ref[...]
    om = jnp.zeros((N_META, NA_WIDTH), F32)
    for h in range(NA_HEADS):
        in_head = (lane >= h * NA_HEAD_DIM) & (lane < (h + 1) * NA_HEAD_DIM)
        qh = jnp.where(in_head, q, jnp.zeros_like(q))
        s = lax.dot_general(qh, k, _CONTRACT_LAST, preferred_element_type=F32)
        m = jnp.max(s, axis=-1, keepdims=True)
        pw = jnp.exp2(s - m)
        pw = pw / jnp.sum(pw, axis=-1, keepdims=True)
        oh = jnp.dot(pw.astype(BF16), v, preferred_element_type=F32)
        om = jnp.where(in_head, oh, om)
    o_ref[...] = _rms(om, gain_ref[...]).astype(BF16)


def _na_bias_table(base_ref, o_ref):
    w, kw = GRID_W, NA_WIN_COLS
    cq = lax.broadcasted_iota(jnp.int32, (w, 2 * w), 0)
    kk = lax.broadcasted_iota(jnp.int32, (w, 2 * w), 1) % w
    cs = jnp.clip(cq - kw // 2, 0, w - kw)
    in_win = (kk >= cs) & (kk < cs + kw)
    for h in range(o_ref.shape[0]):
        for e in range(o_ref.shape[1]):
            rows = jnp.broadcast_to(base_ref[h, e:e + 1, :], (w, 2 * w))
            shifted = pltpu.roll(rows, 2 * w - (kw - 1), 1, stride=1, stride_axis=0)
            o_ref[h, e] = jnp.where(in_win, shifted * LOG2E, MASK_NEG)


def _na_body(q_ref, k_ref, v_ref, qm_ref, km_ref, vm_ref, base_ref, gain_ref, o_ref, om_ref, t2_ref,
             *, rq, n_rows):
    j = pl.program_id(1)

    @pl.when(jnp.logical_and(pl.program_id(0) == 0, j == 0))
    def _():
        _na_bias_table(base_ref, t2_ref)
        _na_meta(qm_ref, km_ref, vm_ref, gain_ref, om_ref)

    w = GRID_W
    kh = NA_WIN_ROWS
    lo = lax.broadcasted_iota(jnp.int32, (w, LANES), 1) < NA_HEAD_DIM
    n_pairs = NA_HEADS // 2
    units = [(i, p) for i in range(rq) for p in range(n_pairs)]

    outs = {}

    def unit(i, p):
        r = j * rq + i
        rs = jnp.clip(r - kh // 2, 0, n_rows - kh)
        e0 = rs - r + (NA_WIN_ROWS - 1)
        k0 = pl.multiple_of(rs * w, w)
        cols = slice(p * LANES, (p + 1) * LANES)
        qp = q_ref[0, i * w:(i + 1) * w, cols]
        q2 = _split_heads_rows(qp, lo)
        kw = k_ref[0, pl.ds(k0, kh * w), cols]
        s = lax.dot_general(q2, kw, _CONTRACT_LAST, preferred_element_type=F32)
        bias = jnp.concatenate(
            [jnp.concatenate([t2_ref[2 * p + hh, e0 + 2 * jj] for jj in range(kh // 2)], axis=1)
             for hh in range(2)], axis=0)
        s = s + bias
        sm = lax.dot_general(q2, km_ref[:, cols], _CONTRACT_LAST, preferred_element_type=F32)
        m = jnp.maximum(jnp.max(s, axis=-1, keepdims=True), jnp.max(sm, axis=-1, keepdims=True))
        yield
        pw = jnp.exp2(s - m)
        pm = jnp.exp2(sm - m)
        l = jnp.sum(pw, axis=-1, keepdims=True) + jnp.sum(pm, axis=-1, keepdims=True)
        pw = pw.astype(BF16)
        pm = pm.astype(BF16)
        yield
        o2 = (jnp.dot(pw, v_ref[0, pl.ds(k0, kh * w), cols], preferred_element_type=F32)
              + jnp.dot(pm, vm_ref[:, cols], preferred_element_type=F32))
        o2 = o2 / l
        outs[(i, p)] = jnp.where(lo, o2[:w], o2[w:])

    def finish_row(i):
        ssq = jnp.zeros((w, 1), F32)
        for p in range(n_pairs):
            ssq = ssq + jnp.sum(outs[(i, p)] * outs[(i, p)], axis=-1, keepdims=True)
        inv = lax.rsqrt(ssq * (1.0 / NA_WIDTH) + RMS_EPS)
        for p in range(n_pairs):
            cols = slice(p * LANES, (p + 1) * LANES)
            o_ref[0, i * w:(i + 1) * w, cols] = (outs.pop((i, p)) * inv * gain_ref[:, cols]).astype(BF16)

    def unit_then_row(i, p):
        yield from unit(i, p)
        if p == n_pairs - 1:
            finish_row(i)

    _interleave([unit_then_row(i, p) for i, p in units], NA_SKEW, newest_first=True)


def _na(na_x, na_m, rpb, gain, rq):
    b, t, _ = na_x.shape
    n_rows = t // GRID_W
    nw = NA_WIDTH
    h, nr, nc = rpb.shape
    w = GRID_W
    assert 2 * w == LANES and nc <= w
    padded = jnp.pad(rpb.astype(F32), ((0, 0), (0, 0), (0, w - nc)))
    base = jnp.concatenate([padded[:, :-1], padded[:, 1:]], axis=-1)
    return pl.pallas_call(
        functools.partial(_na_body, rq=rq, n_rows=n_rows),
        grid=(b, n_rows // rq),
        in_specs=[
            pl.BlockSpec((1, rq * GRID_W, nw), lambda bi, j: (bi, j, 0)),
            pl.BlockSpec((1, t, nw), lambda bi, j: (bi, 0, 1)),
            pl.BlockSpec((1, t, nw), lambda bi, j: (bi, 0, 2)),
            pl.BlockSpec((N_META, nw), lambda bi, j: (0, 0)),
            pl.BlockSpec((N_META, nw), lambda bi, j: (0, 1)),
            pl.BlockSpec((N_META, nw), lambda bi, j: (0, 2)),
            _const_spec(base.shape),
            _const_spec(gain.shape),
        ],
        out_specs=[
            pl.BlockSpec((1, rq * GRID_W, nw), lambda bi, j: (bi, j, 0)),
            pl.BlockSpec((N_META, nw), lambda bi, j: (0, 0)),
        ],
        out_shape=[
            jax.ShapeDtypeStruct((b, t, nw), BF16),
            jax.ShapeDtypeStruct((N_META, nw), BF16),
        ],
        scratch_shapes=[pltpu.VMEM((h, nr - 1, w, 2 * w), F32)],
        compiler_params=pltpu.CompilerParams(
            dimension_semantics=("arbitrary", "arbitrary"), vmem_limit_bytes=VMEM_LIMIT),
        name="na",
    )(na_x, na_x, na_x, na_m, na_m, na_m, base, gain)


_GQ, _GK, _GV, _GG, _GZ = 0, GLA_KW, 2 * GLA_KW, 2 * GLA_KW + GLA_VW, 2 * GLA_KW + 2 * GLA_VW


GLA_GROUP = 16
GLA_SCAN_UNROLL = 3
GLA_SKEW = 2


def _gla_body(x_ref, m_ref, upf_ref, upb_ref, bf_ref, bb_ref, gain_ref, ox_ref, om_ref, c0, qd, oi, dst, decs):
    c = GLA_CHUNK
    kw = GLA_KW
    n_pairs = GLA_HEADS // 2
    t = x_ref.shape[1]
    n_chunks = t // c + 1
    c0[0:GLA_PAD, :] = jnp.zeros((GLA_PAD, GLA_COLS), BF16)
    c0[GLA_PAD:, :] = m_ref[...]

    ti = lax.broadcasted_iota(jnp.int32, (c, c), 0)
    si = lax.broadcasted_iota(jnp.int32, (c, c), 1)
    tri = jnp.where(si <= ti, 1.0, 0.0).astype(BF16)
    lo_c = lax.broadcasted_iota(jnp.int32, (c, LANES), 1) < GLA_DK
    lo_s = lax.broadcasted_iota(jnp.int32, (GLA_DV, 2 * LANES), 1) % LANES < GLA_DK
    t2 = lax.broadcasted_iota(jnp.int32, (2 * c, c), 0) % c
    s2 = lax.broadcasted_iota(jnp.int32, (2 * c, c), 1)
    keep_f = s2 <= t2
    zeros = jnp.zeros((GLA_GATE_RANK, kw), BF16)
    up = jnp.concatenate([jnp.concatenate([upf_ref[...].astype(BF16), zeros], axis=1),
                          jnp.concatenate([zeros, upb_ref[...].astype(BF16)], axis=1)], axis=0)
    bias = jnp.concatenate([bf_ref[...], bb_ref[...]], axis=1)
    gain = gain_ref[...]

    def reader(n):
        if n is None:
            return lambda a, b: c0[:, a:b]
        r0 = _aligned((n - 1) * c, c)
        return lambda a, b: x_ref[0, pl.ds(r0, c), a:b]

    def cidx(n):
        return 0 if n is None else n

    def prow(n):
        return pl.ds(_aligned(cidx(n) * c, c), c)

    def logsig_decay(gate, n):
        la = (jnp.minimum(gate, 0.0) - jnp.log1p(jnp.exp(-jnp.abs(gate)))) * (LOG2E / GLA_GATE_TAU)
        if n is None:
            la = jnp.where(lax.broadcasted_iota(jnp.int32, (c, 1), 0) >= GLA_PAD, la, 0.0)
        hi = la.astype(BF16)
        return la, hi, (la - hi.astype(F32)).astype(BF16)

    def pass_a_chunk(n):
        rd = reader(n)
        gate = jnp.dot(rd(_GZ, _GZ + 2 * GLA_GATE_RANK), up, preferred_element_type=F32) + bias
        yield
        la_f, hi_f, low_f = logsig_decay(gate[:, :kw], n)
        yield
        la_b, hi_b, low_b = logsig_decay(gate[:, kw:], n)
        yield
        cs = jnp.dot(tri, jnp.concatenate([hi_f, hi_b, low_f, low_b], axis=1), preferred_element_type=F32)
        yield
        pre = cs[:, :2 * kw] + cs[:, 2 * kw:]
        b_f = pre[:, :kw]
        bl_f = b_f[c - 1:c]
        bl_b = pre[c - 1:c, kw:]
        b_b = bl_b - pre[:, kw:] + la_b
        q = rd(_GQ, _GQ + kw).astype(F32)
        k = rd(_GK, _GK + kw).astype(F32)
        yield
        qd_f = (q * jnp.exp2(b_f)).astype(BF16)
        ki_f = (k * jnp.exp2(-b_f)).astype(BF16)
        qd[prow(n), :kw] = qd_f
        yield
        qd_b = (q * jnp.exp2(b_b)).astype(BF16)
        ki_b = (k * jnp.exp2(-b_b)).astype(BF16)
        qd[prow(n), kw:] = qd_b
        yield
        araw = []
        for p in range(n_pairs):
            cols = slice(p * LANES, (p + 1) * LANES)
            a_f = lax.dot_general(_split_heads_rows(qd_f[:, cols], lo_c), ki_f[:, cols], _CONTRACT_LAST,
                                  preferred_element_type=F32)
            yield
            a_b = lax.dot_general(_split_heads_rows(qd_b[:, cols], lo_c), ki_b[:, cols], _CONTRACT_LAST,
                                  preferred_element_type=F32)
            araw.append((a_f, a_b))
            yield
        ke = jnp.concatenate([k * jnp.exp2(bl_f - b_f), k * jnp.exp2(bl_b - b_b)], axis=1).astype(BF16)
        dec = jnp.concatenate([jnp.exp2(bl_f), jnp.exp2(bl_b)], axis=1)
        drow = _aligned(cidx(n) * SUBLANES, SUBLANES)
        decs[pl.ds(drow, SUBLANES), :] = jnp.broadcast_to(dec, (SUBLANES, 2 * kw))
        yield
        for p in range(n_pairs):
            amat = jnp.where(keep_f, araw[p][0], araw[p][1]).astype(BF16)
            kcat = jnp.concatenate([ke[:, p * LANES:(p + 1) * LANES],
                                    ke[:, kw + p * LANES:kw + (p + 1) * LANES]], axis=1)
            incr = []
            for hh in range(2):
                h = 2 * p + hh
                vh = rd(_GV + h * GLA_DV, _GV + (h + 1) * GLA_DV)
                yield
                oi[prow(n), h * GLA_DV:(h + 1) * GLA_DV] = jnp.dot(
                    amat[hh * c:(hh + 1) * c], vh, preferred_element_type=F32)
                yield
                incr.append(lax.dot_general(vh, kcat, _CONTRACT_FIRST, preferred_element_type=F32))
            yield
            dst[cidx(n), p] = jnp.where(lo_s, incr[0], incr[1])

    def pass_c_chunk(n):
        rd = reader(n)
        for p in range(n_pairs):
            qf = qd[prow(n), p * LANES:(p + 1) * LANES]
            qb = qd[prow(n), kw + p * LANES:kw + (p + 1) * LANES]
            q2 = jnp.concatenate([_split_heads_rows(qf, lo_c), _split_heads_rows(qb, lo_c)], axis=1)
            inter = lax.dot_general(q2, dst[cidx(n), p].astype(BF16), _CONTRACT_LAST,
                                    preferred_element_type=F32)
            yield
            for hh in range(2):
                h = 2 * p + hh
                hc = slice(h * GLA_DV, (h + 1) * GLA_DV)
                o = oi[prow(n), hc] + inter[hh * c:(hh + 1) * c]
                g = rd(_GG + h * GLA_DV, _GG + (h + 1) * GLA_DV).astype(F32)
                res = (_rms(o, gain) * (g * jax.nn.sigmoid(g))).astype(BF16)
                if n is None:
                    om_ref[0, :, hc] = res[GLA_PAD:]
                else:
                    ox_ref[0, pl.ds(_aligned((n - 1) * c, c), c), hc] = res
                yield

    def run_group(make_gen, chunks, skew):
        _interleave([make_gen(n) for n in chunks], skew)

    first_group = [None] + list(range(1, GLA_GROUP + 1))
    n_groups = (n_chunks - 1) // GLA_GROUP - 1

    def group_chunks(i):
        return [1 + GLA_GROUP * (i + 1) + u for u in range(GLA_GROUP)]

    run_group(pass_a_chunk, first_group, GLA_SKEW)

    def pass_a(i, carry):
        run_group(pass_a_chunk, group_chunks(i), GLA_SKEW)
        return carry

    lax.fori_loop(0, n_groups, pass_a, 0)

    def scan(backward):
        lanes = slice(LANES, 2 * LANES) if backward else slice(0, LANES)
        off = kw if backward else 0

        def step(it, states):
            n = n_chunks - 1 - it if backward else it
            drow = _aligned(n * SUBLANES, SUBLANES)
            out = []
            for p in range(n_pairs):
                inc = dst[n, p, :, lanes]
                dst[n, p, :, lanes] = states[p]
                dec = decs[pl.ds(drow, 1), off + p * LANES:off + (p + 1) * LANES]
                out.append(dec * states[p] + inc)
            return tuple(out)

        zero = jnp.zeros((GLA_DV, LANES), F32)
        lax.fori_loop(0, n_chunks, step, (zero,) * n_pairs, unroll=GLA_SCAN_UNROLL)

    scan(False)
    scan(True)

    run_group(pass_c_chunk, first_group, 1)

    def pass_c(i, carry):
        run_group(pass_c_chunk, group_chunks(i), 1)
        return carry

    lax.fori_loop(0, n_groups, pass_c, 0)


def _gla(gl_x, gl_m, up_f, up_b, bias_f, bias_b, gain):
    b, t, _ = gl_x.shape
    assert (t // GLA_CHUNK) % GLA_GROUP == 0
    n_chunks = t // GLA_CHUNK + 1
    lp = n_chunks * GLA_CHUNK
    n_pairs = GLA_HEADS // 2
    return pl.pallas_call(
        _gla_body,
        grid=(b,),
        in_specs=[
            pl.BlockSpec((1, t, GLA_COLS), lambda bi: (bi, 0, 0)),
            _const_spec(gl_m.shape),
            _const_spec(up_f.shape),
            _const_spec(up_b.shape),
            _const_spec(bias_f.shape),
            _const_spec(bias_b.shape),
            _const_spec(gain.shape),
        ],
        out_specs=[
            pl.BlockSpec((1, t, GLA_VW), lambda bi: (bi, 0, 0)),
            pl.BlockSpec((1, N_META, GLA_VW), lambda bi: (bi, 0, 0)),
        ],
        out_shape=[
            jax.ShapeDtypeStruct((b, t, GLA_VW), BF16),
            jax.ShapeDtypeStruct((b, N_META, GLA_VW), BF16),
        ],
        scratch_shapes=[
            pltpu.VMEM((GLA_CHUNK, GLA_COLS), BF16),
            pltpu.VMEM((lp, 2 * GLA_KW), BF16),
            pltpu.VMEM((lp, GLA_VW), F32),
            pltpu.VMEM((n_chunks, n_pairs, GLA_DV, 2 * LANES), F32),
            pltpu.VMEM((n_chunks * SUBLANES, 2 * GLA_KW), F32),
        ],
        compiler_params=pltpu.CompilerParams(
            dimension_semantics=("arbitrary",), vmem_limit_bytes=VMEM_LIMIT),
        name="gla",
    )(gl_x, gl_m, up_f, up_b, bias_f, bias_b, gain)


def _ffn_body(xm_ref, xp_ref, xn_ref, nam_ref, nap_ref, nan_ref, glm_ref, glp_ref, gln_ref,
              xmeta_ref, nameta_ref, glmeta_ref, wo_hbm, g0_ref, g1_ref, win_hbm, cw_ref, cb_ref, wout_hbm, g2_ref,
              o_ref, y_ref, perm_ref, h_ref, wo_ref, win_ref, wout_ref, st_o, st_i, st_u, sem_o, sem_i, sem_u,
              *, tm, n_tiles):
    t = pl.program_id(1)
    sl = SUBLANES
    nv = tm // sl
    n_col = perm_ref.shape[0]
    hr = HALO_ROWS

    @pl.when(jnp.logical_and(pl.program_id(0) == 0, t == 0))
    def _():
        _stream_cast(wo_hbm, wo_ref, st_o, sem_o, _row_chunks(wo_hbm.shape[0], st_o.shape[1]))
        _stream_cast(win_hbm, win_ref, st_i, sem_i, _row_chunks(win_hbm.shape[0], st_i.shape[1]))
        _stream_cast(wout_hbm, wout_ref, st_u, sem_u, _row_chunks(wout_hbm.shape[0], st_u.shape[1]))

    first = t == 0
    na = jnp.concatenate([jnp.where(first, nameta_ref[...], nap_ref[0]), nam_ref[0], nan_ref[0]],
                         axis=0)
    gl = jnp.concatenate([jnp.where(first, glmeta_ref[0], glp_ref[0]), glm_ref[0], gln_ref[0]], axis=0)
    xe = jnp.concatenate([jnp.where(first, xmeta_ref[...], xp_ref[0]), xm_ref[0], xn_ref[0]],
                         axis=0)
    mixed = (jnp.dot(na, wo_ref[:NA_WIDTH], preferred_element_type=F32)
             + jnp.dot(gl, wo_ref[NA_WIDTH:], preferred_element_type=F32))
    h1e = xe + _rms(mixed, g0_ref[...])
    h_ref[...] = h1e[hr:hr + tm]
    h_prev = h1e[hr - sl:hr]
    h_next = h1e[hr + tm:hr + tm + sl]

    pitch = perm_ref.shape[1] // sl

    def restride(x, to_permuted):
        groups = []
        if to_permuted:
            for c in range(n_col):
                for s in range(sl):
                    perm_ref[c, s * pitch:s * pitch + nv] = x[s * nv:(s + 1) * nv, c * LANES:(c + 1) * LANES]
        else:
            for c in range(n_col):
                perm_ref[c, 0:tm] = x[:, c * LANES:(c + 1) * LANES]
        for k in range(nv):
            if to_permuted:
                start, stride = k, pitch
            else:
                start, stride = sl * ((sl * k) % nv) + (sl * k) // nv, sl
            groups.append(jnp.concatenate(
                [perm_ref[c, pl.ds(start, sl, stride=stride), :] for c in range(n_col)], axis=1))
        return jnp.concatenate(groups, axis=0)

    g1 = g1_ref[...]
    n2_main = restride(_rms(h_ref[...], g1), True)
    sub = lax.broadcasted_iota(jnp.int32, (sl, 1), 0)
    slab = jnp.where(sub == 0, pltpu.roll(h_prev, 1, 0),
                     jnp.where(sub == sl - 1, pltpu.roll(h_next, sl - 1, 0), 0.0))
    keep = jnp.logical_or(sub < sl - 1, t < n_tiles - 1)
    n2 = jnp.concatenate([n2_main, jnp.where(keep, _rms(slab, g1), 0.0)], axis=0).astype(BF16)

    nb = D_FF // FF_BLK
    sub_b = lax.broadcasted_iota(jnp.int32, (sl, FF_BLK), 0)

    def proj(cb):
        va = jnp.dot(n2, win_ref[:, cb * FF_BLK:(cb + 1) * FF_BLK], preferred_element_type=F32)
        ga = jnp.dot(n2, win_ref[:, D_FF + cb * FF_BLK:D_FF + (cb + 1) * FF_BLK],
                     preferred_element_type=F32)
        return va, ga

    def conv(a, taps, tap_bias):
        main = a[:tm]
        hal = a[tm:]
        first_prev = jnp.where(sub_b == 0, hal, pltpu.roll(main[tm - sl:], 1, 0))
        last_next = jnp.where(sub_b == sl - 1, hal, pltpu.roll(main[:sl], sl - 1, 0))
        a_prev = jnp.concatenate([first_prev, main[:tm - sl]], axis=0)
        a_next = jnp.concatenate([main[sl:], last_next], axis=0)
        return a_prev * taps[0:1] + main * taps[1:2] + a_next * taps[2:3] + tap_bias

    def act(cb, va, ga):
        vcols = slice(cb * FF_BLK, (cb + 1) * FF_BLK)
        gcols = slice(D_FF + cb * FF_BLK, D_FF + (cb + 1) * FF_BLK)
        val = conv(va, cw_ref[:, vcols], cb_ref[:, vcols])
        gate = conv(ga, cw_ref[:, gcols], cb_ref[:, gcols])
        y_ref[:, cb * FF_BLK:(cb + 1) * FF_BLK] = (jax.nn.gelu(gate, approximate=True) * val).astype(BF16)

    pending = proj(0)
    for cb in range(nb):
        nxt = proj(cb + 1) if cb + 1 < nb else None
        act(cb, *pending)
        pending = nxt
    r = _rms(jnp.dot(y_ref[...], wout_ref[...], preferred_element_type=F32), g2_ref[...])
    o_ref[0] = h_ref[...] + restride(r, False)


def _ffn(x, o_na, o_gl, x_meta, o_na_m, o_gl_m, wo, g0, g1, win, cw, cb, wout, g2, tm):
    b, t, d = x.shape
    n_tiles = t // tm
    hb = tm // HALO_ROWS
    last = t // HALO_ROWS - 1
    steps = WEIGHT_CAST_STEPS
    assert all(w.shape[0] % (steps * 2 * SUBLANES) == 0 for w in (wo, win, wout))

    def with_halo(width):
        return [
            pl.BlockSpec((1, tm, width), lambda bi, ti: (bi, ti, 0)),
            pl.BlockSpec((1, HALO_ROWS, width), lambda bi, ti: (bi, jnp.maximum(ti * hb - 1, 0), 0)),
            pl.BlockSpec((1, HALO_ROWS, width), lambda bi, ti: (bi, jnp.minimum((ti + 1) * hb, last), 0)),
        ]

    return pl.pallas_call(
        functools.partial(_ffn_body, tm=tm, n_tiles=n_tiles),
        grid=(b, n_tiles),
        in_specs=with_halo(d) + with_halo(NA_WIDTH) + with_halo(GLA_VW) + [
            _const_spec(x_meta.shape),
            _const_spec(o_na_m.shape),
            pl.BlockSpec((1, N_META, GLA_VW), lambda bi, ti: (bi, 0, 0)),
            pl.BlockSpec(memory_space=pl.ANY),
            _const_spec(g0.shape),
            _const_spec(g1.shape),
            pl.BlockSpec(memory_space=pl.ANY),
            _const_spec(cw.shape),
            _const_spec(cb.shape),
            pl.BlockSpec(memory_space=pl.ANY),
            _const_spec(g2.shape),
        ],
        out_specs=pl.BlockSpec((1, tm, d), lambda bi, ti: (bi, ti, 0)),
        out_shape=jax.ShapeDtypeStruct((b, t, d), F32),
        scratch_shapes=[pltpu.VMEM((tm, D_FF), BF16),
                        pltpu.VMEM((d // LANES, tm + SUBLANES * PERM_PITCH_PAD, LANES), F32),
                        pltpu.VMEM((tm, d), F32),
                        pltpu.VMEM(wo.shape, BF16), pltpu.VMEM(win.shape, BF16), pltpu.VMEM(wout.shape, BF16),
                        pltpu.VMEM((2, wo.shape[0] // steps, wo.shape[1]), F32),
                        pltpu.VMEM((2, win.shape[0] // steps, win.shape[1]), F32),
                        pltpu.VMEM((2, wout.shape[0] // steps, wout.shape[1]), F32),
                        pltpu.SemaphoreType.DMA((2,)), pltpu.SemaphoreType.DMA((2,)),
                        pltpu.SemaphoreType.DMA((2,))],
        compiler_params=pltpu.CompilerParams(
            dimension_semantics=("arbitrary", "arbitrary"), vmem_limit_bytes=VMEM_LIMIT),
        name="ffn",
    )(x, x, x, o_na, o_na, o_na, o_gl, o_gl, o_gl, x_meta, o_na_m, o_gl_m, wo, g0, g1, win, cw, cb, wout, g2)


def kernel(x, meta_tokens, norm_mix_pre, w_in, na_rel_bias, na_out_gain, gla_gate_up_fwd,
           gla_gate_bias_fwd, gla_gate_up_bwd, gla_gate_bias_bwd, gla_out_gain, w_o, norm_mix_post,
           norm_ffn_pre, w_ffn_in, ffn_conv_w, ffn_conv_b, w_ffn_out, norm_ffn_post):
    b, t, d = x.shape
    depth = w_in.shape[0]
    assert depth == 1, "meta rows are only carried as far as a single layer needs them"
    assert t % GRID_W == 0 and t // GRID_W >= NA_WIN_ROWS and N_META == 2 * SUBLANES
    l = 0
    row = lambda a: a[l].reshape(1, -1).astype(F32)

    w_in_t = jnp.swapaxes(w_in[l], 0, 1)
    wo, win, wout = w_o[l], w_ffn_in[l], w_ffn_out[l]

    x2 = x.reshape(b * t, d)
    g_pre = row(norm_mix_pre)
    na_x, gl_x, na_m, gl_m = _inproj(x2, meta_tokens.astype(F32), g_pre, w_in_t, 1024, 2)
    na_x = na_x.reshape(b, t, NA_COLS)
    gl_x = gl_x.reshape(b, t, GLA_COLS)

    na_gain = row(na_out_gain)
    o_na, o_na_m = _na(na_x, na_m, na_rel_bias[l], na_gain, 16)
    o_gl, o_gl_m = _gla(gl_x, gl_m, gla_gate_up_fwd[l].astype(F32), gla_gate_up_bwd[l].astype(F32),
                        row(gla_gate_bias_fwd), row(gla_gate_bias_bwd), row(gla_out_gain))

    g_post = row(norm_mix_post)
    assert N_META == HALO_ROWS and ffn_conv_w.shape[1] == CONV_W
    return _ffn(x, o_na, o_gl, meta_tokens.astype(F32), o_na_m, o_gl_m, wo, g_post, row(norm_ffn_pre), win,
                ffn_conv_w[l].astype(F32), row(ffn_conv_b), wout,
                row(norm_ffn_post), 512)
```
